```python
import jax, jax.numpy as jnp
from jax import lax
import numpy as np

D_MODEL = 1024
BATCH = 8
SEQ = 4096
DEPTH = 2

HEAD_DIM = 64
ATT_HEADS = 8
ATT_WIDTH = ATT_HEADS * HEAD_DIM
LRU_BLOCKS = 4
LRU_WIDTH = LRU_BLOCKS * HEAD_DIM
RWKV_HEADS = 4
RWKV_WIDTH = RWKV_HEADS * HEAD_DIM
MIX_WIDTH = ATT_WIDTH + LRU_WIDTH + RWKV_WIDTH
ROPE_DIMS = HEAD_DIM // 4
ROPE_THETA = 500000.0
MOBA_BLOCK = 256
MOBA_TOPK = 3
MOBA_Q_CHUNK = 32
CONV_WIDTH = 4
LRU_C = 8.0
RWKV_DECAY_RANK = 64
RWKV_A_RANK = 64
RWKV_GATE_RANK = 128
RWKV_LN_EPS = 64e-5
RWKV_PROJ = 3 * RWKV_WIDTH + RWKV_DECAY_RANK + RWKV_A_RANK + RWKV_GATE_RANK
IN_WIDTH = 3 * ATT_WIDTH + 2 * LRU_WIDTH + RWKV_PROJ
IN_SPLITS = [ATT_WIDTH, 2 * ATT_WIDTH, 3 * ATT_WIDTH,
             3 * ATT_WIDTH + LRU_WIDTH, 3 * ATT_WIDTH + 2 * LRU_WIDTH]
RWKV_SPLITS = [RWKV_WIDTH, 2 * RWKV_WIDTH, 3 * RWKV_WIDTH,
               3 * RWKV_WIDTH + RWKV_DECAY_RANK, 3 * RWKV_WIDTH + RWKV_DECAY_RANK + RWKV_A_RANK]
D_FF = 2816
N_EXPERTS = 8
TOP_K = 2
D_EXPERT = 3584
MOE_BLOCK = 512
LN_EPS = 1e-5
DEEPNORM_ALPHA = (2 * DEPTH) ** 0.25
DEEPNORM_BETA = (8 * DEPTH) ** -0.25
N_DENSE = (DEPTH + 1) // 2
N_MOE = DEPTH // 2
NEG_INF = -1e30

kernel_name = "hybrid_moba_rglru_rwkv7_deepnorm"


def layer_norm(x, g, b):
    xf = x.astype(jnp.float32)
    m = xf.mean(-1, keepdims=True)
    var = jnp.square(xf - m).mean(-1, keepdims=True)
    return ((xf - m) * lax.rsqrt(var + LN_EPS) * g + b).astype(x.dtype)


def partial_rotary(x, cos, sin):
    half = ROPE_DIMS // 2
    x1 = x[..., :half]
    x2 = x[..., half:ROPE_DIMS]
    return jnp.concatenate([x1 * cos - x2 * sin, x2 * cos + x1 * sin, x[..., ROPE_DIMS:]], axis=-1)


def moba_attention(q, k, v):
    B, H, S, Dh = q.shape
    nb = max(-(-S // MOBA_BLOCK), MOBA_TOPK)
    pad = nb * MOBA_BLOCK - S
    kp = jnp.pad(k, ((0, 0), (0, 0), (0, pad), (0, 0)))
    vp = jnp.pad(v, ((0, 0), (0, 0), (0, pad), (0, 0)))
    kb = kp.reshape(B, H, nb, MOBA_BLOCK, Dh)
    vb = vp.reshape(B, H, nb, MOBA_BLOCK, Dh)
    k_mean = kb.astype(jnp.float32).mean(axis=3).astype(q.dtype)
    scale = Dh ** -0.5
    bi = jnp.arange(B)[:, None, None, None]
    hi = jnp.arange(H)[None, :, None, None]
    n_sel = MOBA_TOPK * MOBA_BLOCK

    def chunk(c):
        start = c * MOBA_Q_CHUNK
        qblk = start // MOBA_BLOCK
        qc = lax.dynamic_slice_in_dim(q, start, MOBA_Q_CHUNK, axis=2)
        gate = jnp.einsum('bhqd,bhnd->bhqn', qc, k_mean).astype(jnp.float32)
        gate = jnp.where(jnp.arange(nb) < qblk, gate, -jnp.inf)
        _, sel = lax.top_k(gate, MOBA_TOPK)
        slot_ok = jnp.arange(MOBA_TOPK) < qblk
        k_sel = kb[bi, hi, sel]
        v_sel = vb[bi, hi, sel]
        s_sel = jnp.einsum('bhqd,bhqkld->bhqkl', qc, k_sel).astype(jnp.float32) * scale
        s_sel = jnp.where(slot_ok[:, None], s_sel, NEG_INF)
        own = qblk * MOBA_BLOCK
        k_own = lax.dynamic_slice_in_dim(kp, own, MOBA_BLOCK, axis=2)
        v_own = lax.dynamic_slice_in_dim(vp, own, MOBA_BLOCK, axis=2)
        s_own = jnp.einsum('bhqd,bhld->bhql', qc, k_own).astype(jnp.float32) * scale
        causal = (own + jnp.arange(MOBA_BLOCK))[None, :] <= (start + jnp.arange(MOBA_Q_CHUNK))[:, None]
        s_own = jnp.where(causal, s_own, NEG_INF)
        logits = jnp.concatenate([s_sel.reshape(B, H, MOBA_Q_CHUNK, n_sel), s_own], axis=-1)
        p = jax.nn.softmax(logits, axis=-1).astype(v.dtype)
        p_sel = p[..., :n_sel].reshape(B, H, MOBA_Q_CHUNK, MOBA_TOPK, MOBA_BLOCK)
        return (jnp.einsum('bhqkl,bhqkld->bhqd', p_sel, v_sel)
                + jnp.einsum('bhql,bhld->bhqd', p[..., n_sel:], v_own))

    out = lax.map(chunk, jnp.arange(S // MOBA_Q_CHUNK))
    return out.transpose(1, 2, 0, 3, 4).reshape(B, H, S, Dh)


def rglru_mixer(xb, gb, conv_w, conv_b, ga_w, ga_b, gx_w, gx_b, lam):
    B, S, _ = xb.shape
    xp = jnp.pad(xb, ((0, 0), (CONV_WIDTH - 1, 0), (0, 0)))
    xc = conv_b
    for j in range(CONV_WIDTH):
        xc = xc + xp[:, j:j + S] * conv_w[j]
    xh = xc.reshape(B, S, LRU_BLOCKS, HEAD_DIM)
    r = jax.nn.sigmoid(jnp.einsum('bsnd,nde->bsne', xh, ga_w).reshape(B, S, LRU_WIDTH) + ga_b)
    i = jax.nn.sigmoid(jnp.einsum('bsnd,nde->bsne', xh, gx_w).reshape(B, S, LRU_WIDTH) + gx_b)
    log_a = (-LRU_C * r * jax.nn.softplus(-lam)).astype(jnp.float32)
    a = jnp.exp(log_a)
    u = jnp.sqrt(-jnp.expm1(2.0 * log_a)) * (i * xc).astype(jnp.float32)

    def combine(lhs, rhs):
        a1, b1 = lhs
        a2, b2 = rhs
        return a1 * a2, a2 * b1 + b2

    _, h = lax.associative_scan(combine, (a, u), axis=1)
    return h.astype(xb.dtype) * jax.nn.gelu(gb)


def wkv7_scan(r, w, k, v, a, b):
    B, S, H, N = r.shape

    def step(state, inp):
        r_t, w_t, k_t, v_t, a_t, b_t = inp
        sa = jnp.einsum('bhvk,bhk->bhv', state, a_t)
        state = (state * w_t[:, :, None, :] + sa[..., None] * b_t[:, :, None, :]
                 + v_t[..., None] * k_t[:, :, None, :])
        return state, jnp.einsum('bhvk,bhk->bhv', state, r_t)

    xs = tuple(t.transpose(1, 0, 2, 3) for t in (r, w, k, v, a, b))
    _, ys = lax.scan(step, jnp.zeros((B, H, N, N), jnp.float32), xs)
    return ys.transpose(1, 0, 2, 3)


def rwkv7_mixer(p, mu, w0, w_up, a0, a_up, g_up, k_k, k_a, r_k, lnx_g, lnx_b):
    B, S, _ = p.shape
    pf = p.astype(jnp.float32)
    p_prev = jnp.pad(pf, ((0, 0), (1, 0), (0, 0)))[:, :S]
    pf = pf + (p_prev - pf) * mu
    r, k, v, xw, xa, xg = jnp.split(pf, RWKV_SPLITS, axis=-1)
    w = -jax.nn.softplus(-(w0 + jnp.tanh(xw) @ w_up)) - 0.5
    decay = jnp.exp(-jnp.exp(w))
    a = jax.nn.sigmoid(a0 + xa @ a_up)
    g = jax.nn.sigmoid(xg) @ g_up
    hs = lambda t: t.reshape(B, S, RWKV_HEADS, HEAD_DIM)
    kk = hs(k * k_k)
    kk = kk / jnp.maximum(jnp.sqrt(jnp.sum(kk * kk, axis=-1, keepdims=True)), 1e-12)
    k = k * (1.0 + (a - 1.0) * k_a)
    rh, kh, vh, ah = hs(r), hs(k), hs(v), hs(a)
    y = wkv7_scan(rh, hs(decay), kh, vh, -kk, kk * ah)
    m = y.mean(-1, keepdims=True)
    var = jnp.square(y - m).mean(-1, keepdims=True)
    y = ((y - m) * lax.rsqrt(var + RWKV_LN_EPS)).reshape(B, S, RWKV_WIDTH) * lnx_g + lnx_b
    bonus = jnp.sum(rh * kh * r_k, axis=-1, keepdims=True) * vh
    y = y + bonus.reshape(B, S, RWKV_WIDTH)
    return (y * g).astype(p.dtype)


def hybrid_mixer(x, cos, sin, w_in, conv_w, conv_b, ga_w, ga_b, gx_w, gx_b, lam,
                 mu, w0, w_up, a0, a_up, g_up, k_k, k_a, r_k, lnx_g, lnx_b, w_out):
    B, S, _ = x.shape
    proj = x @ w_in
    q, k, v, lru_x, lru_g, rwkv_p = jnp.split(proj, IN_SPLITS, axis=-1)
    heads = lambda t: t.reshape(B, S, ATT_HEADS, HEAD_DIM)
    q = partial_rotary(heads(q), cos, sin).transpose(0, 2, 1, 3)
    k = partial_rotary(heads(k), cos, sin).transpose(0, 2, 1, 3)
    v = heads(v).transpose(0, 2, 1, 3)
    att = moba_attention(q, k, v).transpose(0, 2, 1, 3).reshape(B, S, ATT_WIDTH)
    lru = rglru_mixer(lru_x, lru_g, conv_w, conv_b, ga_w, ga_b, gx_w, gx_b, lam)
    rwk = rwkv7_mixer(rwkv_p, mu, w0, w_up, a0, a_up, g_up, k_k, k_a, r_k, lnx_g, lnx_b)
    return jnp.concatenate([att, lru, rwk], axis=-1) @ w_out


def swiglu(x, w_gate, w_up, w_down):
    return (jax.nn.silu(x @ w_gate) * (x @ w_up)) @ w_down


def moe_swiglu(x, w_router, w_gate, w_up, w_down):
    T, D = x.shape
    logits = (x @ w_router).astype(jnp.float32)
    top_val, top_idx = lax.top_k(logits, TOP_K)
    gates = jax.nn.softmax(top_val, axis=-1)
    expert_of = top_idx.reshape(-1)
    token_of = jnp.repeat(jnp.arange(T, dtype=jnp.int32), TOP_K)
    gate_of = gates.reshape(-1)
    onehot = jax.nn.one_hot(expert_of, N_EXPERTS, dtype=jnp.int32)
    rank = jnp.sum(jnp.cumsum(onehot, axis=0) * onehot, axis=-1) - 1
    counts = onehot.sum(0)
    padded = (counts + MOE_BLOCK - 1) // MOE_BLOCK * MOE_BLOCK
    pad_end = jnp.cumsum(padded)
    dest = pad_end[expert_of] - padded[expert_of] + rank
    n_rows = -(-(T * TOP_K) // MOE_BLOCK) * MOE_BLOCK + N_EXPERTS * MOE_BLOCK
    n_blocks = n_rows // MOE_BLOCK
    row_tok = jnp.zeros((n_rows,), jnp.int32).at[dest].set(token_of)
    row_gate = jnp.zeros((n_rows,), jnp.float32).at[dest].set(gate_of)
    blk_start = jnp.arange(n_blocks, dtype=jnp.int32) * MOE_BLOCK
    blk_expert = jnp.minimum(jnp.searchsorted(pad_end, blk_start, side='right'), N_EXPERTS - 1)
    xs = x[row_tok].reshape(n_blocks, MOE_BLOCK, D)

    def expert_block(args):
        xb, e = args
        return (jax.nn.silu(xb @ w_gate[e]) * (xb @ w_up[e])) @ w_down[e]

    ys = lax.map(expert_block, (xs, blk_expert)).reshape(n_rows, D)
    ys = ys * row_gate[:, None].astype(ys.dtype)
    return jnp.zeros_like(x).at[row_tok].add(ys)


def setup_inputs(seed: int = 0) -> dict:
    key = jax.random.key(seed)
    ks = iter(jax.random.split(key, 48))
    f32 = jnp.float32
    L = DEPTH

    def nrm(shape, scale):
        return jax.random.normal(next(ks), shape, f32) * scale

    x = nrm((BATCH, SEQ, D_MODEL), 1.0)
    offset = jax.random.randint(next(ks), (BATCH, 1), 0, 1024, dtype=jnp.int32)
    positions = offset + jnp.arange(SEQ, dtype=jnp.int32)[None, :]
    u = jax.random.uniform(next(ks), (L, LRU_WIDTH), f32, 0.9, 0.999)
    return {
        "x": x,
        "positions": positions,
        "w_in": nrm((L, D_MODEL, IN_WIDTH), D_MODEL ** -0.5),
        "lru_conv_w": nrm((L, CONV_WIDTH, LRU_WIDTH), CONV_WIDTH ** -0.5),
        "lru_conv_b": nrm((L, LRU_WIDTH), 0.01),
        "lru_ga_w": nrm((L, LRU_BLOCKS, HEAD_DIM, HEAD_DIM), HEAD_DIM ** -0.5),
        "lru_ga_b": nrm((L, LRU_WIDTH), 0.01),
        "lru_gx_w": nrm((L, LRU_BLOCKS, HEAD_DIM, HEAD_DIM), HEAD_DIM ** -0.5),
        "lru_gx_b": nrm((L, LRU_WIDTH), 0.01),
        "lru_lambda": jnp.log(u) - jnp.log1p(-u),
        "rwkv_mu": jax.random.uniform(next(ks), (L, RWKV_PROJ), f32, 0.0, 1.0),
        "rwkv_w0": jnp.linspace(-6.0, -1.0, RWKV_WIDTH, dtype=f32)[None, :] + nrm((L, RWKV_WIDTH), 0.1),
        "rwkv_w_up": nrm((L, RWKV_DECAY_RANK, RWKV_WIDTH), 0.1),
        "rwkv_a0": nrm((L, RWKV_WIDTH), 0.1),
        "rwkv_a_up": nrm((L, RWKV_A_RANK, RWKV_WIDTH), 0.5 * RWKV_A_RANK ** -0.5),
        "rwkv_g_up": nrm((L, RWKV_GATE_RANK, RWKV_WIDTH), RWKV_GATE_RANK ** -0.5),
        "rwkv_k_k": 0.85 + nrm((L, RWKV_WIDTH), 0.02),
        "rwkv_k_a": 1.0 + nrm((L, RWKV_WIDTH), 0.02),
        "rwkv_r_k": nrm((L, RWKV_HEADS, HEAD_DIM), 0.1),
        "rwkv_lnx_g": 1.0 + nrm((L, RWKV_WIDTH), 0.02),
        "rwkv_lnx_b": nrm((L, RWKV_WIDTH), 0.01),
        "w_out": nrm((L, MIX_WIDTH, D_MODEL), DEEPNORM_BETA * MIX_WIDTH ** -0.5),
        "ln1_g": 1.0 + nrm((L, D_MODEL), 0.02),
        "ln1_b": nrm((L, D_MODEL), 0.01),
        "ffn_w_gate": nrm((N_DENSE, D_MODEL, D_FF), D_MODEL ** -0.5),
        "ffn_w_up": nrm((N_DENSE, D_MODEL, D_FF), D_MODEL ** -0.5),
        "ffn_w_down": nrm((N_DENSE, D_FF, D_MODEL), DEEPNORM_BETA * D_FF ** -0.5),
        "moe_router": nrm((N_MOE, D_MODEL, N_EXPERTS), D_MODEL ** -0.5),
        "moe_w_gate": nrm((N_MOE, N_EXPERTS, D_MODEL, D_EXPERT), D_MODEL ** -0.5),
        "moe_w_up": nrm((N_MOE, N_EXPERTS, D_MODEL, D_EXPERT), D_MODEL ** -0.5),
        "moe_w_down": nrm((N_MOE, N_EXPERTS, D_EXPERT, D_MODEL), DEEPNORM_BETA * D_EXPERT ** -0.5),
        "ln2_g": 1.0 + nrm((L, D_MODEL), 0.02),
        "ln2_b": nrm((L, D_MODEL), 0.01),
    }


def reference(x, positions, w_in, lru_conv_w, lru_conv_b, lru_ga_w, lru_ga_b, lru_gx_w, lru_gx_b, lru_lambda,
              rwkv_mu, rwkv_w0, rwkv_w_up, rwkv_a0, rwkv_a_up, rwkv_g_up, rwkv_k_k, rwkv_k_a, rwkv_r_k,
              rwkv_lnx_g, rwkv_lnx_b, w_out, ln1_g, ln1_b, ffn_w_gate, ffn_w_up, ffn_w_down,
              moe_router, moe_w_gate, moe_w_up, moe_w_down, ln2_g, ln2_b):
    B, S, D = x.shape
    inv_freq = ROPE_THETA ** (-jnp.arange(0, ROPE_DIMS, 2, dtype=jnp.float32) / ROPE_DIMS)
    ang = positions.astype(jnp.float32)[..., None] * inv_freq
    cos = jnp.cos(ang)[:, :, None, :].astype(x.dtype)
    sin = jnp.sin(ang)[:, :, None, :].astype(x.dtype)
    for l in range(DEPTH):
        h = hybrid_mixer(x, cos, sin, w_in[l], lru_conv_w[l], lru_conv_b[l], lru_ga_w[l], lru_ga_b[l],
                         lru_gx_w[l], lru_gx_b[l], lru_lambda[l], rwkv_mu[l], rwkv_w0[l], rwkv_w_up[l],
                         rwkv_a0[l], rwkv_a_up[l], rwkv_g_up[l], rwkv_k_k[l], rwkv_k_a[l], rwkv_r_k[l],
                         rwkv_lnx_g[l], rwkv_lnx_b[l], w_out[l])
        x = layer_norm(DEEPNORM_ALPHA * x + h, ln1_g[l], ln1_b[l])
        if l % 2 == 0:
            f = swiglu(x, ffn_w_gate[l // 2], ffn_w_up[l // 2], ffn_w_down[l // 2])
        else:
            f = moe_swiglu(x.reshape(B * S, D), moe_router[l // 2], moe_w_gate[l // 2],
                           moe_w_up[l // 2], moe_w_down[l // 2]).reshape(B, S, D)
        x = layer_norm(DEEPNORM_ALPHA * x + f, ln2_g[l], ln2_b[l])
    return x
```

```python
import functools

import jax
import jax.numpy as jnp
from jax import lax
from jax.experimental import pallas as pl
from jax.experimental.pallas import tpu as pltpu

F32 = jnp.float32
BF16 = jnp.bfloat16
HI = lax.Precision.HIGHEST
NT_DIMS = (((1,), (1,)), ((), ()))
TN_DIMS = (((0,), (0,)), ((), ()))

HEAD_DIM = 64
ATT_HEADS = 8
ATT_WIDTH = ATT_HEADS * HEAD_DIM
LRU_WIDTH = 256
RWKV_HEADS = 4
RWKV_WIDTH = 256
ROPE_DIMS = 16
ROPE_THETA = 500000.0
MOBA_BLOCK = 256
MOBA_TOPK = 3
CONV_WIDTH = 4
LRU_C = 8.0
RWKV_LN_EPS = 64e-5
N_EXPERTS = 8
LN_EPS = 1e-5
DEPTH = 2
DEEPNORM_ALPHA = (2 * DEPTH) ** 0.25
NEG_INF = -1e30

LANES = 128
VMEM_LIMIT = 56 * 1024 * 1024

WKV_CHUNK = 64
LRU_CHUNK = 256
RWKV_PREP_ROWS = 512


def _params(*sem):
    return pltpu.CompilerParams(dimension_semantics=sem, vmem_limit_bytes=VMEM_LIMIT)


def _sigmoid(x):
    return 1.0 / (1.0 + jnp.exp(-x))


def _softplus(x):
    return jnp.maximum(x, 0.0) + jnp.log1p(jnp.exp(-jnp.abs(x)))


def _expm1(z):
    u = jnp.exp(z)
    um1 = u - 1.0
    return jnp.where(u == 1.0, z, jnp.where(um1 == -1.0, -1.0, um1 * z / jnp.log(u)))


def _layer_norm(y, g, b):
    m = jnp.mean(y, axis=-1, keepdims=True)
    d = y - m
    var = jnp.mean(d * d, axis=-1, keepdims=True)
    return d * lax.rsqrt(var + LN_EPS) * g + b


def _matmul_kernel(x_ref, w_ref, o_ref):
    o_ref[...] = jnp.dot(x_ref[...], w_ref[...], preferred_element_type=F32)


def _in_proj(xb, w):
    T, D = xb.shape
    N = w.shape[1]
    tm, tn = 512, 1024
    return pl.pallas_call(
        _matmul_kernel,
        grid=(N // tn, T // tm),
        in_specs=[pl.BlockSpec((tm, D), lambda n, m: (m, 0)),
                  pl.BlockSpec((D, tn), lambda n, m: (0, n))],
        out_specs=pl.BlockSpec((tm, tn), lambda n, m: (m, n)),
        out_shape=jax.ShapeDtypeStruct((T, N), F32),
        compiler_params=_params("arbitrary", "arbitrary"),
        name="in_proj",
    )(xb, w)


def _attn_prep_kernel(proj_ref, c_ref, s1_ref, s2_ref, q_ref, k_ref, vt_ref, pen_ref, km_ref, *, nb):
    i = pl.program_id(1)

    @pl.when(i == 0)
    def _():
        km_ref[...] = jnp.zeros_like(km_ref)

    c = c_ref[...]
    s1 = s1_ref[...]
    s2 = s2_ref[...]

    def rope(xt):
        return xt * c + pltpu.roll(xt, LANES - ROPE_DIMS // 2, 1) * s1 + pltpu.roll(xt, ROPE_DIMS // 2, 1) * s2

    q_tiles = []
    for ct in range(ATT_WIDTH // LANES):
        lo, hi = ct * LANES, (ct + 1) * LANES
        qr = rope(proj_ref[:, lo:hi])
        q_tiles.append(qr)
        q_ref[:, lo:hi] = (qr * (HEAD_DIM ** -0.5)).astype(BF16)
        kr = rope(proj_ref[:, ATT_WIDTH + lo:ATT_WIDTH + hi])
        k_ref[0, 0, :, lo:hi] = kr.astype(BF16)
        km_row = lax.broadcasted_iota(jnp.int32, (nb, LANES), 0)
        km_ref[:, lo:hi] = jnp.where(km_row == i, jnp.mean(kr, axis=0, keepdims=True), km_ref[:, lo:hi])
    vt_ref[0, 0] = proj_ref[:, 2 * ATT_WIDTH:3 * ATT_WIDTH].T.astype(BF16)

    q_rot = jnp.concatenate(q_tiles, axis=1)
    km = km_ref[...]
    lane_head = lax.broadcasted_iota(jnp.int32, km.shape, 1) // HEAD_DIM
    n_iota = lax.broadcasted_iota(jnp.int32, (nb, MOBA_BLOCK), 0)
    past = n_iota < i
    for h in range(ATT_HEADS):
        kmh = jnp.where(lane_head == h, km, 0.0)
        g = lax.dot_general(kmh, q_rot, NT_DIMS, precision=HI, preferred_element_type=F32)
        rank = jnp.zeros((nb, MOBA_BLOCK), F32)
        for m in range(nb):
            gm = g[m:m + 1, :]
            beats = jnp.where(gm > g, 1.0, jnp.where(gm == g, jnp.where(n_iota > m, 1.0, 0.0), 0.0))
            rank = rank + jnp.where(m < i, beats, 0.0)
        pen = jnp.where(past, jnp.where(rank < float(MOBA_TOPK), 0.0, NEG_INF), NEG_INF)
        pen_ref[0, h] = pen


def _attn_prep(proj, rope_c, rope_s1, rope_s2, B, S):
    T = B * S
    nb = S // MOBA_BLOCK
    blk = MOBA_BLOCK
    return pl.pallas_call(
        functools.partial(_attn_prep_kernel, nb=nb),
        grid=(B, nb),
        in_specs=[pl.BlockSpec((blk, 3 * ATT_WIDTH), lambda b, i: (b * nb + i, 0)),
                  pl.BlockSpec((blk, LANES), lambda b, i: (b * nb + i, 0)),
                  pl.BlockSpec((blk, LANES), lambda b, i: (b * nb + i, 0)),
                  pl.BlockSpec((blk, LANES), lambda b, i: (b * nb + i, 0))],
        out_specs=[pl.BlockSpec((blk, ATT_WIDTH), lambda b, i: (b * nb + i, 0)),
                   pl.BlockSpec((1, 1, blk, ATT_WIDTH), lambda b, i: (b, i, 0, 0)),
                   pl.BlockSpec((1, 1, ATT_WIDTH, blk), lambda b, i: (b, i, 0, 0)),
                   pl.BlockSpec((1, ATT_HEADS, nb, blk), lambda b, i: (b, 0, 0, i))],
        out_shape=[jax.ShapeDtypeStruct((T, ATT_WIDTH), BF16),
                   jax.ShapeDtypeStruct((B, nb, blk, ATT_WIDTH), BF16),
                   jax.ShapeDtypeStruct((B, nb, ATT_WIDTH, blk), BF16),
                   jax.ShapeDtypeStruct((B, ATT_HEADS, nb, S), F32)],
        scratch_shapes=[pltpu.VMEM((nb, ATT_WIDTH), F32)],
        compiler_params=_params("arbitrary", "arbitrary"),
        name="attn_prep",
    )(proj, rope_c, rope_s1, rope_s2)


def _attn_kernel(q_ref, k_ref, vt_ref, pen_ref, o_ref):
    i = pl.program_id(2)
    blk = MOBA_BLOCK
    q = q_ref[...]
    lane_head = lax.broadcasted_iota(jnp.int32, q.shape, 1) // HEAD_DIM
    kidx = lax.broadcasted_iota(jnp.int32, (blk, blk), 0)
    qidx = lax.broadcasted_iota(jnp.int32, (blk, blk), 1)
    causal = kidx <= qidx
    outs = []
    for h in range(LANES // HEAD_DIM):
        qm = jnp.where(lane_head == h, q, jnp.zeros_like(q))
        s = lax.dot_general(k_ref[0, i], qm, NT_DIMS, preferred_element_type=F32)
        s = jnp.where(causal, s, NEG_INF)
        m0 = jnp.max(s, axis=0, keepdims=True)
        p = jnp.exp(s - m0)
        l0 = jnp.sum(p, axis=0, keepdims=True)
        acc0 = jnp.dot(vt_ref[0, i], p.astype(BF16), preferred_element_type=F32)

        def body(j, carry, qm=qm, h=h):
            m, l, acc = carry
            s = lax.dot_general(k_ref[0, j], qm, NT_DIMS, preferred_element_type=F32) + pen_ref[0, h, pl.ds(j, 1), :]
            m_new = jnp.maximum(m, jnp.max(s, axis=0, keepdims=True))
            alpha = jnp.exp(m - m_new)
            p = jnp.exp(s - m_new)
            l = alpha * l + jnp.sum(p, axis=0, keepdims=True)
            acc = acc * alpha + jnp.dot(vt_ref[0, j], p.astype(BF16), preferred_element_type=F32)
            return m_new, l, acc

        _, l, acc = lax.fori_loop(0, i, body, (m0, l0, acc0))
        outs.append(acc / l)
    row = lax.broadcasted_iota(jnp.int32, (LANES, blk), 0)
    o_ref[...] = jnp.where(row < HEAD_DIM, outs[0], outs[1]).T


def _attention(q, k, vt, pen, B, S):
    T = B * S
    nb = S // MOBA_BLOCK
    blk = MOBA_BLOCK
    n_pairs = ATT_WIDTH // LANES
    return pl.pallas_call(
        _attn_kernel,
        grid=(B, n_pairs, nb),
        in_specs=[pl.BlockSpec((blk, LANES), lambda b, hp, i: (b * nb + i, hp)),
                  pl.BlockSpec((1, nb, blk, LANES), lambda b, hp, i: (b, 0, 0, hp)),
                  pl.BlockSpec((1, nb, LANES, blk), lambda b, hp, i: (b, 0, hp, 0)),
                  pl.BlockSpec((1, LANES // HEAD_DIM, nb, blk), lambda b, hp, i: (b, hp, 0, i))],
        out_specs=pl.BlockSpec((blk, LANES), lambda b, hp, i: (b * nb + i, hp)),
        out_shape=jax.ShapeDtypeStruct((T, ATT_WIDTH), F32),
        compiler_params=_params("arbitrary", "arbitrary", "arbitrary"),
        name="moba_attn",
    )(q, k, vt, pen)


def _lru_kernel(p_ref, cw_ref, cb_ref, ga_ref, gab_ref, gx_ref, gxb_ref, lam_ref, o_ref, xbuf, hc):
    t = pl.program_id(1)
    tc = LRU_CHUNK
    pad = 8

    @pl.when(t == 0)
    def _():
        xbuf[0:pad, :] = jnp.zeros((pad, LRU_WIDTH), F32)
        hc[...] = jnp.zeros_like(hc)

    @pl.when(t > 0)
    def _():
        xbuf[0:pad, :] = xbuf[tc:tc + pad, :]

    x = p_ref[:, 0:LRU_WIDTH]
    gate = p_ref[:, LRU_WIDTH:2 * LRU_WIDTH]
    xbuf[pad:pad + tc, :] = x
    xc = cb_ref[...] + x * cw_ref[CONV_WIDTH - 1:CONV_WIDTH, :]
    for j in range(CONV_WIDTH - 1):
        back = CONV_WIDTH - 1 - j
        xc = xc + xbuf[pad - back:pad - back + tc, :] * cw_ref[j:j + 1, :]

    r = _sigmoid(jnp.dot(xc, ga_ref[...], precision=HI, preferred_element_type=F32) + gab_ref[...])
    ig = _sigmoid(jnp.dot(xc, gx_ref[...], precision=HI, preferred_element_type=F32) + gxb_ref[...])
    log_a = -LRU_C * r * _softplus(-lam_ref[...])
    a = jnp.exp(log_a)
    u = jnp.sqrt(-_expm1(2.0 * log_a)) * (ig * xc)

    rows = lax.broadcasted_iota(jnp.int32, (tc, LRU_WIDTH), 0)
    d = 1
    while d < tc:
        keep = rows >= d
        a_prev = jnp.where(keep, pltpu.roll(a, d, 0), 1.0)
        u_prev = jnp.where(keep, pltpu.roll(u, d, 0), 0.0)
        u = a * u_prev + u
        a = a * a_prev
        d *= 2
    h = u + a * hc[...]
    hc[...] = h[tc - 1:tc, :]
    gl = 0.5 * gate * (1.0 + jnp.tanh(0.7978845608028654 * (gate + 0.044715 * gate * gate * gate)))
    o_ref[...] = h * gl


def _rglru(proj, cw, cb, ga, gab, gx, gxb, lam, B, S):
    T = B * S
    tc = LRU_CHUNK
    nt = S // tc
    col = (3 * ATT_WIDTH) // (2 * LRU_WIDTH)
    vec = lambda: pl.BlockSpec((1, LRU_WIDTH), lambda b, t: (0, 0))
    mat = lambda: pl.BlockSpec((LRU_WIDTH, LRU_WIDTH), lambda b, t: (0, 0))
    return pl.pallas_call(
        _lru_kernel,
        grid=(B, nt),
        in_specs=[pl.BlockSpec((tc, 2 * LRU_WIDTH), lambda b, t: (b * nt + t, col)),
                  pl.BlockSpec((CONV_WIDTH, LRU_WIDTH), lambda b, t: (0, 0)),
                  vec(), mat(), vec(), mat(), vec(), vec()],
        out_specs=pl.BlockSpec((tc, LRU_WIDTH), lambda b, t: (b * nt + t, 0)),
        out_shape=jax.ShapeDtypeStruct((T, LRU_WIDTH), F32),
        scratch_shapes=[pltpu.VMEM((tc + 8, LRU_WIDTH), F32), pltpu.VMEM((1, LRU_WIDTH), F32)],
        compiler_params=_params("arbitrary", "arbitrary"),
        name="rglru",
    )(proj, cw, cb, ga, gab, gx, gxb, lam)


def _rwkv_prep_kernel(p_ref, pp_ref, mu_ref, wlr_ref, w0_ref, a0_ref, kk_ref, ka_ref, ones_ref,
                      r_ref, ld_ref, k2_ref, v_ref, kn_ref, a_ref, g_ref):
    i = pl.program_id(1)
    W = RWKV_WIDTH
    p = p_ref[...]
    rows = lax.broadcasted_iota(jnp.int32, p.shape, 0)
    prev_last = jnp.where(i == 0, 0.0, pp_ref[7:8, :])
    p_prev = jnp.where(rows == 0, prev_last, pltpu.roll(p, 1, 0))
    pf = p + (p_prev - p) * mu_ref[...]
    r = pf[:, 0:W]
    k = pf[:, W:2 * W]
    v = pf[:, 2 * W:3 * W]
    z = pf[:, 3 * W:4 * W]
    lane = lax.broadcasted_iota(jnp.int32, z.shape, 1)
    zz = jnp.where(lane < 64, jnp.tanh(z), jnp.where(lane < 128, z, _sigmoid(z)))
    lr = jnp.dot(zz, wlr_ref[...], precision=HI, preferred_element_type=F32)
    w = -_softplus(-(w0_ref[...] + lr[:, 0:W])) - 0.5
    a = _sigmoid(a0_ref[...] + lr[:, W:2 * W])
    kk = k * kk_ref[...]
    ssq = jnp.dot(kk * kk, ones_ref[...], precision=HI, preferred_element_type=F32)
    kk = kk / jnp.maximum(jnp.sqrt(ssq), 1e-12)
    r_ref[...] = r
    ld_ref[...] = -jnp.exp(w)
    k2_ref[...] = k * (1.0 + (a - 1.0) * ka_ref[...])
    v_ref[...] = v
    kn_ref[...] = kk
    a_ref[...] = a
    g_ref[...] = lr[:, 2 * W:3 * W]


def _rwkv_prep(proj, mu, wlr, w0, a0, k_k, k_a, head_ones, B, S):
    T = B * S
    tr = RWKV_PREP_ROWS
    nt = S // tr
    W = RWKV_WIDTH
    col = 2
    vec = lambda: pl.BlockSpec((1, W), lambda b, t: (0, 0))
    out = lambda: pl.BlockSpec((tr, W), lambda b, t: (b * nt + t, 0))
    return pl.pallas_call(
        _rwkv_prep_kernel,
        grid=(B, nt),
        in_specs=[pl.BlockSpec((tr, 4 * W), lambda b, t: (b * nt + t, col)),
                  pl.BlockSpec((8, 4 * W), lambda b, t: (jnp.maximum((b * nt + t) * (tr // 8) - 1, 0), col)),
                  pl.BlockSpec((1, 4 * W), lambda b, t: (0, 0)),
                  pl.BlockSpec((W, 3 * W), lambda b, t: (0, 0)),
                  vec(), vec(), vec(), vec(),
                  pl.BlockSpec((W, W), lambda b, t: (0, 0))],
        out_specs=[out() for _ in range(7)],
        out_shape=[jax.ShapeDtypeStruct((T, W), F32) for _ in range(7)],
        compiler_params=_params("arbitrary", "arbitrary"),
        name="rwkv_prep",
    )(proj, proj, mu, wlr, w0, a0, k_k, k_a, head_ones)


def _wkv_kernel(r_ref, ld_ref, k2_ref, v_ref, kn_ref, a_ref, g_ref, rk_ref, lg_ref, lb_ref, ones_ref,
                o_ref, st_ref):
    c = pl.program_id(1)
    C = WKV_CHUNK
    W = RWKV_WIDTH
    H = RWKV_HEADS

    @pl.when(c == 0)
    def _():
        st_ref[...] = jnp.zeros_like(st_ref)

    r = r_ref[...]
    ld = ld_ref[...]
    k2 = k2_ref[...]
    v = v_ref[...]
    kn = kn_ref[...]
    b_s = kn * a_ref[...]
    ones = ones_ref[...]

    def mm(x, y):
        return jnp.dot(x, y, precision=HI, preferred_element_type=F32)

    def mm_nt(x, y):
        return lax.dot_general(x, y, NT_DIMS, precision=HI, preferred_element_type=F32)

    def mm_tn(x, y):
        return lax.dot_general(x, y, TN_DIMS, precision=HI, preferred_element_type=F32)

    tr = lax.broadcasted_iota(jnp.int32, (C, C), 0)
    tc = lax.broadcasted_iota(jnp.int32, (C, C), 1)
    cl = mm(jnp.where(tr >= tc, 1.0, 0.0), ld)
    cl_end = cl[C - 1:C, :]
    e_neg = jnp.exp(-cl)
    e_end = jnp.exp(cl_end - cl)

    row = lax.broadcasted_iota(jnp.int32, (W, W), 0)
    colm = lax.broadcasted_iota(jnp.int32, (W, W), 1)
    same_head = (row // C) == (colm // HEAD_DIM)
    strict = lambda x: jnp.where(same_head, jnp.where(row > colm, x, 0.0), 0.0)
    incl = lambda x: jnp.where(same_head, jnp.where(row >= colm, x, 0.0), 0.0)
    same16 = (row // 16) == (colm // 16)
    eye = jnp.where(row == colm, 1.0, 0.0)

    def stack(x):
        return jnp.where(same_head, jnp.concatenate([x] * H, axis=0), 0.0)

    at_s = stack(-kn * jnp.exp(cl - ld))
    rt_s = stack(r * jnp.exp(cl))
    bt_s = stack(b_s * e_neg)
    kt_s = stack(k2 * e_neg)
    v_s = stack(v)
    bend_s = stack(b_s * e_end)
    kend_s = stack(k2 * e_end)

    n_ab = strict(mm_nt(at_s, bt_s))
    a_ak = strict(mm_nt(at_s, kt_s))
    a_rb = incl(mm_nt(rt_s, bt_s))
    a_rk = incl(mm_nt(rt_s, kt_s))

    nd = jnp.where(same16, n_ab, 0.0)
    lo = n_ab - nd
    n2 = mm(nd, nd)
    n4 = mm(n2, n2)
    n8 = mm(n4, n4)
    p1 = eye + nd
    p2 = p1 + mm(p1, n2)
    p3 = p2 + mm(p2, n4)
    dinv = p3 + mm(p3, n8)
    x1 = mm(dinv, lo)
    x2 = mm(x1, x1)
    y1 = eye + x1
    tinv = mm(y1 + mm(y1, x2), dinv)

    st = st_ref[...]
    u_s = mm(tinv, mm(at_s, st) + mm(a_ak, v_s))
    y_s = mm(rt_s, st) + mm(a_rb, u_s) + mm(a_rk, v_s)
    y = y_s[0:C, :]
    for h in range(1, H):
        y = y + y_s[h * C:(h + 1) * C, :]

    pc = jnp.exp(mm_tn(stack(ld), jnp.ones((W, W), F32)))
    st_ref[...] = pc * st + mm_tn(bend_s, u_s) + mm_tn(kend_s, v_s)

    inv_n = 1.0 / HEAD_DIM
    mean = mm(y, ones) * inv_n
    d = y - mean
    var = mm(d * d, ones) * inv_n
    yn = d * lax.rsqrt(var + RWKV_LN_EPS) * lg_ref[...] + lb_ref[...]
    bonus = mm(r * k2 * rk_ref[...], ones) * v
    o_ref[...] = (yn + bonus) * g_ref[...]


def _wkv(r, ld, k2, v, kn, a, g, r_k, lnx_g, lnx_b, head_ones, B, S):
    T = B * S
    C = WKV_CHUNK
    nc = S // C
    W = RWKV_WIDTH
    blk = lambda: pl.BlockSpec((C, W), lambda b, c: (b * nc + c, 0))
    vec = lambda: pl.BlockSpec((1, W), lambda b, c: (0, 0))
    return pl.pallas_call(
        _wkv_kernel,
        grid=(B, nc),
        in_specs=[blk() for _ in range(7)] + [vec(), vec(), vec(), pl.BlockSpec((W, W), lambda b, c: (0, 0))],
        out_specs=blk(),
        out_shape=jax.ShapeDtypeStruct((T, W), F32),
        scratch_shapes=[pltpu.VMEM((W, W), F32)],
        compiler_params=_params("arbitrary", "arbitrary"),
        name="wkv7",
    )(r, ld, k2, v, kn, a, g, r_k, lnx_g, lnx_b, head_ones)


def _outproj_ln_kernel(att_ref, lru_ref, rwk_ref, x_ref, wa_ref, wl_ref, wr_ref, g_ref, b_ref, o_ref, ob_ref):
    h = jnp.dot(att_ref[...].astype(BF16), wa_ref[...], preferred_element_type=F32)
    h = h + jnp.dot(lru_ref[...].astype(BF16), wl_ref[...], preferred_element_type=F32)
    h = h + jnp.dot(rwk_ref[...].astype(BF16), wr_ref[...], preferred_element_type=F32)
    y = _layer_norm(DEEPNORM_ALPHA * x_ref[...] + h, g_ref[...], b_ref[...])
    o_ref[...] = y
    ob_ref[...] = y.astype(BF16)


def _outproj_ln(att, lru, rwk, x, wa, wl, wr, g, b):
    T, D = x.shape
    tm = 512
    row = lambda w: pl.BlockSpec((tm, w), lambda m: (m, 0))
    full = lambda a: pl.BlockSpec(a.shape, lambda m: (0, 0))
    return pl.pallas_call(
        _outproj_ln_kernel,
        grid=(T // tm,),
        in_specs=[row(ATT_WIDTH), row(LRU_WIDTH), row(RWKV_WIDTH), row(D), full(wa), full(wl), full(wr), full(g), full(b)],
        out_specs=[row(D), row(D)],
        out_shape=[jax.ShapeDtypeStruct((T, D), F32), jax.ShapeDtypeStruct((T, D), BF16)],
        compiler_params=_params("arbitrary"),
        name="outproj_ln",
    )(att, lru, rwk, x, wa, wl, wr, g, b)


def _ffn_kernel(xb_ref, x_ref, wg_ref, wu_ref, wd_ref, g_ref, b_ref, o_ref, ob_ref, acc_ref, *, nf):
    f = pl.program_id(1)
    xb = xb_ref[...]
    hg = jnp.dot(xb, wg_ref[...], preferred_element_type=F32)
    hu = jnp.dot(xb, wu_ref[...], preferred_element_type=F32)
    hh = (hg * _sigmoid(hg) * hu).astype(BF16)
    contrib = jnp.dot(hh, wd_ref[...], preferred_element_type=F32)

    @pl.when(f == 0)
    def _():
        acc_ref[...] = contrib

    @pl.when(f > 0)
    def _():
        acc_ref[...] = acc_ref[...] + contrib

    @pl.when(f == nf - 1)
    def _():
        y = _layer_norm(DEEPNORM_ALPHA * x_ref[...] + acc_ref[...], g_ref[...], b_ref[...])
        o_ref[...] = y
        ob_ref[...] = y.astype(BF16)


def _ffn_ln(xb, x, wg, wu, wd, g, b):
    T, D = x.shape
    F = wg.shape[1]
    tm = 512
    nf = 2
    tf = F // nf
    return pl.pallas_call(
        functools.partial(_ffn_kernel, nf=nf),
        grid=(T // tm, nf),
        in_specs=[pl.BlockSpec((tm, D), lambda m, f: (m, 0)),
                  pl.BlockSpec((tm, D), lambda m, f: (m, 0)),
                  pl.BlockSpec((D, tf), lambda m, f: (0, f)),
                  pl.BlockSpec((D, tf), lambda m, f: (0, f)),
                  pl.BlockSpec((tf, D), lambda m, f: (f, 0)),
                  pl.BlockSpec((1, D), lambda m, f: (0, 0)),
                  pl.BlockSpec((1, D), lambda m, f: (0, 0))],
        out_specs=[pl.BlockSpec((tm, D), lambda m, f: (m, 0)), pl.BlockSpec((tm, D), lambda m, f: (m, 0))],
        out_shape=[jax.ShapeDtypeStruct((T, D), F32), jax.ShapeDtypeStruct((T, D), BF16)],
        scratch_shapes=[pltpu.VMEM((tm, D), F32)],
        compiler_params=_params("arbitrary", "arbitrary"),
        name="ffn_ln",
    )(xb, x, wg, wu, wd, g, b)


def _router_kernel(x_ref, wr_ref, g_ref):
    logits = jnp.dot(x_ref[...], wr_ref[...], precision=HI, preferred_element_type=F32)
    lane = lax.broadcasted_iota(jnp.int32, logits.shape, 1).astype(F32)
    logits = jnp.where(lane < N_EXPERTS, logits, -jnp.inf)
    m1 = jnp.max(logits, axis=1, keepdims=True)
    i1 = jnp.min(jnp.where(logits == m1, lane, float(LANES)), axis=1, keepdims=True)
    rest = jnp.where(lane == i1, -jnp.inf, logits)
    m2 = jnp.max(rest, axis=1, keepdims=True)
    i2 = jnp.min(jnp.where(rest == m2, lane, float(LANES)), axis=1, keepdims=True)
    e = jnp.exp(m2 - m1)
    g1 = 1.0 / (1.0 + e)
    g2 = e / (1.0 + e)
    g_ref[...] = jnp.where(lane == i1, g1, jnp.where(lane == i2, g2, 0.0))


def _router(x, wr_pad):
    T, D = x.shape
    tm = 1024
    return pl.pallas_call(
        _router_kernel,
        grid=(T // tm,),
        in_specs=[pl.BlockSpec((tm, D), lambda m: (m, 0)), pl.BlockSpec((D, LANES), lambda m: (0, 0))],
        out_specs=pl.BlockSpec((tm, LANES), lambda m: (m, 0)),
        out_shape=jax.ShapeDtypeStruct((T, LANES), F32),
        compiler_params=_params("arbitrary"),
        name="moe_router",
    )(x, wr_pad)


def _moe_kernel(xb_ref, x_ref, gate_ref, wg_ref, wu_ref, wd_ref, g_ref, b_ref, o_ref, acc_ref, *, nf):
    e = pl.program_id(1)
    f = pl.program_id(2)
    xb = xb_ref[...]
    gates = gate_ref[...]
    lane = lax.broadcasted_iota(jnp.int32, gates.shape, 1)
    ge = jnp.sum(jnp.where(lane == e, gates, 0.0), axis=1, keepdims=True)
    hg = jnp.dot(xb, wg_ref[0], preferred_element_type=F32)
    hu = jnp.dot(xb, wu_ref[0], preferred_element_type=F32)
    hh = (hg * _sigmoid(hg) * hu * ge).astype(BF16)
    contrib = jnp.dot(hh, wd_ref[0], preferred_element_type=F32)
    first = jnp.logical_and(e == 0, f == 0)

    @pl.when(first)
    def _():
        acc_ref[...] = contrib

    @pl.when(jnp.logical_not(first))
    def _():
        acc_ref[...] = acc_ref[...] + contrib

    @pl.when(jnp.logical_and(e == N_EXPERTS - 1, f == nf - 1))
    def _():
        o_ref[...] = _layer_norm(DEEPNORM_ALPHA * x_ref[...] + acc_ref[...], g_ref[...], b_ref[...])


def _moe_ln(xb, x, gates, wg, wu, wd, g, b):
    T, D = x.shape
    F = wg.shape[2]
    tm = 1024
    nf = 4
    tf = F // nf
    return pl.pallas_call(
        functools.partial(_moe_kernel, nf=nf),
        grid=(T // tm, N_EXPERTS, nf),
        in_specs=[pl.BlockSpec((tm, D), lambda m, e, f: (m, 0)),
                  pl.BlockSpec((tm, D), lambda m, e, f: (m, 0)),
                  pl.BlockSpec((tm, LANES), lambda m, e, f: (m, 0)),
                  pl.BlockSpec((1, D, tf), lambda m, e, f: (e, 0, f)),
                  pl.BlockSpec((1, D, tf), lambda m, e, f: (e, 0, f)),
                  pl.BlockSpec((1, tf, D), lambda m, e, f: (e, f, 0)),
                  pl.BlockSpec((1, D), lambda m, e, f: (0, 0)),
                  pl.BlockSpec((1, D), lambda m, e, f: (0, 0))],
        out_specs=pl.BlockSpec((tm, D), lambda m, e, f: (m, 0)),
        out_shape=jax.ShapeDtypeStruct((T, D), F32),
        scratch_shapes=[pltpu.VMEM((tm, D), F32)],
        compiler_params=_params("arbitrary", "arbitrary", "arbitrary"),
        name="moe_ln",
    )(xb, x, gates, wg, wu, wd, g, b)


def _block_diag(w):
    n, d, _ = w.shape
    out = jnp.zeros((n * d, n * d), w.dtype)
    for i in range(n):
        out = out.at[i * d:(i + 1) * d, i * d:(i + 1) * d].set(w[i])
    return out


def _rope_tables(positions):
    B, S = positions.shape
    half = ROPE_DIMS // 2
    inv_freq = ROPE_THETA ** (-jnp.arange(0, ROPE_DIMS, 2, dtype=F32) / ROPE_DIMS)
    ang = positions.astype(F32)[..., None] * inv_freq
    cos = jnp.cos(ang).reshape(B * S, half)
    sin = jnp.sin(ang).reshape(B * S, half)
    zeros = jnp.zeros((B * S, HEAD_DIM - ROPE_DIMS), F32)
    zh = jnp.zeros((B * S, half), F32)
    c = jnp.concatenate([cos, cos, zeros + 1.0], axis=1)
    s1 = jnp.concatenate([-sin, zh, zeros], axis=1)
    s2 = jnp.concatenate([zh, sin, zeros], axis=1)
    tile = lambda t: jnp.concatenate([t] * (LANES // HEAD_DIM), axis=1)
    return tile(c), tile(s1), tile(s2)


def _mixer_layer(x, xb, rope, B, S, w_in, conv_w, conv_b, ga_w, ga_b, gx_w, gx_b, lam,
                 mu, w0, w_up, a0, a_up, g_up, k_k, k_a, r_k, lnx_g, lnx_b, w_out, ln_g, ln_b):
    W = RWKV_WIDTH
    row = lambda t: t.reshape(1, -1)
    proj = _in_proj(xb, w_in.astype(BF16))
    q, k, vt, pen = _attn_prep(proj, *rope, B, S)
    att = _attention(q, k, vt, pen, B, S)
    lru = _rglru(proj, conv_w, row(conv_b), _block_diag(ga_w), row(ga_b), _block_diag(gx_w), row(gx_b), row(lam), B, S)
    head_ones = _block_diag(jnp.ones((RWKV_HEADS, HEAD_DIM, HEAD_DIM), F32))
    wlr = jnp.zeros((W, 3 * W), F32)
    wlr = wlr.at[0:64, 0:W].set(w_up).at[64:128, W:2 * W].set(a_up).at[128:256, 2 * W:3 * W].set(g_up)
    r, ld, k2, v, kn, a, g = _rwkv_prep(proj, row(mu), wlr, row(w0), row(a0), row(k_k), row(k_a), head_ones, B, S)
    rwk = _wkv(r, ld, k2, v, kn, a, g, row(r_k), row(lnx_g), row(lnx_b), head_ones, B, S)
    wo = w_out.astype(BF16)
    return _outproj_ln(att, lru, rwk, x, wo[0:ATT_WIDTH], wo[ATT_WIDTH:ATT_WIDTH + LRU_WIDTH],
                       wo[ATT_WIDTH + LRU_WIDTH:], row(ln_g), row(ln_b))


def kernel(x, positions, w_in, lru_conv_w, lru_conv_b, lru_ga_w, lru_ga_b, lru_gx_w, lru_gx_b, lru_lambda, rwkv_mu, rwkv_w0, rwkv_w_up, rwkv_a0, rwkv_a_up, rwkv_g_up, rwkv_k_k, rwkv_k_a, rwkv_r_k, rwkv_lnx_g, rwkv_lnx_b, w_out, ln1_g, ln1_b, ffn_w_gate, ffn_w_up, ffn_w_down, moe_router, moe_w_gate, moe_w_up, moe_w_down, ln2_g, ln2_b):
    B, S, D = x.shape
    T = B * S
    rope = _rope_tables(positions)
    xf = x.reshape(T, D)
    xb = xf.astype(BF16)
    for l in range(DEPTH):
        xf, xb = _mixer_layer(xf, xb, rope, B, S, w_in[l], lru_conv_w[l], lru_conv_b[l], lru_ga_w[l], lru_ga_b[l],
                              lru_gx_w[l], lru_gx_b[l], lru_lambda[l], rwkv_mu[l], rwkv_w0[l], rwkv_w_up[l],
                              rwkv_a0[l], rwkv_a_up[l], rwkv_g_up[l], rwkv_k_k[l], rwkv_k_a[l], rwkv_r_k[l],
                              rwkv_lnx_g[l], rwkv_lnx_b[l], w_out[l], ln1_g[l], ln1_b[l])
        g2 = ln2_g[l].reshape(1, D)
        b2 = ln2_b[l].reshape(1, D)
        if l % 2 == 0:
            i = l // 2
            xf, xb = _ffn_ln(xb, xf, ffn_w_gate[i].astype(BF16), ffn_w_up[i].astype(BF16),
                             ffn_w_down[i].astype(BF16), g2, b2)
        else:
            i = l // 2
            wr_pad = jnp.zeros((D, LANES), F32).at[:, 0:N_EXPERTS].set(moe_router[i])
            gates = _router(xf, wr_pad)
            xf = _moe_ln(xb, xf, gates, moe_w_gate[i].astype(BF16), moe_w_up[i].astype(BF16),
                         moe_w_down[i].astype(BF16), g2, b2)
            xb = xf.astype(BF16)
    return xf.reshape(B, S, D)
```

```python
import functools

import jax
import jax.numpy as jnp
from jax import lax
from jax.experimental import pallas as pl
from jax.experimental.pallas import tpu as pltpu

F32 = jnp.float32
BF16 = jnp.bfloat16
HI = lax.Precision.HIGHEST
NT_DIMS = (((1,), (1,)), ((), ()))
TN_DIMS = (((0,), (0,)), ((), ()))

HEAD_DIM = 64
ATT_HEADS = 8
ATT_WIDTH = ATT_HEADS * HEAD_DIM
LRU_WIDTH = 256
RWKV_HEADS = 4
RWKV_WIDTH = 256
ROPE_DIMS = 16
ROPE_THETA = 500000.0
MOBA_BLOCK = 256
MOBA_TOPK = 3
CONV_WIDTH = 4
LRU_C = 8.0
RWKV_LN_EPS = 64e-5
N_EXPERTS = 8
LN_EPS = 1e-5
DEPTH = 2
DEEPNORM_ALPHA = (2 * DEPTH) ** 0.25
NEG_INF = -1e30
LOG2_E = 1.4426950408889634

LANES = 128
VMEM_LIMIT = 56 * 1024 * 1024

WKV_CHUNK = 64
WKV_GROUP = 2
LRU_CHUNK = 256
RWKV_PREP_ROWS = 512


def _params(*sem):
    return pltpu.CompilerParams(dimension_semantics=sem, vmem_limit_bytes=VMEM_LIMIT)


def _sigmoid(x):
    return 1.0 / (1.0 + jnp.exp(-x))


def _softplus(x):
    return jnp.maximum(x, 0.0) + jnp.log1p(jnp.exp(-jnp.abs(x)))


def _expm1(z):
    u = jnp.exp(z)
    um1 = u - 1.0
    return jnp.where(u == 1.0, z, jnp.where(um1 == -1.0, -1.0, um1 * z / jnp.log(u)))


def _layer_norm(y, g, b):
    m = jnp.mean(y, axis=-1, keepdims=True)
    d = y - m
    var = jnp.mean(d * d, axis=-1, keepdims=True)
    return d * lax.rsqrt(var + LN_EPS) * g + b


def _matmul_kernel(x_ref, w_ref, o_ref):
    o_ref[...] = jnp.dot(x_ref[...], w_ref[...], preferred_element_type=F32)


def _in_proj(xb, w):
    T, D = xb.shape
    N = w.shape[1]
    tm, tn = 512, 1024
    return pl.pallas_call(
        _matmul_kernel,
        grid=(N // tn, T // tm),
        in_specs=[pl.BlockSpec((tm, D), lambda n, m: (m, 0)),
                  pl.BlockSpec((D, tn), lambda n, m: (0, n))],
        out_specs=pl.BlockSpec((tm, tn), lambda n, m: (m, n)),
        out_shape=jax.ShapeDtypeStruct((T, N), F32),
        compiler_params=_params("arbitrary", "arbitrary"),
        name="in_proj",
    )(xb, w)


def _attn_prep_kernel(proj_ref, c_ref, s1_ref, s2_ref, q_ref, k_ref, vt_ref, pen_ref, km_ref, *, nb):
    i = pl.program_id(1)

    @pl.when(i == 0)
    def _():
        km_ref[...] = jnp.zeros_like(km_ref)

    c = c_ref[...]
    s1 = s1_ref[...]
    s2 = s2_ref[...]

    def rope(xt):
        return xt * c + pltpu.roll(xt, LANES - ROPE_DIMS // 2, 1) * s1 + pltpu.roll(xt, ROPE_DIMS // 2, 1) * s2

    q_tiles = []
    for ct in range(ATT_WIDTH // LANES):
        lo, hi = ct * LANES, (ct + 1) * LANES
        qr = rope(proj_ref[:, lo:hi])
        q_tiles.append(qr)
        q_ref[:, lo:hi] = (qr * (HEAD_DIM ** -0.5 * LOG2_E)).astype(BF16)
        kr = rope(proj_ref[:, ATT_WIDTH + lo:ATT_WIDTH + hi])
        k_ref[0, 0, :, lo:hi] = kr.astype(BF16)
        km_row = lax.broadcasted_iota(jnp.int32, (nb, LANES), 0)
        km_ref[:, lo:hi] = jnp.where(km_row == i, jnp.mean(kr, axis=0, keepdims=True), km_ref[:, lo:hi])
    vt_ref[0, 0] = proj_ref[:, 2 * ATT_WIDTH:3 * ATT_WIDTH].T.astype(BF16)

    q_rot = jnp.concatenate(q_tiles, axis=1)
    km = km_ref[...]
    lane_head = lax.broadcasted_iota(jnp.int32, km.shape, 1) // HEAD_DIM
    n_iota = lax.broadcasted_iota(jnp.int32, (nb, MOBA_BLOCK), 0)
    past = n_iota < i
    for h in range(ATT_HEADS):
        kmh = jnp.where(lane_head == h, km, 0.0)
        g = lax.dot_general(kmh, q_rot, NT_DIMS, precision=HI, preferred_element_type=F32)
        rank = jnp.zeros((nb, MOBA_BLOCK), F32)
        for m in range(nb):
            gm = g[m:m + 1, :]
            beats = jnp.where(gm > g, 1.0, jnp.where(gm == g, jnp.where(n_iota > m, 1.0, 0.0), 0.0))
            rank = rank + jnp.where(m < i, beats, 0.0)
        pen = jnp.where(past, jnp.where(rank < float(MOBA_TOPK), 0.0, NEG_INF), NEG_INF)
        half = (h % 2) * MOBA_BLOCK
        pen_ref[0, h // 2, :, half:half + MOBA_BLOCK] = pen


def _attn_prep(proj, rope_c, rope_s1, rope_s2, B, S):
    T = B * S
    nb = S // MOBA_BLOCK
    blk = MOBA_BLOCK
    return pl.pallas_call(
        functools.partial(_attn_prep_kernel, nb=nb),
        grid=(B, nb),
        in_specs=[pl.BlockSpec((blk, 3 * ATT_WIDTH), lambda b, i: (b * nb + i, 0)),
                  pl.BlockSpec((blk, LANES), lambda b, i: (b * nb + i, 0)),
                  pl.BlockSpec((blk, LANES), lambda b, i: (b * nb + i, 0)),
                  pl.BlockSpec((blk, LANES), lambda b, i: (b * nb + i, 0))],
        out_specs=[pl.BlockSpec((blk, ATT_WIDTH), lambda b, i: (b * nb + i, 0)),
                   pl.BlockSpec((1, 1, blk, ATT_WIDTH), lambda b, i: (b, i, 0, 0)),
                   pl.BlockSpec((1, 1, ATT_WIDTH, blk), lambda b, i: (b, i, 0, 0)),
                   pl.BlockSpec((1, ATT_HEADS // 2, nb, 2 * blk), lambda b, i: (b, 0, 0, i))],
        out_shape=[jax.ShapeDtypeStruct((T, ATT_WIDTH), BF16),
                   jax.ShapeDtypeStruct((B, nb, blk, ATT_WIDTH), BF16),
                   jax.ShapeDtypeStruct((B, nb, ATT_WIDTH, blk), BF16),
                   jax.ShapeDtypeStruct((B, ATT_HEADS // 2, nb, 2 * S), F32)],
        scratch_shapes=[pltpu.VMEM((nb, ATT_WIDTH), F32)],
        compiler_params=_params("arbitrary", "arbitrary"),
        name="attn_prep",
    )(proj, rope_c, rope_s1, rope_s2)


def _attn_kernel(q_ref, k_ref, vt_ref, pen_ref, o_ref):
    i = pl.program_id(2)
    blk = MOBA_BLOCK
    q = q_ref[...]
    lane_head = lax.broadcasted_iota(jnp.int32, q.shape, 1) // HEAD_DIM
    zero = jnp.zeros_like(q)
    qs = jnp.concatenate([jnp.where(lane_head == 0, q, zero), jnp.where(lane_head == 1, q, zero)], axis=0)
    kidx = lax.broadcasted_iota(jnp.int32, (blk, 2 * blk), 0)
    qidx = lax.broadcasted_iota(jnp.int32, (blk, 2 * blk), 1) % blk
    s = lax.dot_general(k_ref[0, i], qs, NT_DIMS, preferred_element_type=F32)
    s = jnp.where(kidx <= qidx, s, NEG_INF)
    m0 = jnp.max(s, axis=0, keepdims=True)
    p = jnp.exp2(s - m0)
    l0 = jnp.sum(p, axis=0, keepdims=True)
    acc0 = jnp.dot(vt_ref[0, i], p.astype(BF16), preferred_element_type=F32)

    def body(jj, carry):
        m, l, acc = carry
        j0 = 2 * jj
        j1 = j0 + 1
        s0 = lax.dot_general(k_ref[0, j0], qs, NT_DIMS, preferred_element_type=F32) + pen_ref[0, 0, pl.ds(j0, 1), :]
        s1 = lax.dot_general(k_ref[0, j1], qs, NT_DIMS, preferred_element_type=F32) + pen_ref[0, 0, pl.ds(j1, 1), :]
        m_new = jnp.maximum(m, jnp.maximum(jnp.max(s0, axis=0, keepdims=True), jnp.max(s1, axis=0, keepdims=True)))
        alpha = jnp.exp2(m - m_new)
        p0 = jnp.exp2(s0 - m_new)
        p1 = jnp.exp2(s1 - m_new)
        l = alpha * l + jnp.sum(p0, axis=0, keepdims=True) + jnp.sum(p1, axis=0, keepdims=True)
        acc = (acc * alpha + jnp.dot(vt_ref[0, j0], p0.astype(BF16), preferred_element_type=F32)
               + jnp.dot(vt_ref[0, j1], p1.astype(BF16), preferred_element_type=F32))
        return m_new, l, acc

    _, l, acc = lax.fori_loop(0, (i + 1) // 2, body, (m0, l0, acc0))
    out = acc / l
    row = lax.broadcasted_iota(jnp.int32, (LANES, blk), 0)
    o_ref[...] = jnp.where(row < HEAD_DIM, out[:, 0:blk], out[:, blk:2 * blk]).T


def _attention(q, k, vt, pen, B, S):
    T = B * S
    nb = S // MOBA_BLOCK
    blk = MOBA_BLOCK
    n_pairs = ATT_WIDTH // LANES
    return pl.pallas_call(
        _attn_kernel,
        grid=(B, n_pairs, nb),
        in_specs=[pl.BlockSpec((blk, LANES), lambda b, hp, i: (b * nb + i, hp)),
                  pl.BlockSpec((1, nb, blk, LANES), lambda b, hp, i: (b, 0, 0, hp)),
                  pl.BlockSpec((1, nb, LANES, blk), lambda b, hp, i: (b, 0, hp, 0)),
                  pl.BlockSpec((1, 1, nb, 2 * blk), lambda b, hp, i: (b, hp, 0, i))],
        out_specs=pl.BlockSpec((blk, LANES), lambda b, hp, i: (b * nb + i, hp)),
        out_shape=jax.ShapeDtypeStruct((T, ATT_WIDTH), F32),
        compiler_params=_params("arbitrary", "arbitrary", "arbitrary"),
        name="moba_attn",
    )(q, k, vt, pen)


def _lru_kernel(p_ref, cw_ref, cb_ref, ga_ref, gab_ref, gx_ref, gxb_ref, lam_ref, o_ref, xbuf, hc):
    t = pl.program_id(1)
    tc = LRU_CHUNK
    pad = 8

    @pl.when(t == 0)
    def _():
        xbuf[0:pad, :] = jnp.zeros((pad, LRU_WIDTH), F32)
        hc[...] = jnp.zeros_like(hc)

    @pl.when(t > 0)
    def _():
        xbuf[0:pad, :] = xbuf[tc:tc + pad, :]

    x = p_ref[:, 0:LRU_WIDTH]
    gate = p_ref[:, LRU_WIDTH:2 * LRU_WIDTH]
    xbuf[pad:pad + tc, :] = x
    xc = cb_ref[...] + x * cw_ref[CONV_WIDTH - 1:CONV_WIDTH, :]
    for j in range(CONV_WIDTH - 1):
        back = CONV_WIDTH - 1 - j
        xc = xc + xbuf[pad - back:pad - back + tc, :] * cw_ref[j:j + 1, :]

    r = _sigmoid(jnp.dot(xc, ga_ref[...], precision=HI, preferred_element_type=F32) + gab_ref[...])
    ig = _sigmoid(jnp.dot(xc, gx_ref[...], precision=HI, preferred_element_type=F32) + gxb_ref[...])
    log_a = -LRU_C * r * _softplus(-lam_ref[...])
    a = jnp.exp(log_a)
    u = jnp.sqrt(-_expm1(2.0 * log_a)) * (ig * xc)

    rows = lax.broadcasted_iota(jnp.int32, (tc, LRU_WIDTH), 0)
    d = 1
    while d < tc:
        keep = rows >= d
        a_prev = jnp.where(keep, pltpu.roll(a, d, 0), 1.0)
        u_prev = jnp.where(keep, pltpu.roll(u, d, 0), 0.0)
        u = a * u_prev + u
        a = a * a_prev
        d *= 2
    h = u + a * hc[...]
    hc[...] = h[tc - 1:tc, :]
    gl = 0.5 * gate * (1.0 + jnp.tanh(0.7978845608028654 * (gate + 0.044715 * gate * gate * gate)))
    o_ref[...] = h * gl


def _rglru(proj, cw, cb, ga, gab, gx, gxb, lam, B, S):
    T = B * S
    tc = LRU_CHUNK
    nt = S // tc
    col = (3 * ATT_WIDTH) // (2 * LRU_WIDTH)
    vec = lambda: pl.BlockSpec((1, LRU_WIDTH), lambda b, t: (0, 0))
    mat = lambda: pl.BlockSpec((LRU_WIDTH, LRU_WIDTH), lambda b, t: (0, 0))
    return pl.pallas_call(
        _lru_kernel,
        grid=(B, nt),
        in_specs=[pl.BlockSpec((tc, 2 * LRU_WIDTH), lambda b, t: (b * nt + t, col)),
                  pl.BlockSpec((CONV_WIDTH, LRU_WIDTH), lambda b, t: (0, 0)),
                  vec(), mat(), vec(), mat(), vec(), vec()],
        out_specs=pl.BlockSpec((tc, LRU_WIDTH), lambda b, t: (b * nt + t, 0)),
        out_shape=jax.ShapeDtypeStruct((T, LRU_WIDTH), F32),
        scratch_shapes=[pltpu.VMEM((tc + 8, LRU_WIDTH), F32), pltpu.VMEM((1, LRU_WIDTH), F32)],
        compiler_params=_params("arbitrary", "arbitrary"),
        name="rglru",
    )(proj, cw, cb, ga, gab, gx, gxb, lam)


def _rwkv_prep_kernel(p_ref, pp_ref, mu_ref, wlr_ref, w0_ref, a0_ref, kk_ref, ka_ref, ones_ref,
                      r_ref, ld_ref, k2_ref, v_ref, kn_ref, a_ref, g_ref):
    i = pl.program_id(1)
    W = RWKV_WIDTH
    p = p_ref[...]
    rows = lax.broadcasted_iota(jnp.int32, p.shape, 0)
    prev_last = jnp.where(i == 0, 0.0, pp_ref[7:8, :])
    p_prev = jnp.where(rows == 0, prev_last, pltpu.roll(p, 1, 0))
    pf = p + (p_prev - p) * mu_ref[...]
    r = pf[:, 0:W]
    k = pf[:, W:2 * W]
    v = pf[:, 2 * W:3 * W]
    z = pf[:, 3 * W:4 * W]
    lane = lax.broadcasted_iota(jnp.int32, z.shape, 1)
    zz = jnp.where(lane < 64, jnp.tanh(z), jnp.where(lane < 128, z, _sigmoid(z)))
    lr = jnp.dot(zz, wlr_ref[...], precision=HI, preferred_element_type=F32)
    w = -_softplus(-(w0_ref[...] + lr[:, 0:W])) - 0.5
    a = _sigmoid(a0_ref[...] + lr[:, W:2 * W])
    kk = k * kk_ref[...]
    ssq = jnp.dot(kk * kk, ones_ref[...], precision=HI, preferred_element_type=F32)
    kk = kk / jnp.maximum(jnp.sqrt(ssq), 1e-12)
    r_ref[...] = r
    ld_ref[...] = -jnp.exp(w)
    k2_ref[...] = k * (1.0 + (a - 1.0) * ka_ref[...])
    v_ref[...] = v
    kn_ref[...] = kk
    a_ref[...] = a
    g_ref[...] = lr[:, 2 * W:3 * W]


def _rwkv_prep(proj, mu, wlr, w0, a0, k_k, k_a, head_ones, B, S):
    T = B * S
    tr = RWKV_PREP_ROWS
    nt = S // tr
    W = RWKV_WIDTH
    col = 2
    vec = lambda: pl.BlockSpec((1, W), lambda b, t: (0, 0))
    out = lambda: pl.BlockSpec((tr, W), lambda b, t: (b * nt + t, 0))
    return pl.pallas_call(
        _rwkv_prep_kernel,
        grid=(B, nt),
        in_specs=[pl.BlockSpec((tr, 4 * W), lambda b, t: (b * nt + t, col)),
                  pl.BlockSpec((8, 4 * W), lambda b, t: (jnp.maximum((b * nt + t) * (tr // 8) - 1, 0), col)),
                  pl.BlockSpec((1, 4 * W), lambda b, t: (0, 0)),
                  pl.BlockSpec((W, 3 * W), lambda b, t: (0, 0)),
                  vec(), vec(), vec(), vec(),
                  pl.BlockSpec((W, W), lambda b, t: (0, 0))],
        out_specs=[out() for _ in range(7)],
        out_shape=[jax.ShapeDtypeStruct((T, W), F32) for _ in range(7)],
        compiler_params=_params("arbitrary", "arbitrary"),
        name="rwkv_prep",
    )(proj, proj, mu, wlr, w0, a0, k_k, k_a, head_ones)


def _wkv_kernel(r_ref, ld_ref, k2_ref, v_ref, kn_ref, a_ref, y_ref, st_ref):
    c = pl.program_id(1)
    C = WKV_CHUNK
    W = RWKV_WIDTH
    H = RWKV_HEADS

    @pl.when(c == 0)
    def _():
        st_ref[...] = jnp.zeros_like(st_ref)

    def mm(x, y):
        return jnp.dot(x.astype(BF16), y.astype(BF16), preferred_element_type=F32)

    def mm_nt(x, y):
        return lax.dot_general(x.astype(BF16), y.astype(BF16), NT_DIMS, preferred_element_type=F32)

    def mm_tn(x, y):
        return lax.dot_general(x.astype(BF16), y.astype(BF16), TN_DIMS, preferred_element_type=F32)

    tr = lax.broadcasted_iota(jnp.int32, (C, C), 0)
    tc = lax.broadcasted_iota(jnp.int32, (C, C), 1)
    tri = jnp.where(tr >= tc, 1.0, 0.0)
    row = lax.broadcasted_iota(jnp.int32, (W, W), 0)
    colm = lax.broadcasted_iota(jnp.int32, (W, W), 1)
    same_head = (row // C) == (colm // HEAD_DIM)
    strict = lambda x: jnp.where(same_head, jnp.where(row > colm, x, 0.0), 0.0)
    incl = lambda x: jnp.where(same_head, jnp.where(row >= colm, x, 0.0), 0.0)
    same16 = (row // 16) == (colm // 16)
    eye = jnp.where(row == colm, 1.0, 0.0)

    def stack(x):
        return jnp.where(same_head, jnp.concatenate([x] * H, axis=0), 0.0)

    for gi in range(WKV_GROUP):
        r = r_ref[gi]
        ld = ld_ref[gi]
        k2 = k2_ref[gi]
        v = v_ref[gi]
        kn = kn_ref[gi]
        b_s = kn * a_ref[gi]
        cl = jnp.dot(tri, ld, precision=HI, preferred_element_type=F32)
        cl_end = cl[C - 1:C, :]
        e_neg = jnp.exp(-cl)
        e_end = jnp.exp(cl_end - cl)

        ar_s = jnp.concatenate([stack(-kn * jnp.exp(cl - ld)), stack(r * jnp.exp(cl))], axis=0)
        bk_s = jnp.concatenate([stack(b_s * e_neg), stack(k2 * e_neg)], axis=0)
        v_s = stack(v)
        end_s = jnp.concatenate([stack(b_s * e_end), stack(k2 * e_end)], axis=0)

        prod = mm_nt(ar_s, bk_s)
        n_ab = strict(prod[0:W, 0:W])
        a_ak = strict(prod[0:W, W:2 * W])
        a_rb = incl(prod[W:2 * W, 0:W])
        a_rk = incl(prod[W:2 * W, W:2 * W])

        nd = jnp.where(same16, n_ab, 0.0)
        lo = n_ab - nd
        n2 = mm(nd, nd)
        n4 = mm(n2, n2)
        n8 = mm(n4, n4)
        p1 = eye + nd
        p2 = p1 + mm(p1, n2)
        p3 = p2 + mm(p2, n4)
        dinv = p3 + mm(p3, n8)
        x1 = mm(dinv, lo)
        x2 = mm(x1, x1)
        y1 = eye + x1
        tinv = mm(y1 + mm(y1, x2), dinv)

        st = st_ref[gi]
        ar_st = mm_nt(ar_s, st)
        u_s = mm(tinv, ar_st[0:W] + mm(a_ak, v_s))
        y_s = ar_st[W:2 * W] + mm(a_rb, u_s) + mm(a_rk, v_s)
        y = y_s[0:C, :]
        for h in range(1, H):
            y = y + y_s[h * C:(h + 1) * C, :]
        y_ref[gi] = y
        st_ref[gi] = st * jnp.exp(cl_end) + mm_tn(jnp.concatenate([u_s, v_s], axis=0), end_s)


def _wkv(r, ld, k2, v, kn, a, B, S):
    C = WKV_CHUNK
    nc = S // C
    W = RWKV_WIDTH
    G = WKV_GROUP
    blk = lambda: pl.BlockSpec((G, C, W), lambda b, c: (b, c, 0))
    seq = lambda t: t.reshape(B, S, W)
    y = pl.pallas_call(
        _wkv_kernel,
        grid=(B // G, nc),
        in_specs=[blk() for _ in range(6)],
        out_specs=blk(),
        out_shape=jax.ShapeDtypeStruct((B, S, W), F32),
        scratch_shapes=[pltpu.VMEM((G, W, W), F32)],
        compiler_params=_params("arbitrary", "arbitrary"),
        name="wkv7",
    )(seq(r), seq(ld), seq(k2), seq(v), seq(kn), seq(a))
    return y.reshape(B * S, W)


def _wkv_post_kernel(y_ref, r_ref, k2_ref, v_ref, g_ref, rk_ref, lg_ref, lb_ref, ones_ref, o_ref):
    ones = ones_ref[...]

    def head_sum(x):
        hi = x.astype(BF16)
        lo = (x - hi.astype(F32)).astype(BF16)
        return (jnp.dot(hi, ones, preferred_element_type=F32) + jnp.dot(lo, ones, preferred_element_type=F32))

    y = y_ref[...]
    r = r_ref[...]
    k2 = k2_ref[...]
    inv_n = 1.0 / HEAD_DIM
    d = y - head_sum(y) * inv_n
    var = head_sum(d * d) * inv_n
    yn = d * lax.rsqrt(var + RWKV_LN_EPS) * lg_ref[...] + lb_ref[...]
    bonus = head_sum(r * k2 * rk_ref[...]) * v_ref[...]
    o_ref[...] = (yn + bonus) * g_ref[...]


def _wkv_post(y, r, k2, v, g, r_k, lnx_g, lnx_b, head_ones_bf16):
    T, W = y.shape
    tm = 512
    blk = lambda: pl.BlockSpec((tm, W), lambda m: (m, 0))
    vec = lambda: pl.BlockSpec((1, W), lambda m: (0, 0))
    return pl.pallas_call(
        _wkv_post_kernel,
        grid=(T // tm,),
        in_specs=[blk() for _ in range(5)] + [vec(), vec(), vec(), pl.BlockSpec((W, W), lambda m: (0, 0))],
        out_specs=blk(),
        out_shape=jax.ShapeDtypeStruct((T, W), F32),
        compiler_params=_params("arbitrary"),
        name="wkv_post",
    )(y, r, k2, v, g, r_k, lnx_g, lnx_b, head_ones_bf16)


def _outproj_ln_kernel(att_ref, lru_ref, rwk_ref, x_ref, wa_ref, wl_ref, wr_ref, g_ref, b_ref, o_ref, ob_ref):
    h = jnp.dot(att_ref[...].astype(BF16), wa_ref[...], preferred_element_type=F32)
    h = h + jnp.dot(lru_ref[...].astype(BF16), wl_ref[...], preferred_element_type=F32)
    h = h + jnp.dot(rwk_ref[...].astype(BF16), wr_ref[...], preferred_element_type=F32)
    y = _layer_norm(DEEPNORM_ALPHA * x_ref[...] + h, g_ref[...], b_ref[...])
    o_ref[...] = y
    ob_ref[...] = y.astype(BF16)


def _outproj_ln(att, lru, rwk, x, wa, wl, wr, g, b):
    T, D = x.shape
    tm = 512
    row = lambda w: pl.BlockSpec((tm, w), lambda m: (m, 0))
    full = lambda a: pl.BlockSpec(a.shape, lambda m: (0, 0))
    return pl.pallas_call(
        _outproj_ln_kernel,
        grid=(T // tm,),
        in_specs=[row(ATT_WIDTH), row(LRU_WIDTH), row(RWKV_WIDTH), row(D), full(wa), full(wl), full(wr), full(g), full(b)],
        out_specs=[row(D), row(D)],
        out_shape=[jax.ShapeDtypeStruct((T, D), F32), jax.ShapeDtypeStruct((T, D), BF16)],
        compiler_params=_params("arbitrary"),
        name="outproj_ln",
    )(att, lru, rwk, x, wa, wl, wr, g, b)


def _ffn_kernel(xb_ref, x_ref, wg_ref, wu_ref, wd_ref, g_ref, b_ref, o_ref, ob_ref, acc_ref, *, nf):
    f = pl.program_id(1)
    xb = xb_ref[...]
    hg = jnp.dot(xb, wg_ref[...], preferred_element_type=F32)
    hu = jnp.dot(xb, wu_ref[...], preferred_element_type=F32)
    hh = (hg * _sigmoid(hg) * hu).astype(BF16)
    contrib = jnp.dot(hh, wd_ref[...], preferred_element_type=F32)

    @pl.when(f == 0)
    def _():
        acc_ref[...] = contrib

    @pl.when(f > 0)
    def _():
        acc_ref[...] = acc_ref[...] + contrib

    @pl.when(f == nf - 1)
    def _():
        y = _layer_norm(DEEPNORM_ALPHA * x_ref[...] + acc_ref[...], g_ref[...], b_ref[...])
        o_ref[...] = y
        ob_ref[...] = y.astype(BF16)


def _ffn_ln(xb, x, wg, wu, wd, g, b):
    T, D = x.shape
    F = wg.shape[1]
    tm = 512
    nf = 2
    tf = F // nf
    return pl.pallas_call(
        functools.partial(_ffn_kernel, nf=nf),
        grid=(T // tm, nf),
        in_specs=[pl.BlockSpec((tm, D), lambda m, f: (m, 0)),
                  pl.BlockSpec((tm, D), lambda m, f: (m, 0)),
                  pl.BlockSpec((D, tf), lambda m, f: (0, f)),
                  pl.BlockSpec((D, tf), lambda m, f: (0, f)),
                  pl.BlockSpec((tf, D), lambda m, f: (f, 0)),
                  pl.BlockSpec((1, D), lambda m, f: (0, 0)),
                  pl.BlockSpec((1, D), lambda m, f: (0, 0))],
        out_specs=[pl.BlockSpec((tm, D), lambda m, f: (m, 0)), pl.BlockSpec((tm, D), lambda m, f: (m, 0))],
        out_shape=[jax.ShapeDtypeStruct((T, D), F32), jax.ShapeDtypeStruct((T, D), BF16)],
        scratch_shapes=[pltpu.VMEM((tm, D), F32)],
        compiler_params=_params("arbitrary", "arbitrary"),
        name="ffn_ln",
    )(xb, x, wg, wu, wd, g, b)


def _router_kernel(x_ref, wr_ref, g_ref):
    logits = jnp.dot(x_ref[...], wr_ref[...], precision=HI, preferred_element_type=F32)
    lane = lax.broadcasted_iota(jnp.int32, logits.shape, 1).astype(F32)
    logits = jnp.where(lane < N_EXPERTS, logits, -jnp.inf)
    m1 = jnp.max(logits, axis=1, keepdims=True)
    i1 = jnp.min(jnp.where(logits == m1, lane, float(LANES)), axis=1, keepdims=True)
    rest = jnp.where(lane == i1, -jnp.inf, logits)
    m2 = jnp.max(rest, axis=1, keepdims=True)
    i2 = jnp.min(jnp.where(rest == m2, lane, float(LANES)), axis=1, keepdims=True)
    e = jnp.exp(m2 - m1)
    g1 = 1.0 / (1.0 + e)
    g2 = e / (1.0 + e)
    g_ref[...] = jnp.where(lane == i1, g1, jnp.where(lane == i2, g2, 0.0))


def _router(x, wr_pad):
    T, D = x.shape
    tm = 1024
    return pl.pallas_call(
        _router_kernel,
        grid=(T // tm,),
        in_specs=[pl.BlockSpec((tm, D), lambda m: (m, 0)), pl.BlockSpec((D, LANES), lambda m: (0, 0))],
        out_specs=pl.BlockSpec((tm, LANES), lambda m: (m, 0)),
        out_shape=jax.ShapeDtypeStruct((T, LANES), F32),
        compiler_params=_params("arbitrary"),
        name="moe_router",
    )(x, wr_pad)


def _moe_kernel(xb_ref, x_ref, gate_ref, wg_ref, wu_ref, wd_ref, g_ref, b_ref, o_ref, acc_ref, *, nf):
    e = pl.program_id(1)
    f = pl.program_id(2)
    xb = xb_ref[...]
    gates = gate_ref[...]
    lane = lax.broadcasted_iota(jnp.int32, gates.shape, 1)
    ge = jnp.sum(jnp.where(lane == e, gates, 0.0), axis=1, keepdims=True)
    hg = jnp.dot(xb, wg_ref[0], preferred_element_type=F32)
    hu = jnp.dot(xb, wu_ref[0], preferred_element_type=F32)
    hh = (hg * _sigmoid(hg) * hu * ge).astype(BF16)
    contrib = jnp.dot(hh, wd_ref[0], preferred_element_type=F32)
    first = jnp.logical_and(e == 0, f == 0)

    @pl.when(first)
    def _():
        acc_ref[...] = contrib

    @pl.when(jnp.logical_not(first))
    def _():
        acc_ref[...] = acc_ref[...] + contrib

    @pl.when(jnp.logical_and(e == N_EXPERTS - 1, f == nf - 1))
    def _():
        o_ref[...] = _layer_norm(DEEPNORM_ALPHA * x_ref[...] + acc_ref[...], g_ref[...], b_ref[...])


def _moe_ln(xb, x, gates, wg, wu, wd, g, b):
    T, D = x.shape
    F = wg.shape[2]
    tm = 1024
    nf = 4
    tf = F // nf
    return pl.pallas_call(
        functools.partial(_moe_kernel, nf=nf),
        grid=(T // tm, N_EXPERTS, nf),
        in_specs=[pl.BlockSpec((tm, D), lambda m, e, f: (m, 0)),
                  pl.BlockSpec((tm, D), lambda m, e, f: (m, 0)),
                  pl.BlockSpec((tm, LANES), lambda m, e, f: (m, 0)),
                  pl.BlockSpec((1, D, tf), lambda m, e, f: (e, 0, f)),
                  pl.BlockSpec((1, D, tf), lambda m, e, f: (e, 0, f)),
                  pl.BlockSpec((1, tf, D), lambda m, e, f: (e, f, 0)),
                  pl.BlockSpec((1, D), lambda m, e, f: (0, 0)),
                  pl.BlockSpec((1, D), lambda m, e, f: (0, 0))],
        out_specs=pl.BlockSpec((tm, D), lambda m, e, f: (m, 0)),
        out_shape=jax.ShapeDtypeStruct((T, D), F32),
        scratch_shapes=[pltpu.VMEM((tm, D), F32)],
        compiler_params=_params("arbitrary", "arbitrary", "arbitrary"),
        name="moe_ln",
    )(xb, x, gates, wg, wu, wd, g, b)


def _block_diag(w):
    n, d, _ = w.shape
    out = jnp.zeros((n * d, n * d), w.dtype)
    for i in range(n):
        out = out.at[i * d:(i + 1) * d, i * d:(i + 1) * d].set(w[i])
    return out


def _rope_tables(positions):
    B, S = positions.shape
    half = ROPE_DIMS // 2
    inv_freq = ROPE_THETA ** (-jnp.arange(0, ROPE_DIMS, 2, dtype=F32) / ROPE_DIMS)
    ang = positions.astype(F32)[..., None] * inv_freq
    cos = jnp.cos(ang).reshape(B * S, half)
    sin = jnp.sin(ang).reshape(B * S, half)
    zeros = jnp.zeros((B * S, HEAD_DIM - ROPE_DIMS), F32)
    zh = jnp.zeros((B * S, half), F32)
    c = jnp.concatenate([cos, cos, zeros + 1.0], axis=1)
    s1 = jnp.concatenate([-sin, zh, zeros], axis=1)
    s2 = jnp.concatenate([zh, sin, zeros], axis=1)
    tile = lambda t: jnp.concatenate([t] * (LANES // HEAD_DIM), axis=1)
    return tile(c), tile(s1), tile(s2)


def _mixer_layer(x, xb, rope, B, S, w_in, conv_w, conv_b, ga_w, ga_b, gx_w, gx_b, lam,
                 mu, w0, w_up, a0, a_up, g_up, k_k, k_a, r_k, lnx_g, lnx_b, w_out, ln_g, ln_b):
    W = RWKV_WIDTH
    row = lambda t: t.reshape(1, -1)
    proj = _in_proj(xb, w_in.astype(BF16))
    q, k, vt, pen = _attn_prep(proj, *rope, B, S)
    att = _attention(q, k, vt, pen, B, S)
    lru = _rglru(proj, conv_w, row(conv_b), _block_diag(ga_w), row(ga_b), _block_diag(gx_w), row(gx_b), row(lam), B, S)
    head_ones = _block_diag(jnp.ones((RWKV_HEADS, HEAD_DIM, HEAD_DIM), F32))
    wlr = jnp.zeros((W, 3 * W), F32)
    wlr = wlr.at[0:64, 0:W].set(w_up).at[64:128, W:2 * W].set(a_up).at[128:256, 2 * W:3 * W].set(g_up)
    r, ld, k2, v, kn, a, g = _rwkv_prep(proj, row(mu), wlr, row(w0), row(a0), row(k_k), row(k_a), head_ones, B, S)
    y = _wkv(r, ld, k2, v, kn, a, B, S)
    rwk = _wkv_post(y, r, k2, v, g, row(r_k), row(lnx_g), row(lnx_b), head_ones.astype(BF16))
    wo = w_out.astype(BF16)
    return _outproj_ln(att, lru, rwk, x, wo[0:ATT_WIDTH], wo[ATT_WIDTH:ATT_WIDTH + LRU_WIDTH],
                       wo[ATT_WIDTH + LRU_WIDTH:], row(ln_g), row(ln_b))


def kernel(x, positions, w_in, lru_conv_w, lru_conv_b, lru_ga_w, lru_ga_b, lru_gx_w, lru_gx_b, lru_lambda, rwkv_mu, rwkv_w0, rwkv_w_up, rwkv_a0, rwkv_a_up, rwkv_g_up, rwkv_k_k, rwkv_k_a, rwkv_r_k, rwkv_lnx_g, rwkv_lnx_b, w_out, ln1_g, ln1_b, ffn_w_gate, ffn_w_up, ffn_w_down, moe_router, moe_w_gate, moe_w_up, moe_w_down, ln2_g, ln2_b):
    B, S, D = x.shape
    T = B * S
    rope = _rope_tables(positions)
    xf = x.reshape(T, D)
    xb = xf.astype(BF16)
    for l in range(DEPTH):
        xf, xb = _mixer_layer(xf, xb, rope, B, S, w_in[l], lru_conv_w[l], lru_conv_b[l], lru_ga_w[l], lru_ga_b[l],
                              lru_gx_w[l], lru_gx_b[l], lru_lambda[l], rwkv_mu[l], rwkv_w0[l], rwkv_w_up[l],
                              rwkv_a0[l], rwkv_a_up[l], rwkv_g_up[l], rwkv_k_k[l], rwkv_k_a[l], rwkv_r_k[l],
                              rwkv_lnx_g[l], rwkv_lnx_b[l], w_out[l], ln1_g[l], ln1_b[l])
        g2 = ln2_g[l].reshape(1, D)
        b2 = ln2_b[l].reshape(1, D)
        if l % 2 == 0:
            i = l // 2
            xf, xb = _ffn_ln(xb, xf, ffn_w_gate[i].astype(BF16), ffn_w_up[i].astype(BF16),
                             ffn_w_down[i].astype(BF16), g2, b2)
        else:
            i = l // 2
            wr_pad = jnp.zeros((D, LANES), F32).at[:, 0:N_EXPERTS].set(moe_router[i])
            gates = _router(xf, wr_pad)
            xf = _moe_ln(xb, xf, gates, moe_w_gate[i].astype(BF16), moe_w_up[i].astype(BF16),
                         moe_w_down[i].astype(BF16), g2, b2)
            xb = xf.astype(BF16)
    return xf.reshape(B, S, D)
```

```python
import functools

import jax
import jax.numpy as jnp
from jax import lax
from jax.experimental import pallas as pl
from jax.experimental.pallas import tpu as pltpu

F32 = jnp.float32
BF16 = jnp.bfloat16
HI = lax.Precision.HIGHEST
NT_DIMS = (((1,), (1,)), ((), ()))
TN_DIMS = (((0,), (0,)), ((), ()))

HEAD_DIM = 64
ATT_HEADS = 8
ATT_WIDTH = ATT_HEADS * HEAD_DIM
LRU_WIDTH = 256
RWKV_HEADS = 4
RWKV_WIDTH = 256
ROPE_DIMS = 16
ROPE_THETA = 500000.0
MOBA_BLOCK = 256
MOBA_TOPK = 3
CONV_WIDTH = 4
LRU_C = 8.0
RWKV_LN_EPS = 64e-5
N_EXPERTS = 8
LN_EPS = 1e-5
DEPTH = 2
DEEPNORM_ALPHA = (2 * DEPTH) ** 0.25
NEG_INF = -1e30
LOG2_E = 1.4426950408889634

LANES = 128
VMEM_LIMIT = 56 * 1024 * 1024

WKV_CHUNK = 64
WKV_GROUP = 2
LRU_CHUNK = 256
RWKV_PREP_ROWS = 512
MOE_TILE = 1024
MOE_ROWS = 256


def _params(*sem):
    return pltpu.CompilerParams(dimension_semantics=sem, vmem_limit_bytes=VMEM_LIMIT)


def _sigmoid(x):
    return 1.0 / (1.0 + jnp.exp(-x))


def _softplus(x):
    return jnp.maximum(x, 0.0) + jnp.log1p(jnp.exp(-jnp.abs(x)))


def _expm1(z):
    u = jnp.exp(z)
    um1 = u - 1.0
    return jnp.where(u == 1.0, z, jnp.where(um1 == -1.0, -1.0, um1 * z / jnp.log(u)))


def _layer_norm(y, g, b):
    m = jnp.mean(y, axis=-1, keepdims=True)
    d = y - m
    var = jnp.mean(d * d, axis=-1, keepdims=True)
    return d * lax.rsqrt(var + LN_EPS) * g + b


def _matmul_kernel(x_ref, w_ref, o_ref):
    o_ref[...] = jnp.dot(x_ref[...], w_ref[...], preferred_element_type=F32)


def _in_proj(xb, w):
    T, D = xb.shape
    N = w.shape[1]
    tm, tn = 512, 1024
    return pl.pallas_call(
        _matmul_kernel,
        grid=(N // tn, T // tm),
        in_specs=[pl.BlockSpec((tm, D), lambda n, m: (m, 0)),
                  pl.BlockSpec((D, tn), lambda n, m: (0, n))],
        out_specs=pl.BlockSpec((tm, tn), lambda n, m: (m, n)),
        out_shape=jax.ShapeDtypeStruct((T, N), F32),
        compiler_params=_params("arbitrary", "arbitrary"),
        name="in_proj",
    )(xb, w)


def _attn_prep_kernel(proj_ref, c_ref, s1_ref, s2_ref, q_ref, k_ref, vt_ref, pen_ref, km_ref, *, nb):
    i = pl.program_id(1)

    @pl.when(i == 0)
    def _():
        km_ref[...] = jnp.zeros_like(km_ref)

    c = c_ref[...]
    s1 = s1_ref[...]
    s2 = s2_ref[...]

    def rope(xt):
        return xt * c + pltpu.roll(xt, LANES - ROPE_DIMS // 2, 1) * s1 + pltpu.roll(xt, ROPE_DIMS // 2, 1) * s2

    q_tiles = []
    for ct in range(ATT_WIDTH // LANES):
        lo, hi = ct * LANES, (ct + 1) * LANES
        qr = rope(proj_ref[:, lo:hi])
        q_tiles.append(qr)
        q_ref[:, lo:hi] = (qr * (HEAD_DIM ** -0.5 * LOG2_E)).astype(BF16)
        kr = rope(proj_ref[:, ATT_WIDTH + lo:ATT_WIDTH + hi])
        k_ref[0, 0, :, lo:hi] = kr.astype(BF16)
        km_row = lax.broadcasted_iota(jnp.int32, (nb, LANES), 0)
        km_ref[:, lo:hi] = jnp.where(km_row == i, jnp.mean(kr, axis=0, keepdims=True), km_ref[:, lo:hi])
    vt_ref[0, 0] = proj_ref[:, 2 * ATT_WIDTH:3 * ATT_WIDTH].T.astype(BF16)

    q_rot = jnp.concatenate(q_tiles, axis=1)
    km = km_ref[...]
    lane_head = lax.broadcasted_iota(jnp.int32, km.shape, 1) // HEAD_DIM
    n_iota = lax.broadcasted_iota(jnp.int32, (nb, MOBA_BLOCK), 0)
    past = n_iota < i
    for h in range(ATT_HEADS):
        kmh = jnp.where(lane_head == h, km, 0.0)
        g = lax.dot_general(kmh, q_rot, NT_DIMS, precision=HI, preferred_element_type=F32)
        rank = jnp.zeros((nb, MOBA_BLOCK), F32)
        for m in range(nb):
            gm = g[m:m + 1, :]
            beats = jnp.where(gm > g, 1.0, jnp.where(gm == g, jnp.where(n_iota > m, 1.0, 0.0), 0.0))
            rank = rank + jnp.where(m < i, beats, 0.0)
        pen = jnp.where(past, jnp.where(rank < float(MOBA_TOPK), 0.0, NEG_INF), NEG_INF)
        half = (h % 2) * MOBA_BLOCK
        pen_ref[0, h // 2, :, half:half + MOBA_BLOCK] = pen


def _attn_prep(proj, rope_c, rope_s1, rope_s2, B, S):
    T = B * S
    nb = S // MOBA_BLOCK
    blk = MOBA_BLOCK
    return pl.pallas_call(
        functools.partial(_attn_prep_kernel, nb=nb),
        grid=(B, nb),
        in_specs=[pl.BlockSpec((blk, 3 * ATT_WIDTH), lambda b, i: (b * nb + i, 0)),
                  pl.BlockSpec((blk, LANES), lambda b, i: (b * nb + i, 0)),
                  pl.BlockSpec((blk, LANES), lambda b, i: (b * nb + i, 0)),
                  pl.BlockSpec((blk, LANES), lambda b, i: (b * nb + i, 0))],
        out_specs=[pl.BlockSpec((blk, ATT_WIDTH), lambda b, i: (b * nb + i, 0)),
                   pl.BlockSpec((1, 1, blk, ATT_WIDTH), lambda b, i: (b, i, 0, 0)),
                   pl.BlockSpec((1, 1, ATT_WIDTH, blk), lambda b, i: (b, i, 0, 0)),
                   pl.BlockSpec((1, ATT_HEADS // 2, nb, 2 * blk), lambda b, i: (b, 0, 0, i))],
        out_shape=[jax.ShapeDtypeStruct((T, ATT_WIDTH), BF16),
                   jax.ShapeDtypeStruct((B, nb, blk, ATT_WIDTH), BF16),
                   jax.ShapeDtypeStruct((B, nb, ATT_WIDTH, blk), BF16),
                   jax.ShapeDtypeStruct((B, ATT_HEADS // 2, nb, 2 * S), F32)],
        scratch_shapes=[pltpu.VMEM((nb, ATT_WIDTH), F32)],
        compiler_params=_params("arbitrary", "arbitrary"),
        name="attn_prep",
    )(proj, rope_c, rope_s1, rope_s2)


def _attn_kernel(q_ref, k_ref, vt_ref, pen_ref, o_ref):
    i = pl.program_id(2)
    blk = MOBA_BLOCK
    q = q_ref[...]
    lane_head = lax.broadcasted_iota(jnp.int32, q.shape, 1) // HEAD_DIM
    zero = jnp.zeros_like(q)
    qs = jnp.concatenate([jnp.where(lane_head == 0, q, zero), jnp.where(lane_head == 1, q, zero)], axis=0)
    kidx = lax.broadcasted_iota(jnp.int32, (blk, 2 * blk), 0)
    qidx = lax.broadcasted_iota(jnp.int32, (blk, 2 * blk), 1) % blk
    s = lax.dot_general(k_ref[0, i], qs, NT_DIMS, preferred_element_type=F32)
    s = jnp.where(kidx <= qidx, s, NEG_INF)
    m0 = jnp.max(s, axis=0, keepdims=True)
    p = jnp.exp2(s - m0)
    l0 = jnp.sum(p, axis=0, keepdims=True)
    acc0 = jnp.dot(vt_ref[0, i], p.astype(BF16), preferred_element_type=F32)

    def body(jj, carry):
        m, l, acc = carry
        j0 = 2 * jj
        j1 = j0 + 1
        s0 = lax.dot_general(k_ref[0, j0], qs, NT_DIMS, preferred_element_type=F32) + pen_ref[0, 0, pl.ds(j0, 1), :]
        s1 = lax.dot_general(k_ref[0, j1], qs, NT_DIMS, preferred_element_type=F32) + pen_ref[0, 0, pl.ds(j1, 1), :]
        m_new = jnp.maximum(m, jnp.maximum(jnp.max(s0, axis=0, keepdims=True), jnp.max(s1, axis=0, keepdims=True)))
        alpha = jnp.exp2(m - m_new)
        p0 = jnp.exp2(s0 - m_new)
        p1 = jnp.exp2(s1 - m_new)
        l = alpha * l + jnp.sum(p0, axis=0, keepdims=True) + jnp.sum(p1, axis=0, keepdims=True)
        acc = (acc * alpha + jnp.dot(vt_ref[0, j0], p0.astype(BF16), preferred_element_type=F32)
               + jnp.dot(vt_ref[0, j1], p1.astype(BF16), preferred_element_type=F32))
        return m_new, l, acc

    _, l, acc = lax.fori_loop(0, (i + 1) // 2, body, (m0, l0, acc0))
    out = acc / l
    row = lax.broadcasted_iota(jnp.int32, (LANES, blk), 0)
    o_ref[...] = jnp.where(row < HEAD_DIM, out[:, 0:blk], out[:, blk:2 * blk]).T


def _attention(q, k, vt, pen, B, S):
    T = B * S
    nb = S // MOBA_BLOCK
    blk = MOBA_BLOCK
    n_pairs = ATT_WIDTH // LANES
    return pl.pallas_call(
        _attn_kernel,
        grid=(B, n_pairs, nb),
        in_specs=[pl.BlockSpec((blk, LANES), lambda b, hp, i: (b * nb + i, hp)),
                  pl.BlockSpec((1, nb, blk, LANES), lambda b, hp, i: (b, 0, 0, hp)),
                  pl.BlockSpec((1, nb, LANES, blk), lambda b, hp, i: (b, 0, hp, 0)),
                  pl.BlockSpec((1, 1, nb, 2 * blk), lambda b, hp, i: (b, hp, 0, i))],
        out_specs=pl.BlockSpec((blk, LANES), lambda b, hp, i: (b * nb + i, hp)),
        out_shape=jax.ShapeDtypeStruct((T, ATT_WIDTH), F32),
        compiler_params=_params("arbitrary", "arbitrary", "arbitrary"),
        name="moba_attn",
    )(q, k, vt, pen)


def _lru_kernel(p_ref, cw_ref, cb_ref, ga_ref, gab_ref, gx_ref, gxb_ref, lam_ref, o_ref, xbuf, hc):
    t = pl.program_id(1)
    tc = LRU_CHUNK
    pad = 8

    @pl.when(t == 0)
    def _():
        xbuf[0:pad, :] = jnp.zeros((pad, LRU_WIDTH), F32)
        hc[...] = jnp.zeros_like(hc)

    @pl.when(t > 0)
    def _():
        xbuf[0:pad, :] = xbuf[tc:tc + pad, :]

    x = p_ref[:, 0:LRU_WIDTH]
    gate = p_ref[:, LRU_WIDTH:2 * LRU_WIDTH]
    xbuf[pad:pad + tc, :] = x
    xc = cb_ref[...] + x * cw_ref[CONV_WIDTH - 1:CONV_WIDTH, :]
    for j in range(CONV_WIDTH - 1):
        back = CONV_WIDTH - 1 - j
        xc = xc + xbuf[pad - back:pad - back + tc, :] * cw_ref[j:j + 1, :]

    r = _sigmoid(jnp.dot(xc, ga_ref[...], precision=HI, preferred_element_type=F32) + gab_ref[...])
    ig = _sigmoid(jnp.dot(xc, gx_ref[...], precision=HI, preferred_element_type=F32) + gxb_ref[...])
    log_a = -LRU_C * r * _softplus(-lam_ref[...])
    a = jnp.exp(log_a)
    u = jnp.sqrt(-_expm1(2.0 * log_a)) * (ig * xc)

    rows = lax.broadcasted_iota(jnp.int32, (tc, LRU_WIDTH), 0)
    d = 1
    while d < tc:
        keep = rows >= d
        a_prev = jnp.where(keep, pltpu.roll(a, d, 0), 1.0)
        u_prev = jnp.where(keep, pltpu.roll(u, d, 0), 0.0)
        u = a * u_prev + u
        a = a * a_prev
        d *= 2
    h = u + a * hc[...]
    hc[...] = h[tc - 1:tc, :]
    gl = 0.5 * gate * (1.0 + jnp.tanh(0.7978845608028654 * (gate + 0.044715 * gate * gate * gate)))
    o_ref[...] = h * gl


def _rglru(proj, cw, cb, ga, gab, gx, gxb, lam, B, S):
    T = B * S
    tc = LRU_CHUNK
    nt = S // tc
    col = (3 * ATT_WIDTH) // (2 * LRU_WIDTH)
    vec = lambda: pl.BlockSpec((1, LRU_WIDTH), lambda b, t: (0, 0))
    mat = lambda: pl.BlockSpec((LRU_WIDTH, LRU_WIDTH), lambda b, t: (0, 0))
    return pl.pallas_call(
        _lru_kernel,
        grid=(B, nt),
        in_specs=[pl.BlockSpec((tc, 2 * LRU_WIDTH), lambda b, t: (b * nt + t, col)),
                  pl.BlockSpec((CONV_WIDTH, LRU_WIDTH), lambda b, t: (0, 0)),
                  vec(), mat(), vec(), mat(), vec(), vec()],
        out_specs=pl.BlockSpec((tc, LRU_WIDTH), lambda b, t: (b * nt + t, 0)),
        out_shape=jax.ShapeDtypeStruct((T, LRU_WIDTH), F32),
        scratch_shapes=[pltpu.VMEM((tc + 8, LRU_WIDTH), F32), pltpu.VMEM((1, LRU_WIDTH), F32)],
        compiler_params=_params("arbitrary", "arbitrary"),
        name="rglru",
    )(proj, cw, cb, ga, gab, gx, gxb, lam)


def _rwkv_prep_kernel(p_ref, pp_ref, mu_ref, wlr_ref, w0_ref, a0_ref, kk_ref, ka_ref, ones_ref,
                      r_ref, ld_ref, k2_ref, v_ref, kn_ref, a_ref, g_ref):
    i = pl.program_id(1)
    W = RWKV_WIDTH
    p = p_ref[...]
    rows = lax.broadcasted_iota(jnp.int32, p.shape, 0)
    prev_last = jnp.where(i == 0, 0.0, pp_ref[7:8, :])
    p_prev = jnp.where(rows == 0, prev_last, pltpu.roll(p, 1, 0))
    pf = p + (p_prev - p) * mu_ref[...]
    r = pf[:, 0:W]
    k = pf[:, W:2 * W]
    v = pf[:, 2 * W:3 * W]
    z = pf[:, 3 * W:4 * W]
    lane = lax.broadcasted_iota(jnp.int32, z.shape, 1)
    zz = jnp.where(lane < 64, jnp.tanh(z), jnp.where(lane < 128, z, _sigmoid(z)))
    lr = jnp.dot(zz, wlr_ref[...], precision=HI, preferred_element_type=F32)
    w = -_softplus(-(w0_ref[...] + lr[:, 0:W])) - 0.5
    a = _sigmoid(a0_ref[...] + lr[:, W:2 * W])
    kk = k * kk_ref[...]
    ssq = jnp.dot(kk * kk, ones_ref[...], precision=HI, preferred_element_type=F32)
    kk = kk / jnp.maximum(jnp.sqrt(ssq), 1e-12)
    r_ref[...] = r
    ld_ref[...] = -jnp.exp(w)
    k2_ref[...] = k * (1.0 + (a - 1.0) * ka_ref[...])
    v_ref[...] = v
    kn_ref[...] = kk
    a_ref[...] = a
    g_ref[...] = lr[:, 2 * W:3 * W]


def _rwkv_prep(proj, mu, wlr, w0, a0, k_k, k_a, head_ones, B, S):
    T = B * S
    tr = RWKV_PREP_ROWS
    nt = S // tr
    W = RWKV_WIDTH
    col = 2
    vec = lambda: pl.BlockSpec((1, W), lambda b, t: (0, 0))
    out = lambda: pl.BlockSpec((tr, W), lambda b, t: (b * nt + t, 0))
    return pl.pallas_call(
        _rwkv_prep_kernel,
        grid=(B, nt),
        in_specs=[pl.BlockSpec((tr, 4 * W), lambda b, t: (b * nt + t, col)),
                  pl.BlockSpec((8, 4 * W), lambda b, t: (jnp.maximum((b * nt + t) * (tr // 8) - 1, 0), col)),
                  pl.BlockSpec((1, 4 * W), lambda b, t: (0, 0)),
                  pl.BlockSpec((W, 3 * W), lambda b, t: (0, 0)),
                  vec(), vec(), vec(), vec(),
                  pl.BlockSpec((W, W), lambda b, t: (0, 0))],
        out_specs=[out() for _ in range(7)],
        out_shape=[jax.ShapeDtypeStruct((T, W), F32) for _ in range(7)],
        compiler_params=_params("arbitrary", "arbitrary"),
        name="rwkv_prep",
    )(proj, proj, mu, wlr, w0, a0, k_k, k_a, head_ones)


def _wkv_kernel(r_ref, ld_ref, k2_ref, v_ref, kn_ref, a_ref, y_ref, st_ref):
    c = pl.program_id(1)
    C = WKV_CHUNK
    W = RWKV_WIDTH
    H = RWKV_HEADS

    @pl.when(c == 0)
    def _():
        st_ref[...] = jnp.zeros_like(st_ref)

    def mm(x, y):
        return jnp.dot(x.astype(BF16), y.astype(BF16), preferred_element_type=F32)

    def mm_nt(x, y):
        return lax.dot_general(x.astype(BF16), y.astype(BF16), NT_DIMS, preferred_element_type=F32)

    def mm_tn(x, y):
        return lax.dot_general(x.astype(BF16), y.astype(BF16), TN_DIMS, preferred_element_type=F32)

    tr = lax.broadcasted_iota(jnp.int32, (C, C), 0)
    tc = lax.broadcasted_iota(jnp.int32, (C, C), 1)
    tri = jnp.where(tr >= tc, 1.0, 0.0)
    row = lax.broadcasted_iota(jnp.int32, (W, W), 0)
    colm = lax.broadcasted_iota(jnp.int32, (W, W), 1)
    same_head = (row // C) == (colm // HEAD_DIM)
    strict = lambda x: jnp.where(same_head, jnp.where(row > colm, x, 0.0), 0.0)
    incl = lambda x: jnp.where(same_head, jnp.where(row >= colm, x, 0.0), 0.0)
    same16 = (row // 16) == (colm // 16)
    eye = jnp.where(row == colm, 1.0, 0.0)

    def stack(x):
        return jnp.where(same_head, jnp.concatenate([x] * H, axis=0), 0.0)

    for gi in range(WKV_GROUP):
        r = r_ref[gi]
        ld = ld_ref[gi]
        k2 = k2_ref[gi]
        v = v_ref[gi]
        kn = kn_ref[gi]
        b_s = kn * a_ref[gi]
        cl = jnp.dot(tri, ld, precision=HI, preferred_element_type=F32)
        cl_end = cl[C - 1:C, :]
        e_neg = jnp.exp(-cl)
        e_end = jnp.exp(cl_end - cl)

        ar_s = jnp.concatenate([stack(-kn * jnp.exp(cl - ld)), stack(r * jnp.exp(cl))], axis=0)
        bk_s = jnp.concatenate([stack(b_s * e_neg), stack(k2 * e_neg)], axis=0)
        v_s = stack(v)
        end_s = jnp.concatenate([stack(b_s * e_end), stack(k2 * e_end)], axis=0)

        prod = mm_nt(ar_s, bk_s)
        n_ab = strict(prod[0:W, 0:W])
        a_ak = strict(prod[0:W, W:2 * W])
        a_rb = incl(prod[W:2 * W, 0:W])
        a_rk = incl(prod[W:2 * W, W:2 * W])

        nd = jnp.where(same16, n_ab, 0.0)
        lo = n_ab - nd
        n2 = mm(nd, nd)
        n4 = mm(n2, n2)
        n8 = mm(n4, n4)
        p1 = eye + nd
        p2 = p1 + mm(p1, n2)
        p3 = p2 + mm(p2, n4)
        dinv = p3 + mm(p3, n8)
        x1 = mm(dinv, lo)
        x2 = mm(x1, x1)
        y1 = eye + x1
        tinv = mm(y1 + mm(y1, x2), dinv)

        st = st_ref[gi]
        ar_st = mm_nt(ar_s, st)
        u_s = mm(tinv, ar_st[0:W] + mm(a_ak, v_s))
        y_s = ar_st[W:2 * W] + mm(a_rb, u_s) + mm(a_rk, v_s)
        y = y_s[0:C, :]
        for h in range(1, H):
            y = y + y_s[h * C:(h + 1) * C, :]
        y_ref[gi] = y
        st_ref[gi] = st * jnp.exp(cl_end) + mm_tn(jnp.concatenate([u_s, v_s], axis=0), end_s)


def _wkv(r, ld, k2, v, kn, a, B, S):
    C = WKV_CHUNK
    nc = S // C
    W = RWKV_WIDTH
    G = WKV_GROUP
    blk = lambda: pl.BlockSpec((G, C, W), lambda b, c: (b, c, 0))
    seq = lambda t: t.reshape(B, S, W)
    y = pl.pallas_call(
        _wkv_kernel,
        grid=(B // G, nc),
        in_specs=[blk() for _ in range(6)],
        out_specs=blk(),
        out_shape=jax.ShapeDtypeStruct((B, S, W), F32),
        scratch_shapes=[pltpu.VMEM((G, W, W), F32)],
        compiler_params=_params("arbitrary", "arbitrary"),
        name="wkv7",
    )(seq(r), seq(ld), seq(k2), seq(v), seq(kn), seq(a))
    return y.reshape(B * S, W)


def _wkv_post_kernel(y_ref, r_ref, k2_ref, v_ref, g_ref, rk_ref, lg_ref, lb_ref, ones_ref, o_ref):
    ones = ones_ref[...]

    def head_sum(x):
        hi = x.astype(BF16)
        lo = (x - hi.astype(F32)).astype(BF16)
        return (jnp.dot(hi, ones, preferred_element_type=F32) + jnp.dot(lo, ones, preferred_element_type=F32))

    y = y_ref[...]
    r = r_ref[...]
    k2 = k2_ref[...]
    inv_n = 1.0 / HEAD_DIM
    d = y - head_sum(y) * inv_n
    var = head_sum(d * d) * inv_n
    yn = d * lax.rsqrt(var + RWKV_LN_EPS) * lg_ref[...] + lb_ref[...]
    bonus = head_sum(r * k2 * rk_ref[...]) * v_ref[...]
    o_ref[...] = (yn + bonus) * g_ref[...]


def _wkv_post(y, r, k2, v, g, r_k, lnx_g, lnx_b, head_ones_bf16):
    T, W = y.shape
    tm = 512
    blk = lambda: pl.BlockSpec((tm, W), lambda m: (m, 0))
    vec = lambda: pl.BlockSpec((1, W), lambda m: (0, 0))
    return pl.pallas_call(
        _wkv_post_kernel,
        grid=(T // tm,),
        in_specs=[blk() for _ in range(5)] + [vec(), vec(), vec(), pl.BlockSpec((W, W), lambda m: (0, 0))],
        out_specs=blk(),
        out_shape=jax.ShapeDtypeStruct((T, W), F32),
        compiler_params=_params("arbitrary"),
        name="wkv_post",
    )(y, r, k2, v, g, r_k, lnx_g, lnx_b, head_ones_bf16)


def _outproj_ln_kernel(att_ref, lru_ref, rwk_ref, x_ref, wa_ref, wl_ref, wr_ref, g_ref, b_ref, o_ref, ob_ref):
    h = jnp.dot(att_ref[...].astype(BF16), wa_ref[...], preferred_element_type=F32)
    h = h + jnp.dot(lru_ref[...].astype(BF16), wl_ref[...], preferred_element_type=F32)
    h = h + jnp.dot(rwk_ref[...].astype(BF16), wr_ref[...], preferred_element_type=F32)
    y = _layer_norm(DEEPNORM_ALPHA * x_ref[...] + h, g_ref[...], b_ref[...])
    o_ref[...] = y
    ob_ref[...] = y.astype(BF16)


def _outproj_ln(att, lru, rwk, x, wa, wl, wr, g, b):
    T, D = x.shape
    tm = 512
    row = lambda w: pl.BlockSpec((tm, w), lambda m: (m, 0))
    full = lambda a: pl.BlockSpec(a.shape, lambda m: (0, 0))
    return pl.pallas_call(
        _outproj_ln_kernel,
        grid=(T // tm,),
        in_specs=[row(ATT_WIDTH), row(LRU_WIDTH), row(RWKV_WIDTH), row(D), full(wa), full(wl), full(wr), full(g), full(b)],
        out_specs=[row(D), row(D)],
        out_shape=[jax.ShapeDtypeStruct((T, D), F32), jax.ShapeDtypeStruct((T, D), BF16)],
        compiler_params=_params("arbitrary"),
        name="outproj_ln",
    )(att, lru, rwk, x, wa, wl, wr, g, b)


def _ffn_kernel(xb_ref, x_ref, wg_ref, wu_ref, wd_ref, g_ref, b_ref, o_ref, ob_ref, acc_ref, *, nf):
    f = pl.program_id(1)
    xb = xb_ref[...]
    hg = jnp.dot(xb, wg_ref[...], preferred_element_type=F32)
    hu = jnp.dot(xb, wu_ref[...], preferred_element_type=F32)
    hh = (hg * _sigmoid(hg) * hu).astype(BF16)
    contrib = jnp.dot(hh, wd_ref[...], preferred_element_type=F32)

    @pl.when(f == 0)
    def _():
        acc_ref[...] = contrib

    @pl.when(f > 0)
    def _():
        acc_ref[...] = acc_ref[...] + contrib

    @pl.when(f == nf - 1)
    def _():
        y = _layer_norm(DEEPNORM_ALPHA * x_ref[...] + acc_ref[...], g_ref[...], b_ref[...])
        o_ref[...] = y
        ob_ref[...] = y.astype(BF16)


def _ffn_ln(xb, x, wg, wu, wd, g, b):
    T, D = x.shape
    F = wg.shape[1]
    tm = 512
    nf = 2
    tf = F // nf
    return pl.pallas_call(
        functools.partial(_ffn_kernel, nf=nf),
        grid=(T // tm, nf),
        in_specs=[pl.BlockSpec((tm, D), lambda m, f: (m, 0)),
                  pl.BlockSpec((tm, D), lambda m, f: (m, 0)),
                  pl.BlockSpec((D, tf), lambda m, f: (0, f)),
                  pl.BlockSpec((D, tf), lambda m, f: (0, f)),
                  pl.BlockSpec((tf, D), lambda m, f: (f, 0)),
                  pl.BlockSpec((1, D), lambda m, f: (0, 0)),
                  pl.BlockSpec((1, D), lambda m, f: (0, 0))],
        out_specs=[pl.BlockSpec((tm, D), lambda m, f: (m, 0)), pl.BlockSpec((tm, D), lambda m, f: (m, 0))],
        out_shape=[jax.ShapeDtypeStruct((T, D), F32), jax.ShapeDtypeStruct((T, D), BF16)],
        scratch_shapes=[pltpu.VMEM((tm, D), F32)],
        compiler_params=_params("arbitrary", "arbitrary"),
        name="ffn_ln",
    )(xb, x, wg, wu, wd, g, b)


def _router_kernel(x_ref, wrt_ref, tri_ref, g_ref, pos_ref, cnt_ref):
    tt = x_ref.shape[0]
    logits = lax.dot_general(wrt_ref[...], x_ref[...], NT_DIMS, precision=HI, preferred_element_type=F32)
    row = lax.broadcasted_iota(jnp.int32, logits.shape, 0).astype(F32)
    m1 = jnp.max(logits, axis=0, keepdims=True)
    i1 = jnp.min(jnp.where(logits == m1, row, float(N_EXPERTS)), axis=0, keepdims=True)
    rest = jnp.where(row == i1, -jnp.inf, logits)
    m2 = jnp.max(rest, axis=0, keepdims=True)
    i2 = jnp.min(jnp.where(rest == m2, row, float(N_EXPERTS)), axis=0, keepdims=True)
    e = jnp.exp(m2 - m1)
    g1 = 1.0 / (1.0 + e)
    g2 = e / (1.0 + e)
    g_ref[0] = jnp.where(row == i1, g1, jnp.where(row == i2, g2, 0.0))
    ind = jnp.where(row == i1, 1.0, jnp.where(row == i2, 1.0, 0.0))
    csum = jnp.dot(ind.astype(BF16), tri_ref[...], preferred_element_type=F32)
    pos_ref[0] = jnp.where(ind > 0.5, csum - 1.0, -1.0)
    cnt_ref[0] = jnp.broadcast_to(csum[:, tt - 1:tt], (N_EXPERTS, LANES))


def _router(x, wrt, tri):
    T, D = x.shape
    tt = MOE_TILE
    nt = T // tt
    tile = lambda: pl.BlockSpec((1, N_EXPERTS, tt), lambda m: (m, 0, 0))
    return pl.pallas_call(
        _router_kernel,
        grid=(nt,),
        in_specs=[pl.BlockSpec((tt, D), lambda m: (m, 0)), pl.BlockSpec((N_EXPERTS, D), lambda m: (0, 0)),
                  pl.BlockSpec((tt, tt), lambda m: (0, 0))],
        out_specs=[tile(), tile(), pl.BlockSpec((1, N_EXPERTS, LANES), lambda m: (m, 0, 0))],
        out_shape=[jax.ShapeDtypeStruct((nt, N_EXPERTS, tt), F32), jax.ShapeDtypeStruct((nt, N_EXPERTS, tt), F32),
                   jax.ShapeDtypeStruct((nt, N_EXPERTS, LANES), F32)],
        compiler_params=_params("arbitrary"),
        name="moe_router",
    )(x, wrt, tri)


def _moe_kernel(cnt_ref, xb_ref, gate_ref, pos_ref, wg_ref, wu_ref, wd_ref, o_ref, xg_ref, yacc_ref, gcol_ref, *, nf):
    t = pl.program_id(0)
    e = pl.program_id(1)
    f = pl.program_id(2)
    tt = xb_ref.shape[0]
    R = MOE_ROWS
    n_chunks = (cnt_ref[t * N_EXPERTS + e] + (R - 1)) // R
    pos_row = pos_ref[0, pl.ds(e, 1), :]
    slot = lax.broadcasted_iota(jnp.int32, (R, tt), 0).astype(F32)

    def selection(r):
        return jnp.where(pos_row == slot + (r * R).astype(F32), 1.0, 0.0)

    @pl.when(jnp.logical_and(e == 0, f == 0))
    def _():
        o_ref[...] = jnp.zeros_like(o_ref)

    @pl.when(f == 0)
    def _():
        gate_row = gate_ref[0, pl.ds(e, 1), :]

        def gather(r, carry):
            sel = selection(r)
            rows = pl.ds(pl.multiple_of(r * R, R), R)
            xg_ref[rows, :] = jnp.dot(sel.astype(BF16), xb_ref[...], preferred_element_type=F32).astype(BF16)
            gcol_ref[rows, :] = jnp.broadcast_to(jnp.sum(sel * gate_row, axis=1, keepdims=True), (R, LANES))
            return carry

        lax.fori_loop(0, n_chunks, gather, 0)

    def expert(r, carry):
        rows = pl.ds(pl.multiple_of(r * R, R), R)
        xc = xg_ref[rows, :]
        hg = jnp.dot(xc, wg_ref[0], preferred_element_type=F32)
        hu = jnp.dot(xc, wu_ref[0], preferred_element_type=F32)
        contrib = jnp.dot((hg * _sigmoid(hg) * hu).astype(BF16), wd_ref[0], preferred_element_type=F32)

        @pl.when(f == 0)
        def _():
            yacc_ref[rows, :] = contrib

        @pl.when(f > 0)
        def _():
            yacc_ref[rows, :] = yacc_ref[rows, :] + contrib

        @pl.when(f == nf - 1)
        def _():
            ys = (yacc_ref[rows, :] * pltpu.repeat(gcol_ref[rows, :], o_ref.shape[1] // LANES, axis=1)).astype(BF16)
            o_ref[...] = o_ref[...] + lax.dot_general(selection(r).astype(BF16), ys, TN_DIMS,
                                                      preferred_element_type=F32)
        return carry

    lax.fori_loop(0, n_chunks, expert, 0)


def _moe(xb, gates_t, pos_t, counts, wg, wu, wd):
    T, D = xb.shape
    F = wg.shape[2]
    tt = MOE_TILE
    nf = 4
    tf = F // nf
    grid_spec = pltpu.PrefetchScalarGridSpec(
        num_scalar_prefetch=1,
        grid=(T // tt, N_EXPERTS, nf),
        in_specs=[pl.BlockSpec((tt, D), lambda m, e, f, c: (m, 0)),
                  pl.BlockSpec((1, N_EXPERTS, tt), lambda m, e, f, c: (m, 0, 0)),
                  pl.BlockSpec((1, N_EXPERTS, tt), lambda m, e, f, c: (m, 0, 0)),
                  pl.BlockSpec((1, D, tf), lambda m, e, f, c: (e, 0, f)),
                  pl.BlockSpec((1, D, tf), lambda m, e, f, c: (e, 0, f)),
                  pl.BlockSpec((1, tf, D), lambda m, e, f, c: (e, f, 0))],
        out_specs=pl.BlockSpec((tt, D), lambda m, e, f, c: (m, 0)),
        scratch_shapes=[pltpu.VMEM((tt, D), BF16), pltpu.VMEM((tt, D), F32), pltpu.VMEM((tt, LANES), F32)],
    )
    return pl.pallas_call(
        functools.partial(_moe_kernel, nf=nf),
        grid_spec=grid_spec,
        out_shape=jax.ShapeDtypeStruct((T, D), F32),
        compiler_params=_params("arbitrary", "arbitrary", "arbitrary"),
        name="moe_experts",
    )(counts, xb, gates_t, pos_t, wg, wu, wd)


def _resid_ln_kernel(x_ref, f_ref, g_ref, b_ref, o_ref):
    o_ref[...] = _layer_norm(DEEPNORM_ALPHA * x_ref[...] + f_ref[...], g_ref[...], b_ref[...])


def _resid_ln(x, f, g, b):
    T, D = x.shape
    tm = 512
    row = lambda: pl.BlockSpec((tm, D), lambda m: (m, 0))
    vec = lambda: pl.BlockSpec((1, D), lambda m: (0, 0))
    return pl.pallas_call(
        _resid_ln_kernel,
        grid=(T // tm,),
        in_specs=[row(), row(), vec(), vec()],
        out_specs=row(),
        out_shape=jax.ShapeDtypeStruct((T, D), F32),
        compiler_params=_params("arbitrary"),
        name="resid_ln",
    )(x, f, g, b)


def _block_diag(w):
    n, d, _ = w.shape
    out = jnp.zeros((n * d, n * d), w.dtype)
    for i in range(n):
        out = out.at[i * d:(i + 1) * d, i * d:(i + 1) * d].set(w[i])
    return out


def _rope_tables(positions):
    B, S = positions.shape
    half = ROPE_DIMS // 2
    inv_freq = ROPE_THETA ** (-jnp.arange(0, ROPE_DIMS, 2, dtype=F32) / ROPE_DIMS)
    ang = positions.astype(F32)[..., None] * inv_freq
    cos = jnp.cos(ang).reshape(B * S, half)
    sin = jnp.sin(ang).reshape(B * S, half)
    zeros = jnp.zeros((B * S, HEAD_DIM - ROPE_DIMS), F32)
    zh = jnp.zeros((B * S, half), F32)
    c = jnp.concatenate([cos, cos, zeros + 1.0], axis=1)
    s1 = jnp.concatenate([-sin, zh, zeros], axis=1)
    s2 = jnp.concatenate([zh, sin, zeros], axis=1)
    tile = lambda t: jnp.concatenate([t] * (LANES // HEAD_DIM), axis=1)
    return tile(c), tile(s1), tile(s2)


def _mixer_layer(x, xb, rope, B, S, w_in, conv_w, conv_b, ga_w, ga_b, gx_w, gx_b, lam,
                 mu, w0, w_up, a0, a_up, g_up, k_k, k_a, r_k, lnx_g, lnx_b, w_out, ln_g, ln_b):
    W = RWKV_WIDTH
    row = lambda t: t.reshape(1, -1)
    proj = _in_proj(xb, w_in.astype(BF16))
    q, k, vt, pen = _attn_prep(proj, *rope, B, S)
    att = _attention(q, k, vt, pen, B, S)
    lru = _rglru(proj, conv_w, row(conv_b), _block_diag(ga_w), row(ga_b), _block_diag(gx_w), row(gx_b), row(lam), B, S)
    head_ones = _block_diag(jnp.ones((RWKV_HEADS, HEAD_DIM, HEAD_DIM), F32))
    wlr = jnp.zeros((W, 3 * W), F32)
    wlr = wlr.at[0:64, 0:W].set(w_up).at[64:128, W:2 * W].set(a_up).at[128:256, 2 * W:3 * W].set(g_up)
    r, ld, k2, v, kn, a, g = _rwkv_prep(proj, row(mu), wlr, row(w0), row(a0), row(k_k), row(k_a), head_ones, B, S)
    y = _wkv(r, ld, k2, v, kn, a, B, S)
    rwk = _wkv_post(y, r, k2, v, g, row(r_k), row(lnx_g), row(lnx_b), head_ones.astype(BF16))
    wo = w_out.astype(BF16)
    return _outproj_ln(att, lru, rwk, x, wo[0:ATT_WIDTH], wo[ATT_WIDTH:ATT_WIDTH + LRU_WIDTH],
                       wo[ATT_WIDTH + LRU_WIDTH:], row(ln_g), row(ln_b))


def kernel(x, positions, w_in, lru_conv_w, lru_conv_b, lru_ga_w, lru_ga_b, lru_gx_w, lru_gx_b, lru_lambda, rwkv_mu, rwkv_w0, rwkv_w_up, rwkv_a0, rwkv_a_up, rwkv_g_up, rwkv_k_k, rwkv_k_a, rwkv_r_k, rwkv_lnx_g, rwkv_lnx_b, w_out, ln1_g, ln1_b, ffn_w_gate, ffn_w_up, ffn_w_down, moe_router, moe_w_gate, moe_w_up, moe_w_down, ln2_g, ln2_b):
    B, S, D = x.shape
    T = B * S
    rope = _rope_tables(positions)
    xf = x.reshape(T, D)
    xb = xf.astype(BF16)
    for l in range(DEPTH):
        xf, xb = _mixer_layer(xf, xb, rope, B, S, w_in[l], lru_conv_w[l], lru_conv_b[l], lru_ga_w[l], lru_ga_b[l],
                              lru_gx_w[l], lru_gx_b[l], lru_lambda[l], rwkv_mu[l], rwkv_w0[l], rwkv_w_up[l],
                              rwkv_a0[l], rwkv_a_up[l], rwkv_g_up[l], rwkv_k_k[l], rwkv_k_a[l], rwkv_r_k[l],
                              rwkv_lnx_g[l], rwkv_lnx_b[l], w_out[l], ln1_g[l], ln1_b[l])
        g2 = ln2_g[l].reshape(1, D)
        b2 = ln2_b[l].reshape(1, D)
        if l % 2 == 0:
            i = l // 2
            xf, xb = _ffn_ln(xb, xf, ffn_w_gate[i].astype(BF16), ffn_w_up[i].astype(BF16),
                             ffn_w_down[i].astype(BF16), g2, b2)
        else:
            i = l // 2
            idx = jnp.arange(MOE_TILE)
            tri = (idx[:, None] <= idx[None, :]).astype(BF16)
            gates_t, pos_t, cnt = _router(xf, moe_router[i].T, tri)
            counts = cnt[:, :, 0].astype(jnp.int32).reshape(-1)
            f = _moe(xb, gates_t, pos_t, counts, moe_w_gate[i].astype(BF16), moe_w_up[i].astype(BF16),
                     moe_w_down[i].astype(BF16))
            xf = _resid_ln(xf, f, g2, b2)
            xb = xf.astype(BF16)
    return xf.reshape(B, S, D)
```

```python
import functools

import jax
import jax.numpy as jnp
from jax import lax
from jax.experimental import pallas as pl
from jax.experimental.pallas import tpu as pltpu

F32 = jnp.float32
BF16 = jnp.bfloat16
HI = lax.Precision.HIGHEST
NT_DIMS = (((1,), (1,)), ((), ()))
TN_DIMS = (((0,), (0,)), ((), ()))

HEAD_DIM = 64
ATT_HEADS = 8
ATT_WIDTH = ATT_HEADS * HEAD_DIM
LRU_WIDTH = 256
RWKV_HEADS = 4
RWKV_WIDTH = 256
ROPE_DIMS = 16
ROPE_THETA = 500000.0
MOBA_BLOCK = 256
MOBA_TOPK = 3
CONV_WIDTH = 4
LRU_C = 8.0
RWKV_LN_EPS = 64e-5
N_EXPERTS = 8
LN_EPS = 1e-5
DEPTH = 2
DEEPNORM_ALPHA = (2 * DEPTH) ** 0.25
NEG_INF = -1e30
LOG2_E = 1.4426950408889634

LANES = 128
VMEM_LIMIT = 56 * 1024 * 1024

WKV_CHUNK = 64
WKV_GROUP = 8
LRU_CHUNK = 256
RWKV_PREP_ROWS = 512
MOE_TILE = 1024
MOE_ROWS = 256


def _params(*sem):
    return pltpu.CompilerParams(dimension_semantics=sem, vmem_limit_bytes=VMEM_LIMIT)


def _sigmoid(x):
    return 1.0 / (1.0 + jnp.exp(-x))


def _softplus(x):
    return jnp.maximum(x, 0.0) + jnp.log1p(jnp.exp(-jnp.abs(x)))


def _expm1(z):
    u = jnp.exp(z)
    um1 = u - 1.0
    return jnp.where(u == 1.0, z, jnp.where(um1 == -1.0, -1.0, um1 * z / jnp.log(u)))


def _layer_norm(y, g, b):
    m = jnp.mean(y, axis=-1, keepdims=True)
    d = y - m
    var = jnp.mean(d * d, axis=-1, keepdims=True)
    return d * lax.rsqrt(var + LN_EPS) * g + b


def _matmul_kernel(x_ref, w_ref, o_ref):
    o_ref[...] = jnp.dot(x_ref[...], w_ref[...], preferred_element_type=F32)


def _in_proj(xb, w):
    T, D = xb.shape
    N = w.shape[1]
    tm, tn = 512, 1024
    return pl.pallas_call(
        _matmul_kernel,
        grid=(N // tn, T // tm),
        in_specs=[pl.BlockSpec((tm, D), lambda n, m: (m, 0)),
                  pl.BlockSpec((D, tn), lambda n, m: (0, n))],
        out_specs=pl.BlockSpec((tm, tn), lambda n, m: (m, n)),
        out_shape=jax.ShapeDtypeStruct((T, N), F32),
        compiler_params=_params("arbitrary", "arbitrary"),
        name="in_proj",
    )(xb, w)


def _attn_prep_kernel(proj_ref, c_ref, s1_ref, s2_ref, q_ref, k_ref, vt_ref, pen_ref, km_ref, *, nb):
    i = pl.program_id(1)

    @pl.when(i == 0)
    def _():
        km_ref[...] = jnp.zeros_like(km_ref)

    c = c_ref[...]
    s1 = s1_ref[...]
    s2 = s2_ref[...]

    def rope(xt):
        return xt * c + pltpu.roll(xt, LANES - ROPE_DIMS // 2, 1) * s1 + pltpu.roll(xt, ROPE_DIMS // 2, 1) * s2

    q_tiles = []
    for ct in range(ATT_WIDTH // LANES):
        lo, hi = ct * LANES, (ct + 1) * LANES
        qr = rope(proj_ref[:, lo:hi])
        q_tiles.append(qr)
        q_ref[:, lo:hi] = (qr * (HEAD_DIM ** -0.5 * LOG2_E)).astype(BF16)
        kr = rope(proj_ref[:, ATT_WIDTH + lo:ATT_WIDTH + hi])
        k_ref[0, 0, :, lo:hi] = kr.astype(BF16)
        km_row = lax.broadcasted_iota(jnp.int32, (nb, LANES), 0)
        km_ref[:, lo:hi] = jnp.where(km_row == i, jnp.mean(kr, axis=0, keepdims=True), km_ref[:, lo:hi])
    vt_ref[0, 0] = proj_ref[:, 2 * ATT_WIDTH:3 * ATT_WIDTH].T.astype(BF16)

    q_rot = jnp.concatenate(q_tiles, axis=1)
    km = km_ref[...]
    km_rows = jnp.concatenate([km] * ATT_HEADS, axis=0)
    row_head = lax.broadcasted_iota(jnp.int32, km_rows.shape, 0) // nb
    lane_head = lax.broadcasted_iota(jnp.int32, km_rows.shape, 1) // HEAD_DIM
    g_all = lax.dot_general(jnp.where(row_head == lane_head, km_rows, 0.0), q_rot, NT_DIMS,
                            precision=HI, preferred_element_type=F32)
    n_iota = lax.broadcasted_iota(jnp.int32, (nb, MOBA_BLOCK), 0)
    past = n_iota < i
    for h in range(ATT_HEADS):
        g = g_all[h * nb:(h + 1) * nb, :]
        rank = jnp.zeros((nb, MOBA_BLOCK), F32)
        for m in range(nb):
            gm = g[m:m + 1, :]
            beats = jnp.where(gm > g, 1.0, jnp.where(gm == g, jnp.where(n_iota > m, 1.0, 0.0), 0.0))
            rank = rank + jnp.where(m < i, beats, 0.0)
        pen = jnp.where(past, jnp.where(rank < float(MOBA_TOPK), 0.0, NEG_INF), NEG_INF)
        half = (h % 2) * MOBA_BLOCK
        pen_ref[0, h // 2, :, half:half + MOBA_BLOCK] = pen


def _attn_prep(proj, rope_c, rope_s1, rope_s2, B, S):
    T = B * S
    nb = S // MOBA_BLOCK
    blk = MOBA_BLOCK
    return pl.pallas_call(
        functools.partial(_attn_prep_kernel, nb=nb),
        grid=(B, nb),
        in_specs=[pl.BlockSpec((blk, 3 * ATT_WIDTH), lambda b, i: (b * nb + i, 0)),
                  pl.BlockSpec((blk, LANES), lambda b, i: (b * nb + i, 0)),
                  pl.BlockSpec((blk, LANES), lambda b, i: (b * nb + i, 0)),
                  pl.BlockSpec((blk, LANES), lambda b, i: (b * nb + i, 0))],
        out_specs=[pl.BlockSpec((blk, ATT_WIDTH), lambda b, i: (b * nb + i, 0)),
                   pl.BlockSpec((1, 1, blk, ATT_WIDTH), lambda b, i: (b, i, 0, 0)),
                   pl.BlockSpec((1, 1, ATT_WIDTH, blk), lambda b, i: (b, i, 0, 0)),
                   pl.BlockSpec((1, ATT_HEADS // 2, nb, 2 * blk), lambda b, i: (b, 0, 0, i))],
        out_shape=[jax.ShapeDtypeStruct((T, ATT_WIDTH), BF16),
                   jax.ShapeDtypeStruct((B, nb, blk, ATT_WIDTH), BF16),
                   jax.ShapeDtypeStruct((B, nb, ATT_WIDTH, blk), BF16),
                   jax.ShapeDtypeStruct((B, ATT_HEADS // 2, nb, 2 * S), F32)],
        scratch_shapes=[pltpu.VMEM((nb, ATT_WIDTH), F32)],
        compiler_params=_params("arbitrary", "arbitrary"),
        name="attn_prep",
    )(proj, rope_c, rope_s1, rope_s2)


def _attn_kernel(q_ref, k_ref, vt_ref, pen_ref, o_ref):
    i = pl.program_id(2)
    blk = MOBA_BLOCK
    q = q_ref[...]
    lane_head = lax.broadcasted_iota(jnp.int32, q.shape, 1) // HEAD_DIM
    zero = jnp.zeros_like(q)
    qs = jnp.concatenate([jnp.where(lane_head == 0, q, zero), jnp.where(lane_head == 1, q, zero)], axis=0)
    kidx = lax.broadcasted_iota(jnp.int32, (blk, 2 * blk), 0)
    qidx = lax.broadcasted_iota(jnp.int32, (blk, 2 * blk), 1) % blk
    s = lax.dot_general(k_ref[0, i], qs, NT_DIMS, preferred_element_type=F32)
    s = jnp.where(kidx <= qidx, s, NEG_INF)
    m0 = jnp.max(s, axis=0, keepdims=True)
    p = jnp.exp2(s - m0)
    l0 = jnp.sum(p, axis=0, keepdims=True)
    acc0 = jnp.dot(vt_ref[0, i], p.astype(BF16), preferred_element_type=F32)

    def body(jj, carry):
        m, l, acc = carry
        j0 = 2 * jj
        j1 = j0 + 1
        s0 = lax.dot_general(k_ref[0, j0], qs, NT_DIMS, preferred_element_type=F32) + pen_ref[0, 0, pl.ds(j0, 1), :]
        s1 = lax.dot_general(k_ref[0, j1], qs, NT_DIMS, preferred_element_type=F32) + pen_ref[0, 0, pl.ds(j1, 1), :]
        m_new = jnp.maximum(m, jnp.maximum(jnp.max(s0, axis=0, keepdims=True), jnp.max(s1, axis=0, keepdims=True)))
        alpha = jnp.exp2(m - m_new)
        p0 = jnp.exp2(s0 - m_new)
        p1 = jnp.exp2(s1 - m_new)
        l = alpha * l + jnp.sum(p0, axis=0, keepdims=True) + jnp.sum(p1, axis=0, keepdims=True)
        acc = (acc * alpha + jnp.dot(vt_ref[0, j0], p0.astype(BF16), preferred_element_type=F32)
               + jnp.dot(vt_ref[0, j1], p1.astype(BF16), preferred_element_type=F32))
        return m_new, l, acc

    _, l, acc = lax.fori_loop(0, (i + 1) // 2, body, (m0, l0, acc0))
    out = acc / l
    row = lax.broadcasted_iota(jnp.int32, (LANES, blk), 0)
    o_ref[...] = jnp.where(row < HEAD_DIM, out[:, 0:blk], out[:, blk:2 * blk]).T


def _attention(q, k, vt, pen, B, S):
    T = B * S
    nb = S // MOBA_BLOCK
    blk = MOBA_BLOCK
    n_pairs = ATT_WIDTH // LANES
    return pl.pallas_call(
        _attn_kernel,
        grid=(B, n_pairs, nb),
        in_specs=[pl.BlockSpec((blk, LANES), lambda b, hp, i: (b * nb + i, hp)),
                  pl.BlockSpec((1, nb, blk, LANES), lambda b, hp, i: (b, 0, 0, hp)),
                  pl.BlockSpec((1, nb, LANES, blk), lambda b, hp, i: (b, 0, hp, 0)),
                  pl.BlockSpec((1, 1, nb, 2 * blk), lambda b, hp, i: (b, hp, 0, i))],
        out_specs=pl.BlockSpec((blk, LANES), lambda b, hp, i: (b * nb + i, hp)),
        out_shape=jax.ShapeDtypeStruct((T, ATT_WIDTH), F32),
        compiler_params=_params("arbitrary", "arbitrary", "arbitrary"),
        name="moba_attn",
    )(q, k, vt, pen)


def _lru_kernel(p_ref, cw_ref, cb_ref, ga_ref, gab_ref, gx_ref, gxb_ref, lam_ref, o_ref, xbuf, hc):
    t = pl.program_id(1)
    tc = LRU_CHUNK
    pad = 8

    @pl.when(t == 0)
    def _():
        xbuf[0:pad, :] = jnp.zeros((pad, LRU_WIDTH), F32)
        hc[...] = jnp.zeros_like(hc)

    @pl.when(t > 0)
    def _():
        xbuf[0:pad, :] = xbuf[tc:tc + pad, :]

    x = p_ref[:, 0:LRU_WIDTH]
    gate = p_ref[:, LRU_WIDTH:2 * LRU_WIDTH]
    xbuf[pad:pad + tc, :] = x
    xc = cb_ref[...] + x * cw_ref[CONV_WIDTH - 1:CONV_WIDTH, :]
    for j in range(CONV_WIDTH - 1):
        back = CONV_WIDTH - 1 - j
        xc = xc + xbuf[pad - back:pad - back + tc, :] * cw_ref[j:j + 1, :]

    r = _sigmoid(jnp.dot(xc, ga_ref[...], precision=HI, preferred_element_type=F32) + gab_ref[...])
    ig = _sigmoid(jnp.dot(xc, gx_ref[...], precision=HI, preferred_element_type=F32) + gxb_ref[...])
    log_a = -LRU_C * r * _softplus(-lam_ref[...])
    a = jnp.exp(log_a)
    u = jnp.sqrt(-_expm1(2.0 * log_a)) * (ig * xc)

    rows = lax.broadcasted_iota(jnp.int32, (tc, LRU_WIDTH), 0)
    d = 1
    while d < tc:
        keep = rows >= d
        a_prev = jnp.where(keep, pltpu.roll(a, d, 0), 1.0)
        u_prev = jnp.where(keep, pltpu.roll(u, d, 0), 0.0)
        u = a * u_prev + u
        a = a * a_prev
        d *= 2
    h = u + a * hc[...]
    hc[...] = h[tc - 1:tc, :]
    gl = 0.5 * gate * (1.0 + jnp.tanh(0.7978845608028654 * (gate + 0.044715 * gate * gate * gate)))
    o_ref[...] = h * gl


def _rglru(proj, cw, cb, ga, gab, gx, gxb, lam, B, S):
    T = B * S
    tc = LRU_CHUNK
    nt = S // tc
    col = (3 * ATT_WIDTH) // (2 * LRU_WIDTH)
    vec = lambda: pl.BlockSpec((1, LRU_WIDTH), lambda b, t: (0, 0))
    mat = lambda: pl.BlockSpec((LRU_WIDTH, LRU_WIDTH), lambda b, t: (0, 0))
    return pl.pallas_call(
        _lru_kernel,
        grid=(B, nt),
        in_specs=[pl.BlockSpec((tc, 2 * LRU_WIDTH), lambda b, t: (b * nt + t, col)),
                  pl.BlockSpec((CONV_WIDTH, LRU_WIDTH), lambda b, t: (0, 0)),
                  vec(), mat(), vec(), mat(), vec(), vec()],
        out_specs=pl.BlockSpec((tc, LRU_WIDTH), lambda b, t: (b * nt + t, 0)),
        out_shape=jax.ShapeDtypeStruct((T, LRU_WIDTH), F32),
        scratch_shapes=[pltpu.VMEM((tc + 8, LRU_WIDTH), F32), pltpu.VMEM((1, LRU_WIDTH), F32)],
        compiler_params=_params("arbitrary", "arbitrary"),
        name="rglru",
    )(proj, cw, cb, ga, gab, gx, gxb, lam)


def _rwkv_prep_kernel(p_ref, pp_ref, mu_ref, wlr_ref, w0_ref, a0_ref, kk_ref, ka_ref, ones_ref,
                      r_ref, ld_ref, k2_ref, v_ref, kn_ref, a_ref, g_ref):
    i = pl.program_id(1)
    W = RWKV_WIDTH
    p = p_ref[...]
    rows = lax.broadcasted_iota(jnp.int32, p.shape, 0)
    prev_last = jnp.where(i == 0, 0.0, pp_ref[7:8, :])
    p_prev = jnp.where(rows == 0, prev_last, pltpu.roll(p, 1, 0))
    pf = p + (p_prev - p) * mu_ref[...]
    r = pf[:, 0:W]
    k = pf[:, W:2 * W]
    v = pf[:, 2 * W:3 * W]
    z = pf[:, 3 * W:4 * W]
    lane = lax.broadcasted_iota(jnp.int32, z.shape, 1)
    zz = jnp.where(lane < 64, jnp.tanh(z), jnp.where(lane < 128, z, _sigmoid(z)))
    lr = jnp.dot(zz, wlr_ref[...], precision=HI, preferred_element_type=F32)
    w = -_softplus(-(w0_ref[...] + lr[:, 0:W])) - 0.5
    a = _sigmoid(a0_ref[...] + lr[:, W:2 * W])
    kk = k * kk_ref[...]
    ssq = jnp.dot(kk * kk, ones_ref[...], precision=HI, preferred_element_type=F32)
    kk = kk / jnp.maximum(jnp.sqrt(ssq), 1e-12)
    r_ref[...] = r
    ld_ref[...] = -jnp.exp(w)
    k2_ref[...] = k * (1.0 + (a - 1.0) * ka_ref[...])
    v_ref[...] = v
    kn_ref[...] = kk
    a_ref[...] = a
    g_ref[...] = lr[:, 2 * W:3 * W]


def _rwkv_prep(proj, mu, wlr, w0, a0, k_k, k_a, head_ones, B, S):
    T = B * S
    tr = RWKV_PREP_ROWS
    nt = S // tr
    W = RWKV_WIDTH
    col = 2
    vec = lambda: pl.BlockSpec((1, W), lambda b, t: (0, 0))
    out = lambda: pl.BlockSpec((tr, W), lambda b, t: (b * nt + t, 0))
    return pl.pallas_call(
        _rwkv_prep_kernel,
        grid=(B, nt),
        in_specs=[pl.BlockSpec((tr, 4 * W), lambda b, t: (b * nt + t, col)),
                  pl.BlockSpec((8, 4 * W), lambda b, t: (jnp.maximum((b * nt + t) * (tr // 8) - 1, 0), col)),
                  pl.BlockSpec((1, 4 * W), lambda b, t: (0, 0)),
                  pl.BlockSpec((W, 3 * W), lambda b, t: (0, 0)),
                  vec(), vec(), vec(), vec(),
                  pl.BlockSpec((W, W), lambda b, t: (0, 0))],
        out_specs=[out() for _ in range(7)],
        out_shape=[jax.ShapeDtypeStruct((T, W), F32) for _ in range(7)],
        compiler_params=_params("arbitrary", "arbitrary"),
        name="rwkv_prep",
    )(proj, proj, mu, wlr, w0, a0, k_k, k_a, head_ones)


def _wkv_kernel(r_ref, ld_ref, k2_ref, v_ref, kn_ref, a_ref, y_ref, st_ref):
    c = pl.program_id(1)
    C = WKV_CHUNK
    W = RWKV_WIDTH
    H = RWKV_HEADS

    @pl.when(c == 0)
    def _():
        st_ref[...] = jnp.zeros_like(st_ref)

    def mm(x, y):
        return jnp.dot(x.astype(BF16), y.astype(BF16), preferred_element_type=F32)

    def mm_nt(x, y):
        return lax.dot_general(x.astype(BF16), y.astype(BF16), NT_DIMS, preferred_element_type=F32)

    def mm_tn(x, y):
        return lax.dot_general(x.astype(BF16), y.astype(BF16), TN_DIMS, preferred_element_type=F32)

    tr = lax.broadcasted_iota(jnp.int32, (C, C), 0)
    tc = lax.broadcasted_iota(jnp.int32, (C, C), 1)
    tri = jnp.where(tr >= tc, 1.0, 0.0)
    row = lax.broadcasted_iota(jnp.int32, (W, W), 0)
    colm = lax.broadcasted_iota(jnp.int32, (W, W), 1)
    same_head = (row // C) == (colm // HEAD_DIM)
    strict = lambda x: jnp.where(same_head, jnp.where(row > colm, x, 0.0), 0.0)
    incl = lambda x: jnp.where(same_head, jnp.where(row >= colm, x, 0.0), 0.0)
    same16 = (row // 16) == (colm // 16)
    eye = jnp.where(row == colm, 1.0, 0.0)

    def stack(x):
        return jnp.where(same_head, jnp.concatenate([x] * H, axis=0), 0.0)

    G = range(WKV_GROUP)
    each = lambda fn, *xs: [fn(*(x[g] for x in xs)) for g in G]

    r = [r_ref[g] for g in G]
    ld = [ld_ref[g] for g in G]
    k2 = [k2_ref[g] for g in G]
    v_s = [stack(v_ref[g]) for g in G]
    kn = [kn_ref[g] for g in G]
    b_s = [kn[g] * a_ref[g] for g in G]
    cl = each(lambda x: jnp.dot(tri, x, precision=HI, preferred_element_type=F32), ld)
    cl_end = [x[C - 1:C, :] for x in cl]
    e_neg = [jnp.exp(-x) for x in cl]
    e_end = each(lambda ce, x: jnp.exp(ce - x), cl_end, cl)
    ar_s = [jnp.concatenate([stack(-kn[g] * jnp.exp(cl[g] - ld[g])), stack(r[g] * jnp.exp(cl[g]))], axis=0) for g in G]
    bk_s = [jnp.concatenate([stack(b_s[g] * e_neg[g]), stack(k2[g] * e_neg[g])], axis=0) for g in G]
    end_s = [jnp.concatenate([stack(b_s[g] * e_end[g]), stack(k2[g] * e_end[g])], axis=0) for g in G]

    prod = each(mm_nt, ar_s, bk_s)
    n_ab = [strict(p[0:W, 0:W]) for p in prod]
    a_ak = [strict(p[0:W, W:2 * W]) for p in prod]
    a_rb = [incl(p[W:2 * W, 0:W]) for p in prod]
    a_rk = [incl(p[W:2 * W, W:2 * W]) for p in prod]

    nd = [jnp.where(same16, n, 0.0) for n in n_ab]
    lo = each(lambda n, d: n - d, n_ab, nd)
    n2 = each(mm, nd, nd)
    p1 = [eye + d for d in nd]
    n4 = each(mm, n2, n2)
    p2 = each(lambda p, n: p + mm(p, n), p1, n2)
    n8 = each(mm, n4, n4)
    p3 = each(lambda p, n: p + mm(p, n), p2, n4)
    dinv = each(lambda p, n: p + mm(p, n), p3, n8)
    x1 = each(mm, dinv, lo)
    x2 = each(mm, x1, x1)
    y1 = [eye + x for x in x1]
    m_inv = each(lambda y, x: y + mm(y, x), y1, x2)
    tinv = each(mm, m_inv, dinv)

    st = [st_ref[g] for g in G]
    ar_st = each(mm_nt, ar_s, st)
    akv = each(mm, a_ak, v_s)
    u_s = each(lambda t, a, b: mm(t, a[0:W] + b), tinv, ar_st, akv)
    rkv = each(mm, a_rk, v_s)
    y_s = each(lambda a, m, u, b: a[W:2 * W] + mm(m, u) + b, ar_st, a_rb, u_s, rkv)
    upd = each(lambda u, v, e: mm_tn(jnp.concatenate([u, v], axis=0), e), u_s, v_s, end_s)
    for g in G:
        y = y_s[g][0:C, :]
        for h in range(1, H):
            y = y + y_s[g][h * C:(h + 1) * C, :]
        y_ref[g] = y
        st_ref[g] = st[g] * jnp.exp(cl_end[g]) + upd[g]


def _wkv(r, ld, k2, v, kn, a, B, S):
    C = WKV_CHUNK
    nc = S // C
    W = RWKV_WIDTH
    G = WKV_GROUP
    blk = lambda: pl.BlockSpec((G, C, W), lambda b, c: (b, c, 0))
    seq = lambda t: t.reshape(B, S, W)
    y = pl.pallas_call(
        _wkv_kernel,
        grid=(B // G, nc),
        in_specs=[blk() for _ in range(6)],
        out_specs=blk(),
        out_shape=jax.ShapeDtypeStruct((B, S, W), F32),
        scratch_shapes=[pltpu.VMEM((G, W, W), F32)],
        compiler_params=_params("arbitrary", "arbitrary"),
        name="wkv7",
    )(seq(r), seq(ld), seq(k2), seq(v), seq(kn), seq(a))
    return y.reshape(B * S, W)


def _wkv_post_kernel(y_ref, r_ref, k2_ref, v_ref, g_ref, rk_ref, lg_ref, lb_ref, ones_ref, o_ref):
    ones = ones_ref[...]

    def head_sum(x):
        hi = x.astype(BF16)
        lo = (x - hi.astype(F32)).astype(BF16)
        return (jnp.dot(hi, ones, preferred_element_type=F32) + jnp.dot(lo, ones, preferred_element_type=F32))

    y = y_ref[...]
    r = r_ref[...]
    k2 = k2_ref[...]
    inv_n = 1.0 / HEAD_DIM
    d = y - head_sum(y) * inv_n
    var = head_sum(d * d) * inv_n
    yn = d * lax.rsqrt(var + RWKV_LN_EPS) * lg_ref[...] + lb_ref[...]
    bonus = head_sum(r * k2 * rk_ref[...]) * v_ref[...]
    o_ref[...] = (yn + bonus) * g_ref[...]


def _wkv_post(y, r, k2, v, g, r_k, lnx_g, lnx_b, head_ones_bf16):
    T, W = y.shape
    tm = 512
    blk = lambda: pl.BlockSpec((tm, W), lambda m: (m, 0))
    vec = lambda: pl.BlockSpec((1, W), lambda m: (0, 0))
    return pl.pallas_call(
        _wkv_post_kernel,
        grid=(T // tm,),
        in_specs=[blk() for _ in range(5)] + [vec(), vec(), vec(), pl.BlockSpec((W, W), lambda m: (0, 0))],
        out_specs=blk(),
        out_shape=jax.ShapeDtypeStruct((T, W), F32),
        compiler_params=_params("arbitrary"),
        name="wkv_post",
    )(y, r, k2, v, g, r_k, lnx_g, lnx_b, head_ones_bf16)


def _outproj_ln_kernel(att_ref, lru_ref, rwk_ref, x_ref, wa_ref, wl_ref, wr_ref, g_ref, b_ref, o_ref, ob_ref):
    h = jnp.dot(att_ref[...].astype(BF16), wa_ref[...], preferred_element_type=F32)
    h = h + jnp.dot(lru_ref[...].astype(BF16), wl_ref[...], preferred_element_type=F32)
    h = h + jnp.dot(rwk_ref[...].astype(BF16), wr_ref[...], preferred_element_type=F32)
    y = _layer_norm(DEEPNORM_ALPHA * x_ref[...] + h, g_ref[...], b_ref[...])
    o_ref[...] = y
    ob_ref[...] = y.astype(BF16)


def _outproj_ln(att, lru, rwk, x, wa, wl, wr, g, b):
    T, D = x.shape
    tm = 512
    row = lambda w: pl.BlockSpec((tm, w), lambda m: (m, 0))
    full = lambda a: pl.BlockSpec(a.shape, lambda m: (0, 0))
    return pl.pallas_call(
        _outproj_ln_kernel,
        grid=(T // tm,),
        in_specs=[row(ATT_WIDTH), row(LRU_WIDTH), row(RWKV_WIDTH), row(D), full(wa), full(wl), full(wr), full(g), full(b)],
        out_specs=[row(D), row(D)],
        out_shape=[jax.ShapeDtypeStruct((T, D), F32), jax.ShapeDtypeStruct((T, D), BF16)],
        compiler_params=_params("arbitrary"),
        name="outproj_ln",
    )(att, lru, rwk, x, wa, wl, wr, g, b)


def _ffn_kernel(xb_ref, x_ref, wg_ref, wu_ref, wd_ref, g_ref, b_ref, o_ref, ob_ref, acc_ref, *, nf):
    f = pl.program_id(1)
    xb = xb_ref[...]
    hg = jnp.dot(xb, wg_ref[...], preferred_element_type=F32)
    hu = jnp.dot(xb, wu_ref[...], preferred_element_type=F32)
    hh = (hg * _sigmoid(hg) * hu).astype(BF16)
    contrib = jnp.dot(hh, wd_ref[...], preferred_element_type=F32)

    @pl.when(f == 0)
    def _():
        acc_ref[...] = contrib

    @pl.when(f > 0)
    def _():
        acc_ref[...] = acc_ref[...] + contrib

    @pl.when(f == nf - 1)
    def _():
        y = _layer_norm(DEEPNORM_ALPHA * x_ref[...] + acc_ref[...], g_ref[...], b_ref[...])
        o_ref[...] = y
        ob_ref[...] = y.astype(BF16)


def _ffn_ln(xb, x, wg, wu, wd, g, b):
    T, D = x.shape
    F = wg.shape[1]
    tm = 512
    nf = 2
    tf = F // nf
    return pl.pallas_call(
        functools.partial(_ffn_kernel, nf=nf),
        grid=(T // tm, nf),
        in_specs=[pl.BlockSpec((tm, D), lambda m, f: (m, 0)),
                  pl.BlockSpec((tm, D), lambda m, f: (m, 0)),
                  pl.BlockSpec((D, tf), lambda m, f: (0, f)),
                  pl.BlockSpec((D, tf), lambda m, f: (0, f)),
                  pl.BlockSpec((tf, D), lambda m, f: (f, 0)),
                  pl.BlockSpec((1, D), lambda m, f: (0, 0)),
                  pl.BlockSpec((1, D), lambda m, f: (0, 0))],
        out_specs=[pl.BlockSpec((tm, D), lambda m, f: (m, 0)), pl.BlockSpec((tm, D), lambda m, f: (m, 0))],
        out_shape=[jax.ShapeDtypeStruct((T, D), F32), jax.ShapeDtypeStruct((T, D), BF16)],
        scratch_shapes=[pltpu.VMEM((tm, D), F32)],
        compiler_params=_params("arbitrary", "arbitrary"),
        name="ffn_ln",
    )(xb, x, wg, wu, wd, g, b)


def _router_kernel(x_ref, wrt_ref, tri_ref, g_ref, pos_ref, cnt_ref):
    tt = x_ref.shape[0]
    logits = lax.dot_general(wrt_ref[...], x_ref[...], NT_DIMS, precision=HI, preferred_element_type=F32)
    row = lax.broadcasted_iota(jnp.int32, logits.shape, 0).astype(F32)
    m1 = jnp.max(logits, axis=0, keepdims=True)
    i1 = jnp.min(jnp.where(logits == m1, row, float(N_EXPERTS)), axis=0, keepdims=True)
    rest = jnp.where(row == i1, -jnp.inf, logits)
    m2 = jnp.max(rest, axis=0, keepdims=True)
    i2 = jnp.min(jnp.where(rest == m2, row, float(N_EXPERTS)), axis=0, keepdims=True)
    e = jnp.exp(m2 - m1)
    g1 = 1.0 / (1.0 + e)
    g2 = e / (1.0 + e)
    g_ref[0] = jnp.where(row == i1, g1, jnp.where(row == i2, g2, 0.0))
    ind = jnp.where(row == i1, 1.0, jnp.where(row == i2, 1.0, 0.0))
    csum = jnp.dot(ind.astype(BF16), tri_ref[...], preferred_element_type=F32)
    pos_ref[0] = jnp.where(ind > 0.5, csum - 1.0, -1.0)
    cnt_ref[0] = jnp.broadcast_to(csum[:, tt - 1:tt], (N_EXPERTS, LANES))


def _router(x, wrt, tri):
    T, D = x.shape
    tt = MOE_TILE
    nt = T // tt
    tile = lambda: pl.BlockSpec((1, N_EXPERTS, tt), lambda m: (m, 0, 0))
    return pl.pallas_call(
        _router_kernel,
        grid=(nt,),
        in_specs=[pl.BlockSpec((tt, D), lambda m: (m, 0)), pl.BlockSpec((N_EXPERTS, D), lambda m: (0, 0)),
                  pl.BlockSpec((tt, tt), lambda m: (0, 0))],
        out_specs=[tile(), tile(), pl.BlockSpec((1, N_EXPERTS, LANES), lambda m: (m, 0, 0))],
        out_shape=[jax.ShapeDtypeStruct((nt, N_EXPERTS, tt), F32), jax.ShapeDtypeStruct((nt, N_EXPERTS, tt), F32),
                   jax.ShapeDtypeStruct((nt, N_EXPERTS, LANES), F32)],
        compiler_params=_params("arbitrary"),
        name="moe_router",
    )(x, wrt, tri)


def _moe_kernel(cnt_ref, xb_ref, gate_ref, pos_ref, wg_ref, wu_ref, wd_ref, o_ref, xg_ref, yacc_ref, gcol_ref, *, nf):
    t = pl.program_id(0)
    e = pl.program_id(1)
    f = pl.program_id(2)
    tt = xb_ref.shape[0]
    R = MOE_ROWS
    n_chunks = (cnt_ref[t * N_EXPERTS + e] + (R - 1)) // R
    pos_row = pos_ref[0, pl.ds(e, 1), :]
    slot = lax.broadcasted_iota(jnp.int32, (R, tt), 0).astype(F32)

    def selection(r):
        return jnp.where(pos_row == slot + (r * R).astype(F32), 1.0, 0.0)

    @pl.when(jnp.logical_and(e == 0, f == 0))
    def _():
        o_ref[...] = jnp.zeros_like(o_ref)

    @pl.when(f == 0)
    def _():
        gate_row = gate_ref[0, pl.ds(e, 1), :]

        def gather(r, carry):
            sel = selection(r)
            rows = pl.ds(pl.multiple_of(r * R, R), R)
            xg_ref[rows, :] = jnp.dot(sel.astype(BF16), xb_ref[...], preferred_element_type=F32).astype(BF16)
            gcol_ref[rows, :] = jnp.broadcast_to(jnp.sum(sel * gate_row, axis=1, keepdims=True), (R, LANES))
            return carry

        lax.fori_loop(0, n_chunks, gather, 0)

    def expert(r, carry):
        rows = pl.ds(pl.multiple_of(r * R, R), R)
        xc = xg_ref[rows, :]
        hg = jnp.dot(xc, wg_ref[0], preferred_element_type=F32)
        hu = jnp.dot(xc, wu_ref[0], preferred_element_type=F32)
        contrib = jnp.dot((hg * _sigmoid(hg) * hu).astype(BF16), wd_ref[0], preferred_element_type=F32)

        @pl.when(f == 0)
        def _():
            yacc_ref[rows, :] = contrib

        @pl.when(f > 0)
        def _():
            yacc_ref[rows, :] = yacc_ref[rows, :] + contrib

        @pl.when(f == nf - 1)
        def _():
            gate = jnp.concatenate([gcol_ref[rows, :]] * (o_ref.shape[1] // LANES), axis=1)
            ys = (yacc_ref[rows, :] * gate).astype(BF16)
            o_ref[...] = o_ref[...] + lax.dot_general(selection(r).astype(BF16), ys, TN_DIMS,
                                                      preferred_element_type=F32)
        return carry

    lax.fori_loop(0, n_chunks, expert, 0)


def _moe(xb, gates_t, pos_t, counts, wg, wu, wd):
    T, D = xb.shape
    F = wg.shape[2]
    tt = MOE_TILE
    nf = 4
    tf = F // nf
    grid_spec = pltpu.PrefetchScalarGridSpec(
        num_scalar_prefetch=1,
        grid=(T // tt, N_EXPERTS, nf),
        in_specs=[pl.BlockSpec((tt, D), lambda m, e, f, c: (m, 0)),
                  pl.BlockSpec((1, N_EXPERTS, tt), lambda m, e, f, c: (m, 0, 0)),
                  pl.BlockSpec((1, N_EXPERTS, tt), lambda m, e, f, c: (m, 0, 0)),
                  pl.BlockSpec((1, D, tf), lambda m, e, f, c: (e, 0, f)),
                  pl.BlockSpec((1, D, tf), lambda m, e, f, c: (e, 0, f)),
                  pl.BlockSpec((1, tf, D), lambda m, e, f, c: (e, f, 0))],
        out_specs=pl.BlockSpec((tt, D), lambda m, e, f, c: (m, 0)),
        scratch_shapes=[pltpu.VMEM((tt, D), BF16), pltpu.VMEM((tt, D), F32), pltpu.VMEM((tt, LANES), F32)],
    )
    return pl.pallas_call(
        functools.partial(_moe_kernel, nf=nf),
        grid_spec=grid_spec,
        out_shape=jax.ShapeDtypeStruct((T, D), F32),
        compiler_params=_params("arbitrary", "arbitrary", "arbitrary"),
        name="moe_experts",
    )(counts, xb, gates_t, pos_t, wg, wu, wd)


def _resid_ln_kernel(x_ref, f_ref, g_ref, b_ref, o_ref):
    o_ref[...] = _layer_norm(DEEPNORM_ALPHA * x_ref[...] + f_ref[...], g_ref[...], b_ref[...])


def _resid_ln(x, f, g, b):
    T, D = x.shape
    tm = 512
    row = lambda: pl.BlockSpec((tm, D), lambda m: (m, 0))
    vec = lambda: pl.BlockSpec((1, D), lambda m: (0, 0))
    return pl.pallas_call(
        _resid_ln_kernel,
        grid=(T // tm,),
        in_specs=[row(), row(), vec(), vec()],
        out_specs=row(),
        out_shape=jax.ShapeDtypeStruct((T, D), F32),
        compiler_params=_params("arbitrary"),
        name="resid_ln",
    )(x, f, g, b)


def _block_diag(w):
    n, d, _ = w.shape
    out = jnp.zeros((n * d, n * d), w.dtype)
    for i in range(n):
        out = out.at[i * d:(i + 1) * d, i * d:(i + 1) * d].set(w[i])
    return out


def _rope_tables(positions):
    B, S = positions.shape
    half = ROPE_DIMS // 2
    inv_freq = ROPE_THETA ** (-jnp.arange(0, ROPE_DIMS, 2, dtype=F32) / ROPE_DIMS)
    d = jnp.arange(LANES) % HEAD_DIM
    ang = positions.astype(F32).reshape(B * S, 1) * inv_freq[d % half][None, :]
    cos = jnp.cos(ang)
    sin = jnp.sin(ang)
    c = jnp.where(d < ROPE_DIMS, cos, 1.0)
    s1 = jnp.where(d < half, -sin, 0.0)
    s2 = jnp.where((d >= half) & (d < ROPE_DIMS), sin, 0.0)
    return c, s1, s2


def _mixer_layer(x, xb, rope, B, S, w_in, conv_w, conv_b, ga_w, ga_b, gx_w, gx_b, lam,
                 mu, w0, w_up, a0, a_up, g_up, k_k, k_a, r_k, lnx_g, lnx_b, w_out, ln_g, ln_b):
    W = RWKV_WIDTH
    row = lambda t: t.reshape(1, -1)
    proj = _in_proj(xb, w_in.astype(BF16))
    q, k, vt, pen = _attn_prep(proj, *rope, B, S)
    att = _attention(q, k, vt, pen, B, S)
    lru = _rglru(proj, conv_w, row(conv_b), _block_diag(ga_w), row(ga_b), _block_diag(gx_w), row(gx_b), row(lam), B, S)
    head_ones = _block_diag(jnp.ones((RWKV_HEADS, HEAD_DIM, HEAD_DIM), F32))
    wlr = jnp.zeros((W, 3 * W), F32)
    wlr = wlr.at[0:64, 0:W].set(w_up).at[64:128, W:2 * W].set(a_up).at[128:256, 2 * W:3 * W].set(g_up)
    r, ld, k2, v, kn, a, g = _rwkv_prep(proj, row(mu), wlr, row(w0), row(a0), row(k_k), row(k_a), head_ones, B, S)
    y = _wkv(r, ld, k2, v, kn, a, B, S)
    rwk = _wkv_post(y, r, k2, v, g, row(r_k), row(lnx_g), row(lnx_b), head_ones.astype(BF16))
    wo = w_out.astype(BF16)
    return _outproj_ln(att, lru, rwk, x, wo[0:ATT_WIDTH], wo[ATT_WIDTH:ATT_WIDTH + LRU_WIDTH],
                       wo[ATT_WIDTH + LRU_WIDTH:], row(ln_g), row(ln_b))


def kernel(x, positions, w_in, lru_conv_w, lru_conv_b, lru_ga_w, lru_ga_b, lru_gx_w, lru_gx_b, lru_lambda, rwkv_mu, rwkv_w0, rwkv_w_up, rwkv_a0, rwkv_a_up, rwkv_g_up, rwkv_k_k, rwkv_k_a, rwkv_r_k, rwkv_lnx_g, rwkv_lnx_b, w_out, ln1_g, ln1_b, ffn_w_gate, ffn_w_up, ffn_w_down, moe_router, moe_w_gate, moe_w_up, moe_w_down, ln2_g, ln2_b):
    B, S, D = x.shape
    T = B * S
    rope = _rope_tables(positions)
    xf = x.reshape(T, D)
    xb = xf.astype(BF16)
    for l in range(DEPTH):
        xf, xb = _mixer_layer(xf, xb, rope, B, S, w_in[l], lru_conv_w[l], lru_conv_b[l], lru_ga_w[l], lru_ga_b[l],
                              lru_gx_w[l], lru_gx_b[l], lru_lambda[l], rwkv_mu[l], rwkv_w0[l], rwkv_w_up[l],
                              rwkv_a0[l], rwkv_a_up[l], rwkv_g_up[l], rwkv_k_k[l], rwkv_k_a[l], rwkv_r_k[l],
                              rwkv_lnx_g[l], rwkv_lnx_b[l], w_out[l], ln1_g[l], ln1_b[l])
        g2 = ln2_g[l].reshape(1, D)
        b2 = ln2_b[l].reshape(1, D)
        if l % 2 == 0:
            i = l // 2
            xf, xb = _ffn_ln(xb, xf, ffn_w_gate[i].astype(BF16), ffn_w_up[i].astype(BF16),
                             ffn_w_down[i].astype(BF16), g2, b2)
        else:
            i = l // 2
            idx = jnp.arange(MOE_TILE)
            tri = (idx[:, None] <= idx[None, :]).astype(BF16)
            gates_t, pos_t, cnt = _router(xf, moe_router[i].T, tri)
            counts = cnt[:, :, 0].astype(jnp.int32).reshape(-1)
            f = _moe(xb, gates_t, pos_t, counts, moe_w_gate[i].astype(BF16), moe_w_up[i].astype(BF16),
                     moe_w_down[i].astype(BF16))
            xf = _resid_ln(xf, f, g2, b2)
            xb = xf.astype(BF16)
    return xf.reshape(B, S, D)
```

```python
import functools

import jax
import jax.numpy as jnp
from jax import lax
from jax.experimental import pallas as pl
from jax.experimental.pallas import tpu as pltpu

F32 = jnp.float32
BF16 = jnp.bfloat16
HI = lax.Precision.HIGHEST
NT_DIMS = (((1,), (1,)), ((), ()))
TN_DIMS = (((0,), (0,)), ((), ()))

HEAD_DIM = 64
ATT_HEADS = 8
ATT_WIDTH = ATT_HEADS * HEAD_DIM
LRU_WIDTH = 256
RWKV_HEADS = 4
RWKV_WIDTH = 256
ROPE_DIMS = 16
ROPE_THETA = 500000.0
MOBA_BLOCK = 256
MOBA_TOPK = 3
CONV_WIDTH = 4
LRU_C = 8.0
RWKV_LN_EPS = 64e-5
N_EXPERTS = 8
LN_EPS = 1e-5
DEPTH = 2
DEEPNORM_ALPHA = (2 * DEPTH) ** 0.25
NEG_INF = -1e30
LOG2_E = 1.4426950408889634

LANES = 128
VMEM_LIMIT = 56 * 1024 * 1024

WKV_CHUNK = 64
WKV_GROUP = 8
LRU_CHUNK = 256
RWKV_PREP_ROWS = 512
MOE_TILE = 1024
MOE_ROWS = 256


def _params(*sem):
    return pltpu.CompilerParams(dimension_semantics=sem, vmem_limit_bytes=VMEM_LIMIT)


def _sigmoid(x):
    return 1.0 / (1.0 + jnp.exp(-x))


def _softplus(x):
    return jnp.maximum(x, 0.0) + jnp.log1p(jnp.exp(-jnp.abs(x)))


def _expm1(z):
    u = jnp.exp(z)
    um1 = u - 1.0
    return jnp.where(u == 1.0, z, jnp.where(um1 == -1.0, -1.0, um1 * z / jnp.log(u)))


def _layer_norm(y, g, b):
    m = jnp.mean(y, axis=-1, keepdims=True)
    d = y - m
    var = jnp.mean(d * d, axis=-1, keepdims=True)
    return d * lax.rsqrt(var + LN_EPS) * g + b


def _matmul_kernel(x_ref, w_ref, o_ref):
    o_ref[...] = jnp.dot(x_ref[...], w_ref[...], preferred_element_type=F32)


def _in_proj(xb, w):
    T, D = xb.shape
    N = w.shape[1]
    tm, tn = 512, 1024
    return pl.pallas_call(
        _matmul_kernel,
        grid=(N // tn, T // tm),
        in_specs=[pl.BlockSpec((tm, D), lambda n, m: (m, 0)),
                  pl.BlockSpec((D, tn), lambda n, m: (0, n))],
        out_specs=pl.BlockSpec((tm, tn), lambda n, m: (m, n)),
        out_shape=jax.ShapeDtypeStruct((T, N), F32),
        compiler_params=_params("arbitrary", "arbitrary"),
        name="in_proj",
    )(xb, w)


def _attn_prep_kernel(proj_ref, c_ref, s1_ref, s2_ref, q_ref, k_ref, vt_ref, pen_ref, km_ref, *, nb):
    i = pl.program_id(1)

    @pl.when(i == 0)
    def _():
        km_ref[...] = jnp.zeros_like(km_ref)

    c = c_ref[...]
    s1 = s1_ref[...]
    s2 = s2_ref[...]

    def rope(xt):
        return xt * c + pltpu.roll(xt, LANES - ROPE_DIMS // 2, 1) * s1 + pltpu.roll(xt, ROPE_DIMS // 2, 1) * s2

    q_tiles = []
    for ct in range(ATT_WIDTH // LANES):
        lo, hi = ct * LANES, (ct + 1) * LANES
        qr = rope(proj_ref[:, lo:hi])
        q_tiles.append(qr)
        q_ref[:, lo:hi] = (qr * (HEAD_DIM ** -0.5 * LOG2_E)).astype(BF16)
        kr = rope(proj_ref[:, ATT_WIDTH + lo:ATT_WIDTH + hi])
        k_ref[0, 0, :, lo:hi] = kr.astype(BF16)
        km_row = lax.broadcasted_iota(jnp.int32, (nb, LANES), 0)
        km_ref[:, lo:hi] = jnp.where(km_row == i, jnp.mean(kr, axis=0, keepdims=True), km_ref[:, lo:hi])
    vt_ref[0, 0] = proj_ref[:, 2 * ATT_WIDTH:3 * ATT_WIDTH].T.astype(BF16)

    q_rot = jnp.concatenate(q_tiles, axis=1)
    km = km_ref[...]
    km_rows = jnp.concatenate([km] * ATT_HEADS, axis=0)
    row_head = lax.broadcasted_iota(jnp.int32, km_rows.shape, 0) // nb
    lane_head = lax.broadcasted_iota(jnp.int32, km_rows.shape, 1) // HEAD_DIM
    g_all = lax.dot_general(jnp.where(row_head == lane_head, km_rows, 0.0), q_rot, NT_DIMS,
                            precision=HI, preferred_element_type=F32)
    n_iota = lax.broadcasted_iota(jnp.int32, (nb, MOBA_BLOCK), 0)
    past = n_iota < i
    for h in range(ATT_HEADS):
        g = g_all[h * nb:(h + 1) * nb, :]
        rank = jnp.zeros((nb, MOBA_BLOCK), F32)
        for m in range(nb):
            gm = g[m:m + 1, :]
            beats = jnp.where(gm > g, 1.0, jnp.where(gm == g, jnp.where(n_iota > m, 1.0, 0.0), 0.0))
            rank = rank + jnp.where(m < i, beats, 0.0)
        pen = jnp.where(past, jnp.where(rank < float(MOBA_TOPK), 0.0, NEG_INF), NEG_INF)
        half = (h % 2) * MOBA_BLOCK
        pen_ref[0, h // 2, :, half:half + MOBA_BLOCK] = pen


def _attn_prep(proj, rope_c, rope_s1, rope_s2, B, S):
    T = B * S
    nb = S // MOBA_BLOCK
    blk = MOBA_BLOCK
    return pl.pallas_call(
        functools.partial(_attn_prep_kernel, nb=nb),
        grid=(B, nb),
        in_specs=[pl.BlockSpec((blk, 3 * ATT_WIDTH), lambda b, i: (b * nb + i, 0)),
                  pl.BlockSpec((blk, LANES), lambda b, i: (b * nb + i, 0)),
                  pl.BlockSpec((blk, LANES), lambda b, i: (b * nb + i, 0)),
                  pl.BlockSpec((blk, LANES), lambda b, i: (b * nb + i, 0))],
        out_specs=[pl.BlockSpec((blk, ATT_WIDTH), lambda b, i: (b * nb + i, 0)),
                   pl.BlockSpec((1, 1, blk, ATT_WIDTH), lambda b, i: (b, i, 0, 0)),
                   pl.BlockSpec((1, 1, ATT_WIDTH, blk), lambda b, i: (b, i, 0, 0)),
                   pl.BlockSpec((1, ATT_HEADS // 2, nb, 2 * blk), lambda b, i: (b, 0, 0, i))],
        out_shape=[jax.ShapeDtypeStruct((T, ATT_WIDTH), BF16),
                   jax.ShapeDtypeStruct((B, nb, blk, ATT_WIDTH), BF16),
                   jax.ShapeDtypeStruct((B, nb, ATT_WIDTH, blk), BF16),
                   jax.ShapeDtypeStruct((B, ATT_HEADS // 2, nb, 2 * S), F32)],
        scratch_shapes=[pltpu.VMEM((nb, ATT_WIDTH), F32)],
        compiler_params=_params("arbitrary", "arbitrary"),
        name="attn_prep",
    )(proj, rope_c, rope_s1, rope_s2)


def _attn_kernel(q_ref, k_ref, vt_ref, pen_ref, o_ref):
    i = pl.program_id(2)
    blk = MOBA_BLOCK
    q = q_ref[...]
    lane_head = lax.broadcasted_iota(jnp.int32, q.shape, 1) // HEAD_DIM
    zero = jnp.zeros_like(q)
    qs = jnp.concatenate([jnp.where(lane_head == 0, q, zero), jnp.where(lane_head == 1, q, zero)], axis=0)
    kidx = lax.broadcasted_iota(jnp.int32, (blk, 2 * blk), 0)
    qidx = lax.broadcasted_iota(jnp.int32, (blk, 2 * blk), 1) % blk
    s = lax.dot_general(k_ref[0, i], qs, NT_DIMS, preferred_element_type=F32)
    s = jnp.where(kidx <= qidx, s, NEG_INF)
    m0 = jnp.max(s, axis=0, keepdims=True)
    p = jnp.exp2(s - m0)
    l0 = jnp.sum(p, axis=0, keepdims=True)
    acc0 = jnp.dot(vt_ref[0, i], p.astype(BF16), preferred_element_type=F32)

    def body(jj, carry):
        m, l, acc = carry
        j0 = 2 * jj
        j1 = j0 + 1
        s0 = lax.dot_general(k_ref[0, j0], qs, NT_DIMS, preferred_element_type=F32) + pen_ref[0, 0, pl.ds(j0, 1), :]
        s1 = lax.dot_general(k_ref[0, j1], qs, NT_DIMS, preferred_element_type=F32) + pen_ref[0, 0, pl.ds(j1, 1), :]
        m_new = jnp.maximum(m, jnp.maximum(jnp.max(s0, axis=0, keepdims=True), jnp.max(s1, axis=0, keepdims=True)))
        alpha = jnp.exp2(m - m_new)
        p0 = jnp.exp2(s0 - m_new)
        p1 = jnp.exp2(s1 - m_new)
        l = alpha * l + jnp.sum(p0, axis=0, keepdims=True) + jnp.sum(p1, axis=0, keepdims=True)
        acc = (acc * alpha + jnp.dot(vt_ref[0, j0], p0.astype(BF16), preferred_element_type=F32)
               + jnp.dot(vt_ref[0, j1], p1.astype(BF16), preferred_element_type=F32))
        return m_new, l, acc

    _, l, acc = lax.fori_loop(0, (i + 1) // 2, body, (m0, l0, acc0))
    out = acc / l
    row = lax.broadcasted_iota(jnp.int32, (LANES, blk), 0)
    o_ref[...] = jnp.where(row < HEAD_DIM, out[:, 0:blk], out[:, blk:2 * blk]).T


def _attention(q, k, vt, pen, B, S):
    T = B * S
    nb = S // MOBA_BLOCK
    blk = MOBA_BLOCK
    n_pairs = ATT_WIDTH // LANES
    return pl.pallas_call(
        _attn_kernel,
        grid=(B, n_pairs, nb),
        in_specs=[pl.BlockSpec((blk, LANES), lambda b, hp, i: (b * nb + i, hp)),
                  pl.BlockSpec((1, nb, blk, LANES), lambda b, hp, i: (b, 0, 0, hp)),
                  pl.BlockSpec((1, nb, LANES, blk), lambda b, hp, i: (b, 0, hp, 0)),
                  pl.BlockSpec((1, 1, nb, 2 * blk), lambda b, hp, i: (b, hp, 0, i))],
        out_specs=pl.BlockSpec((blk, LANES), lambda b, hp, i: (b * nb + i, hp)),
        out_shape=jax.ShapeDtypeStruct((T, ATT_WIDTH), F32),
        compiler_params=_params("arbitrary", "arbitrary", "arbitrary"),
        name="moba_attn",
    )(q, k, vt, pen)


def _lru_kernel(p_ref, cw_ref, cb_ref, ga_ref, gab_ref, gx_ref, gxb_ref, lam_ref, o_ref, xbuf, hc):
    t = pl.program_id(1)
    tc = LRU_CHUNK
    pad = 8

    @pl.when(t == 0)
    def _():
        xbuf[0:pad, :] = jnp.zeros((pad, LRU_WIDTH), F32)
        hc[...] = jnp.zeros_like(hc)

    @pl.when(t > 0)
    def _():
        xbuf[0:pad, :] = xbuf[tc:tc + pad, :]

    x = p_ref[:, 0:LRU_WIDTH]
    gate = p_ref[:, LRU_WIDTH:2 * LRU_WIDTH]
    xbuf[pad:pad + tc, :] = x
    xc = cb_ref[...] + x * cw_ref[CONV_WIDTH - 1:CONV_WIDTH, :]
    for j in range(CONV_WIDTH - 1):
        back = CONV_WIDTH - 1 - j
        xc = xc + xbuf[pad - back:pad - back + tc, :] * cw_ref[j:j + 1, :]

    r = _sigmoid(jnp.dot(xc, ga_ref[...], precision=HI, preferred_element_type=F32) + gab_ref[...])
    ig = _sigmoid(jnp.dot(xc, gx_ref[...], precision=HI, preferred_element_type=F32) + gxb_ref[...])
    log_a = -LRU_C * r * _softplus(-lam_ref[...])
    a = jnp.exp(log_a)
    u = jnp.sqrt(-_expm1(2.0 * log_a)) * (ig * xc)

    rows = lax.broadcasted_iota(jnp.int32, (tc, LRU_WIDTH), 0)
    d = 1
    while d < tc:
        keep = rows >= d
        a_prev = jnp.where(keep, pltpu.roll(a, d, 0), 1.0)
        u_prev = jnp.where(keep, pltpu.roll(u, d, 0), 0.0)
        u = a * u_prev + u
        a = a * a_prev
        d *= 2
    h = u + a * hc[...]
    hc[...] = h[tc - 1:tc, :]
    gl = 0.5 * gate * (1.0 + jnp.tanh(0.7978845608028654 * (gate + 0.044715 * gate * gate * gate)))
    o_ref[...] = h * gl


def _rglru(proj, cw, cb, ga, gab, gx, gxb, lam, B, S):
    T = B * S
    tc = LRU_CHUNK
    nt = S // tc
    col = (3 * ATT_WIDTH) // (2 * LRU_WIDTH)
    vec = lambda: pl.BlockSpec((1, LRU_WIDTH), lambda b, t: (0, 0))
    mat = lambda: pl.BlockSpec((LRU_WIDTH, LRU_WIDTH), lambda b, t: (0, 0))
    return pl.pallas_call(
        _lru_kernel,
        grid=(B, nt),
        in_specs=[pl.BlockSpec((tc, 2 * LRU_WIDTH), lambda b, t: (b * nt + t, col)),
                  pl.BlockSpec((CONV_WIDTH, LRU_WIDTH), lambda b, t: (0, 0)),
                  vec(), mat(), vec(), mat(), vec(), vec()],
        out_specs=pl.BlockSpec((tc, LRU_WIDTH), lambda b, t: (b * nt + t, 0)),
        out_shape=jax.ShapeDtypeStruct((T, LRU_WIDTH), F32),
        scratch_shapes=[pltpu.VMEM((tc + 8, LRU_WIDTH), F32), pltpu.VMEM((1, LRU_WIDTH), F32)],
        compiler_params=_params("arbitrary", "arbitrary"),
        name="rglru",
    )(proj, cw, cb, ga, gab, gx, gxb, lam)


def _rwkv_prep_kernel(p_ref, pp_ref, mu_ref, wlr_ref, w0_ref, a0_ref, kk_ref, ka_ref, ones_ref,
                      r_ref, ld_ref, k2_ref, v_ref, kn_ref, a_ref, g_ref):
    i = pl.program_id(1)
    W = RWKV_WIDTH
    p = p_ref[...]
    rows = lax.broadcasted_iota(jnp.int32, p.shape, 0)
    prev_last = jnp.where(i == 0, 0.0, pp_ref[7:8, :])
    p_prev = jnp.where(rows == 0, prev_last, pltpu.roll(p, 1, 0))
    pf = p + (p_prev - p) * mu_ref[...]
    r = pf[:, 0:W]
    k = pf[:, W:2 * W]
    v = pf[:, 2 * W:3 * W]
    z = pf[:, 3 * W:4 * W]
    lane = lax.broadcasted_iota(jnp.int32, z.shape, 1)
    zz = jnp.where(lane < 64, jnp.tanh(z), jnp.where(lane < 128, z, _sigmoid(z)))
    lr = jnp.dot(zz, wlr_ref[...], precision=HI, preferred_element_type=F32)
    w = -_softplus(-(w0_ref[...] + lr[:, 0:W])) - 0.5
    a = _sigmoid(a0_ref[...] + lr[:, W:2 * W])
    kk = k * kk_ref[...]
    ssq = jnp.dot(kk * kk, ones_ref[...], precision=HI, preferred_element_type=F32)
    kk = kk / jnp.maximum(jnp.sqrt(ssq), 1e-12)
    r_ref[...] = r
    ld_ref[...] = -jnp.exp(w)
    k2_ref[...] = k * (1.0 + (a - 1.0) * ka_ref[...])
    v_ref[...] = v
    kn_ref[...] = kk
    a_ref[...] = a
    g_ref[...] = lr[:, 2 * W:3 * W]


def _rwkv_prep(proj, mu, wlr, w0, a0, k_k, k_a, head_ones, B, S):
    T = B * S
    tr = RWKV_PREP_ROWS
    nt = S // tr
    W = RWKV_WIDTH
    col = 2
    vec = lambda: pl.BlockSpec((1, W), lambda b, t: (0, 0))
    out = lambda: pl.BlockSpec((tr, W), lambda b, t: (b * nt + t, 0))
    return pl.pallas_call(
        _rwkv_prep_kernel,
        grid=(B, nt),
        in_specs=[pl.BlockSpec((tr, 4 * W), lambda b, t: (b * nt + t, col)),
                  pl.BlockSpec((8, 4 * W), lambda b, t: (jnp.maximum((b * nt + t) * (tr // 8) - 1, 0), col)),
                  pl.BlockSpec((1, 4 * W), lambda b, t: (0, 0)),
                  pl.BlockSpec((W, 3 * W), lambda b, t: (0, 0)),
                  vec(), vec(), vec(), vec(),
                  pl.BlockSpec((W, W), lambda b, t: (0, 0))],
        out_specs=[out() for _ in range(7)],
        out_shape=[jax.ShapeDtypeStruct((T, W), F32) for _ in range(7)],
        compiler_params=_params("arbitrary", "arbitrary"),
        name="rwkv_prep",
    )(proj, proj, mu, wlr, w0, a0, k_k, k_a, head_ones)


def _wkv_kernel(r_ref, ld_ref, k2_ref, v_ref, kn_ref, a_ref, y_ref, st_ref):
    c = pl.program_id(1)
    C = WKV_CHUNK
    W = RWKV_WIDTH
    H = RWKV_HEADS

    @pl.when(c == 0)
    def _():
        st_ref[...] = jnp.zeros_like(st_ref)

    def mm(x, y):
        return jnp.dot(x.astype(BF16), y.astype(BF16), preferred_element_type=F32)

    def mm_nt(x, y):
        return lax.dot_general(x.astype(BF16), y.astype(BF16), NT_DIMS, preferred_element_type=F32)

    def mm_tn(x, y):
        return lax.dot_general(x.astype(BF16), y.astype(BF16), TN_DIMS, preferred_element_type=F32)

    tr = lax.broadcasted_iota(jnp.int32, (C, C), 0)
    tc = lax.broadcasted_iota(jnp.int32, (C, C), 1)
    tri = jnp.where(tr >= tc, 1.0, 0.0)
    row = lax.broadcasted_iota(jnp.int32, (W, W), 0)
    colm = lax.broadcasted_iota(jnp.int32, (W, W), 1)
    same_head = (row // C) == (colm // HEAD_DIM)
    strict = lambda x: jnp.where(same_head, jnp.where(row > colm, x, 0.0), 0.0)
    incl = lambda x: jnp.where(same_head, jnp.where(row >= colm, x, 0.0), 0.0)
    same16 = (row // 16) == (colm // 16)
    eye = jnp.where(row == colm, 1.0, 0.0)

    def stack(x):
        return jnp.where(same_head, jnp.concatenate([x] * H, axis=0), 0.0)

    G = range(WKV_GROUP)
    each = lambda fn, *xs: [fn(*(x[g] for x in xs)) for g in G]

    r = [r_ref[g] for g in G]
    ld = [ld_ref[g] for g in G]
    k2 = [k2_ref[g] for g in G]
    v_s = [stack(v_ref[g]) for g in G]
    kn = [kn_ref[g] for g in G]
    b_s = [kn[g] * a_ref[g] for g in G]
    cl = each(lambda x: jnp.dot(tri, x, precision=HI, preferred_element_type=F32), ld)
    cl_end = [x[C - 1:C, :] for x in cl]
    e_neg = [jnp.exp(-x) for x in cl]
    e_end = each(lambda ce, x: jnp.exp(ce - x), cl_end, cl)
    ar_s = [jnp.concatenate([stack(-kn[g] * jnp.exp(cl[g] - ld[g])), stack(r[g] * jnp.exp(cl[g]))], axis=0) for g in G]
    bk_s = [jnp.concatenate([stack(b_s[g] * e_neg[g]), stack(k2[g] * e_neg[g])], axis=0) for g in G]
    end_s = [jnp.concatenate([stack(b_s[g] * e_end[g]), stack(k2[g] * e_end[g])], axis=0) for g in G]

    prod = each(mm_nt, ar_s, bk_s)
    n_ab = [strict(p[0:W, 0:W]) for p in prod]
    a_ak = [strict(p[0:W, W:2 * W]) for p in prod]
    a_rb = [incl(p[W:2 * W, 0:W]) for p in prod]
    a_rk = [incl(p[W:2 * W, W:2 * W]) for p in prod]

    nd = [jnp.where(same16, n, 0.0) for n in n_ab]
    lo = each(lambda n, d: n - d, n_ab, nd)
    n2 = each(mm, nd, nd)
    p1 = [eye + d for d in nd]
    n4 = each(mm, n2, n2)
    p2 = each(lambda p, n: p + mm(p, n), p1, n2)
    n8 = each(mm, n4, n4)
    p3 = each(lambda p, n: p + mm(p, n), p2, n4)
    dinv = each(lambda p, n: p + mm(p, n), p3, n8)
    x1 = each(mm, dinv, lo)
    x2 = each(mm, x1, x1)
    y1 = [eye + x for x in x1]
    m_inv = each(lambda y, x: y + mm(y, x), y1, x2)
    tinv = each(mm, m_inv, dinv)

    st = [st_ref[g] for g in G]
    ar_st = each(mm_nt, ar_s, st)
    akv = each(mm, a_ak, v_s)
    u_s = each(lambda t, a, b: mm(t, a[0:W] + b), tinv, ar_st, akv)
    rkv = each(mm, a_rk, v_s)
    y_s = each(lambda a, m, u, b: a[W:2 * W] + mm(m, u) + b, ar_st, a_rb, u_s, rkv)
    upd = each(lambda u, v, e: mm_tn(jnp.concatenate([u, v], axis=0), e), u_s, v_s, end_s)
    for g in G:
        y = y_s[g][0:C, :]
        for h in range(1, H):
            y = y + y_s[g][h * C:(h + 1) * C, :]
        y_ref[g] = y
        st_ref[g] = st[g] * jnp.exp(cl_end[g]) + upd[g]


def _wkv(r, ld, k2, v, kn, a, B, S):
    C = WKV_CHUNK
    nc = S // C
    W = RWKV_WIDTH
    G = WKV_GROUP
    blk = lambda: pl.BlockSpec((G, C, W), lambda b, c: (b, c, 0))
    seq = lambda t: t.reshape(B, S, W)
    y = pl.pallas_call(
        _wkv_kernel,
        grid=(B // G, nc),
        in_specs=[blk() for _ in range(6)],
        out_specs=blk(),
        out_shape=jax.ShapeDtypeStruct((B, S, W), F32),
        scratch_shapes=[pltpu.VMEM((G, W, W), F32)],
        compiler_params=_params("arbitrary", "arbitrary"),
        name="wkv7",
    )(seq(r), seq(ld), seq(k2), seq(v), seq(kn), seq(a))
    return y.reshape(B * S, W)


def _wkv_post_kernel(y_ref, r_ref, k2_ref, v_ref, g_ref, rk_ref, lg_ref, lb_ref, ones_ref, o_ref):
    ones = ones_ref[...]

    def head_sum(x):
        hi = x.astype(BF16)
        lo = (x - hi.astype(F32)).astype(BF16)
        return (jnp.dot(hi, ones, preferred_element_type=F32) + jnp.dot(lo, ones, preferred_element_type=F32))

    y = y_ref[...]
    r = r_ref[...]
    k2 = k2_ref[...]
    inv_n = 1.0 / HEAD_DIM
    d = y - head_sum(y) * inv_n
    var = head_sum(d * d) * inv_n
    yn = d * lax.rsqrt(var + RWKV_LN_EPS) * lg_ref[...] + lb_ref[...]
    bonus = head_sum(r * k2 * rk_ref[...]) * v_ref[...]
    o_ref[...] = (yn + bonus) * g_ref[...]


def _wkv_post(y, r, k2, v, g, r_k, lnx_g, lnx_b, head_ones_bf16):
    T, W = y.shape
    tm = 512
    blk = lambda: pl.BlockSpec((tm, W), lambda m: (m, 0))
    vec = lambda: pl.BlockSpec((1, W), lambda m: (0, 0))
    return pl.pallas_call(
        _wkv_post_kernel,
        grid=(T // tm,),
        in_specs=[blk() for _ in range(5)] + [vec(), vec(), vec(), pl.BlockSpec((W, W), lambda m: (0, 0))],
        out_specs=blk(),
        out_shape=jax.ShapeDtypeStruct((T, W), F32),
        compiler_params=_params("arbitrary"),
        name="wkv_post",
    )(y, r, k2, v, g, r_k, lnx_g, lnx_b, head_ones_bf16)


def _outproj_ln_kernel(att_ref, lru_ref, rwk_ref, x_ref, wa_ref, wl_ref, wr_ref, g_ref, b_ref, o_ref, ob_ref):
    h = jnp.dot(att_ref[...].astype(BF16), wa_ref[...], preferred_element_type=F32)
    h = h + jnp.dot(lru_ref[...].astype(BF16), wl_ref[...], preferred_element_type=F32)
    h = h + jnp.dot(rwk_ref[...].astype(BF16), wr_ref[...], preferred_element_type=F32)
    y = _layer_norm(DEEPNORM_ALPHA * x_ref[...] + h, g_ref[...], b_ref[...])
    o_ref[...] = y
    ob_ref[...] = y.astype(BF16)


def _outproj_ln(att, lru, rwk, x, wa, wl, wr, g, b):
    T, D = x.shape
    tm = 512
    row = lambda w: pl.BlockSpec((tm, w), lambda m: (m, 0))
    full = lambda a: pl.BlockSpec(a.shape, lambda m: (0, 0))
    return pl.pallas_call(
        _outproj_ln_kernel,
        grid=(T // tm,),
        in_specs=[row(ATT_WIDTH), row(LRU_WIDTH), row(RWKV_WIDTH), row(D), full(wa), full(wl), full(wr), full(g), full(b)],
        out_specs=[row(D), row(D)],
        out_shape=[jax.ShapeDtypeStruct((T, D), F32), jax.ShapeDtypeStruct((T, D), BF16)],
        compiler_params=_params("arbitrary"),
        name="outproj_ln",
    )(att, lru, rwk, x, wa, wl, wr, g, b)


def _ffn_kernel(xb_ref, x_ref, wg_ref, wu_ref, wd_ref, g_ref, b_ref, o_ref, ob_ref, acc_ref, *, nf):
    f = pl.program_id(1)
    xb = xb_ref[...]
    hg = jnp.dot(xb, wg_ref[...], preferred_element_type=F32)
    hu = jnp.dot(xb, wu_ref[...], preferred_element_type=F32)
    hh = (hg * _sigmoid(hg) * hu).astype(BF16)
    contrib = jnp.dot(hh, wd_ref[...], preferred_element_type=F32)

    @pl.when(f == 0)
    def _():
        acc_ref[...] = contrib

    @pl.when(f > 0)
    def _():
        acc_ref[...] = acc_ref[...] + contrib

    @pl.when(f == nf - 1)
    def _():
        y = _layer_norm(DEEPNORM_ALPHA * x_ref[...] + acc_ref[...], g_ref[...], b_ref[...])
        o_ref[...] = y
        ob_ref[...] = y.astype(BF16)


def _ffn_ln(xb, x, wg, wu, wd, g, b):
    T, D = x.shape
    F = wg.shape[1]
    tm = 512
    nf = 2
    tf = F // nf
    return pl.pallas_call(
        functools.partial(_ffn_kernel, nf=nf),
        grid=(T // tm, nf),
        in_specs=[pl.BlockSpec((tm, D), lambda m, f: (m, 0)),
                  pl.BlockSpec((tm, D), lambda m, f: (m, 0)),
                  pl.BlockSpec((D, tf), lambda m, f: (0, f)),
                  pl.BlockSpec((D, tf), lambda m, f: (0, f)),
                  pl.BlockSpec((tf, D), lambda m, f: (f, 0)),
                  pl.BlockSpec((1, D), lambda m, f: (0, 0)),
                  pl.BlockSpec((1, D), lambda m, f: (0, 0))],
        out_specs=[pl.BlockSpec((tm, D), lambda m, f: (m, 0)), pl.BlockSpec((tm, D), lambda m, f: (m, 0))],
        out_shape=[jax.ShapeDtypeStruct((T, D), F32), jax.ShapeDtypeStruct((T, D), BF16)],
        scratch_shapes=[pltpu.VMEM((tm, D), F32)],
        compiler_params=_params("arbitrary", "arbitrary"),
        name="ffn_ln",
    )(xb, x, wg, wu, wd, g, b)


def _router_kernel(x_ref, wrt_ref, tri_ref, g_ref, pos_ref, cnt_ref):
    tt = x_ref.shape[0]
    logits = lax.dot_general(wrt_ref[...], x_ref[...], NT_DIMS, precision=HI, preferred_element_type=F32)
    row = lax.broadcasted_iota(jnp.int32, logits.shape, 0).astype(F32)
    m1 = jnp.max(logits, axis=0, keepdims=True)
    i1 = jnp.min(jnp.where(logits == m1, row, float(N_EXPERTS)), axis=0, keepdims=True)
    rest = jnp.where(row == i1, -jnp.inf, logits)
    m2 = jnp.max(rest, axis=0, keepdims=True)
    i2 = jnp.min(jnp.where(rest == m2, row, float(N_EXPERTS)), axis=0, keepdims=True)
    e = jnp.exp(m2 - m1)
    g1 = 1.0 / (1.0 + e)
    g2 = e / (1.0 + e)
    g_ref[0] = jnp.where(row == i1, g1, jnp.where(row == i2, g2, 0.0))
    ind = jnp.where(row == i1, 1.0, jnp.where(row == i2, 1.0, 0.0))
    csum = jnp.dot(ind.astype(BF16), tri_ref[...], preferred_element_type=F32)
    pos_ref[0] = jnp.where(ind > 0.5, csum - 1.0, -1.0)
    cnt_ref[0] = jnp.broadcast_to(csum[:, tt - 1:tt], (N_EXPERTS, LANES))


def _router(x, wrt, tri):
    T, D = x.shape
    tt = MOE_TILE
    nt = T // tt
    tile = lambda: pl.BlockSpec((1, N_EXPERTS, tt), lambda m: (m, 0, 0))
    return pl.pallas_call(
        _router_kernel,
        grid=(nt,),
        in_specs=[pl.BlockSpec((tt, D), lambda m: (m, 0)), pl.BlockSpec((N_EXPERTS, D), lambda m: (0, 0)),
                  pl.BlockSpec((tt, tt), lambda m: (0, 0))],
        out_specs=[tile(), tile(), pl.BlockSpec((1, N_EXPERTS, LANES), lambda m: (m, 0, 0))],
        out_shape=[jax.ShapeDtypeStruct((nt, N_EXPERTS, tt), F32), jax.ShapeDtypeStruct((nt, N_EXPERTS, tt), F32),
                   jax.ShapeDtypeStruct((nt, N_EXPERTS, LANES), F32)],
        compiler_params=_params("arbitrary"),
        name="moe_router",
    )(x, wrt, tri)


def _moe_kernel(cnt_ref, xb_ref, gate_ref, pos_ref, wg_ref, wu_ref, wd_ref, o_ref, xg_ref, yacc_ref, gcol_ref, *, nf):
    t = pl.program_id(0)
    e = pl.program_id(1)
    f = pl.program_id(2)
    tt = xb_ref.shape[0]
    R = MOE_ROWS
    half = R // 2
    n = cnt_ref[t * N_EXPERTS + e]
    rem = n % R
    n_full = n // R + jnp.where(rem > half, 1, 0)
    has_tail = jnp.logical_and(rem > 0, rem <= half)
    tail_start = pl.multiple_of(n_full * R, half)
    pos_row = pos_ref[0, pl.ds(e, 1), :]

    def selection(start, rows):
        slot = lax.broadcasted_iota(jnp.int32, (rows, tt), 0) + start
        return jnp.where(pos_row == slot.astype(F32), 1.0, 0.0)

    def gather(start, rows):
        sel = selection(start, rows)
        dst = pl.ds(start, rows)
        xg_ref[dst, :] = jnp.dot(sel.astype(BF16), xb_ref[...], preferred_element_type=F32).astype(BF16)
        gate_row = gate_ref[0, pl.ds(e, 1), :]
        gcol_ref[dst, :] = jnp.broadcast_to(jnp.sum(sel * gate_row, axis=1, keepdims=True), (rows, LANES))

    def expert(start, rows):
        dst = pl.ds(start, rows)
        xc = xg_ref[dst, :]
        hg = jnp.dot(xc, wg_ref[0], preferred_element_type=F32)
        hu = jnp.dot(xc, wu_ref[0], preferred_element_type=F32)
        contrib = jnp.dot((hg * _sigmoid(hg) * hu).astype(BF16), wd_ref[0], preferred_element_type=F32)

        @pl.when(f == 0)
        def _():
            yacc_ref[dst, :] = contrib

        @pl.when(f > 0)
        def _():
            yacc_ref[dst, :] = yacc_ref[dst, :] + contrib

        @pl.when(f == nf - 1)
        def _():
            gate = jnp.concatenate([gcol_ref[dst, :]] * (o_ref.shape[1] // LANES), axis=1)
            ys = (yacc_ref[dst, :] * gate).astype(BF16)
            o_ref[...] = o_ref[...] + lax.dot_general(selection(start, rows).astype(BF16), ys, TN_DIMS,
                                                      preferred_element_type=F32)

    def over_chunks(fn):
        def body(r, carry):
            fn(pl.multiple_of(r * R, R), R)
            return carry
        lax.fori_loop(0, n_full, body, 0)

        @pl.when(has_tail)
        def _():
            fn(tail_start, half)

    @pl.when(jnp.logical_and(e == 0, f == 0))
    def _():
        o_ref[...] = jnp.zeros_like(o_ref)

    @pl.when(f == 0)
    def _():
        over_chunks(gather)

    over_chunks(expert)


def _moe(xb, gates_t, pos_t, counts, wg, wu, wd):
    T, D = xb.shape
    F = wg.shape[2]
    tt = MOE_TILE
    nf = 4
    tf = F // nf
    grid_spec = pltpu.PrefetchScalarGridSpec(
        num_scalar_prefetch=1,
        grid=(T // tt, N_EXPERTS, nf),
        in_specs=[pl.BlockSpec((tt, D), lambda m, e, f, c: (m, 0)),
                  pl.BlockSpec((1, N_EXPERTS, tt), lambda m, e, f, c: (m, 0, 0)),
                  pl.BlockSpec((1, N_EXPERTS, tt), lambda m, e, f, c: (m, 0, 0)),
                  pl.BlockSpec((1, D, tf), lambda m, e, f, c: (e, 0, f)),
                  pl.BlockSpec((1, D, tf), lambda m, e, f, c: (e, 0, f)),
                  pl.BlockSpec((1, tf, D), lambda m, e, f, c: (e, f, 0))],
        out_specs=pl.BlockSpec((tt, D), lambda m, e, f, c: (m, 0)),
        scratch_shapes=[pltpu.VMEM((tt, D), BF16), pltpu.VMEM((tt, D), F32), pltpu.VMEM((tt, LANES), F32)],
    )
    return pl.pallas_call(
        functools.partial(_moe_kernel, nf=nf),
        grid_spec=grid_spec,
        out_shape=jax.ShapeDtypeStruct((T, D), F32),
        compiler_params=_params("arbitrary", "arbitrary", "arbitrary"),
        name="moe_experts",
    )(counts, xb, gates_t, pos_t, wg, wu, wd)


def _resid_ln_kernel(x_ref, f_ref, g_ref, b_ref, o_ref):
    o_ref[...] = _layer_norm(DEEPNORM_ALPHA * x_ref[...] + f_ref[...], g_ref[...], b_ref[...])


def _resid_ln(x, f, g, b):
    T, D = x.shape
    tm = 512
    row = lambda: pl.BlockSpec((tm, D), lambda m: (m, 0))
    vec = lambda: pl.BlockSpec((1, D), lambda m: (0, 0))
    return pl.pallas_call(
        _resid_ln_kernel,
        grid=(T // tm,),
        in_specs=[row(), row(), vec(), vec()],
        out_specs=row(),
        out_shape=jax.ShapeDtypeStruct((T, D), F32),
        compiler_params=_params("arbitrary"),
        name="resid_ln",
    )(x, f, g, b)


def _block_diag(w):
    n, d, _ = w.shape
    out = jnp.zeros((n * d, n * d), w.dtype)
    for i in range(n):
        out = out.at[i * d:(i + 1) * d, i * d:(i + 1) * d].set(w[i])
    return out


def _rope_tables(positions):
    B, S = positions.shape
    half = ROPE_DIMS // 2
    inv_freq = ROPE_THETA ** (-jnp.arange(0, ROPE_DIMS, 2, dtype=F32) / ROPE_DIMS)
    d = jnp.arange(LANES) % HEAD_DIM
    ang = positions.astype(F32).reshape(B * S, 1) * inv_freq[d % half][None, :]
    cos = jnp.cos(ang)
    sin = jnp.sin(ang)
    c = jnp.where(d < ROPE_DIMS, cos, 1.0)
    s1 = jnp.where(d < half, -sin, 0.0)
    s2 = jnp.where((d >= half) & (d < ROPE_DIMS), sin, 0.0)
    return c, s1, s2


def _mixer_layer(x, xb, rope, B, S, w_in, conv_w, conv_b, ga_w, ga_b, gx_w, gx_b, lam,
                 mu, w0, w_up, a0, a_up, g_up, k_k, k_a, r_k, lnx_g, lnx_b, w_out, ln_g, ln_b):
    W = RWKV_WIDTH
    row = lambda t: t.reshape(1, -1)
    proj = _in_proj(xb, w_in.astype(BF16))
    q, k, vt, pen = _attn_prep(proj, *rope, B, S)
    att = _attention(q, k, vt, pen, B, S)
    lru = _rglru(proj, conv_w, row(conv_b), _block_diag(ga_w), row(ga_b), _block_diag(gx_w), row(gx_b), row(lam), B, S)
    head_ones = _block_diag(jnp.ones((RWKV_HEADS, HEAD_DIM, HEAD_DIM), F32))
    wlr = jnp.zeros((W, 3 * W), F32)
    wlr = wlr.at[0:64, 0:W].set(w_up).at[64:128, W:2 * W].set(a_up).at[128:256, 2 * W:3 * W].set(g_up)
    r, ld, k2, v, kn, a, g = _rwkv_prep(proj, row(mu), wlr, row(w0), row(a0), row(k_k), row(k_a), head_ones, B, S)
    y = _wkv(r, ld, k2, v, kn, a, B, S)
    rwk = _wkv_post(y, r, k2, v, g, row(r_k), row(lnx_g), row(lnx_b), head_ones.astype(BF16))
    wo = w_out.astype(BF16)
    return _outproj_ln(att, lru, rwk, x, wo[0:ATT_WIDTH], wo[ATT_WIDTH:ATT_WIDTH + LRU_WIDTH],
                       wo[ATT_WIDTH + LRU_WIDTH:], row(ln_g), row(ln_b))


def kernel(x, positions, w_in, lru_conv_w, lru_conv_b, lru_ga_w, lru_ga_b, lru_gx_w, lru_gx_b, lru_lambda, rwkv_mu, rwkv_w0, rwkv_w_up, rwkv_a0, rwkv_a_up, rwkv_g_up, rwkv_k_k, rwkv_k_a, rwkv_r_k, rwkv_lnx_g, rwkv_lnx_b, w_out, ln1_g, ln1_b, ffn_w_gate, ffn_w_up, ffn_w_down, moe_router, moe_w_gate, moe_w_up, moe_w_down, ln2_g, ln2_b):
    B, S, D = x.shape
    T = B * S
    rope = _rope_tables(positions)
    xf = x.reshape(T, D)
    xb = xf.astype(BF16)
    for l in range(DEPTH):
        xf, xb = _mixer_layer(xf, xb, rope, B, S, w_in[l], lru_conv_w[l], lru_conv_b[l], lru_ga_w[l], lru_ga_b[l],
                              lru_gx_w[l], lru_gx_b[l], lru_lambda[l], rwkv_mu[l], rwkv_w0[l], rwkv_w_up[l],
                              rwkv_a0[l], rwkv_a_up[l], rwkv_g_up[l], rwkv_k_k[l], rwkv_k_a[l], rwkv_r_k[l],
                              rwkv_lnx_g[l], rwkv_lnx_b[l], w_out[l], ln1_g[l], ln1_b[l])
        g2 = ln2_g[l].reshape(1, D)
        b2 = ln2_b[l].reshape(1, D)
        if l % 2 == 0:
            i = l // 2
            xf, xb = _ffn_ln(xb, xf, ffn_w_gate[i].astype(BF16), ffn_w_up[i].astype(BF16),
                             ffn_w_down[i].astype(BF16), g2, b2)
        else:
            i = l // 2
            idx = jnp.arange(MOE_TILE)
            tri = (idx[:, None] <= idx[None, :]).astype(BF16)
            gates_t, pos_t, cnt = _router(xf, moe_router[i].T, tri)
            counts = cnt[:, :, 0].astype(jnp.int32).reshape(-1)
            f = _moe(xb, gates_t, pos_t, counts, moe_w_gate[i].astype(BF16), moe_w_up[i].astype(BF16),
                     moe_w_down[i].astype(BF16))
            xf = _resid_ln(xf, f, g2, b2)
            xb = xf.astype(BF16)
    return xf.reshape(B, S, D)
```

```python
import functools

import jax
import jax.numpy as jnp
from jax import lax
from jax.experimental import pallas as pl
from jax.experimental.pallas import tpu as pltpu

F32 = jnp.float32
BF16 = jnp.bfloat16
HI = lax.Precision.HIGHEST
NT_DIMS = (((1,), (1,)), ((), ()))
TN_DIMS = (((0,), (0,)), ((), ()))

HEAD_DIM = 64
ATT_HEADS = 8
ATT_WIDTH = ATT_HEADS * HEAD_DIM
LRU_WIDTH = 256
RWKV_HEADS = 4
RWKV_WIDTH = 256
ROPE_DIMS = 16
ROPE_THETA = 500000.0
MOBA_BLOCK = 256
MOBA_TOPK = 3
CONV_WIDTH = 4
LRU_C = 8.0
RWKV_LN_EPS = 64e-5
N_EXPERTS = 8
LN_EPS = 1e-5
DEPTH = 2
DEEPNORM_ALPHA = (2 * DEPTH) ** 0.25
NEG_INF = -1e30
LOG2_E = 1.4426950408889634

LANES = 128
VMEM_LIMIT = 56 * 1024 * 1024

ATTN_STREAMS = 4
WKV_CHUNK = 64
WKV_GROUP = 8
LRU_CHUNK = 256
RWKV_PREP_ROWS = 512
MOE_TILE = 1024
MOE_ROWS = 256


def _params(*sem):
    return pltpu.CompilerParams(dimension_semantics=sem, vmem_limit_bytes=VMEM_LIMIT)


def _sigmoid(x):
    return 1.0 / (1.0 + jnp.exp(-x))


def _softplus(x):
    return jnp.maximum(x, 0.0) + jnp.log1p(jnp.exp(-jnp.abs(x)))


def _expm1(z):
    u = jnp.exp(z)
    um1 = u - 1.0
    return jnp.where(u == 1.0, z, jnp.where(um1 == -1.0, -1.0, um1 * z / jnp.log(u)))


def _layer_norm(y, g, b):
    m = jnp.mean(y, axis=-1, keepdims=True)
    d = y - m
    var = jnp.mean(d * d, axis=-1, keepdims=True)
    return d * lax.rsqrt(var + LN_EPS) * g + b


def _matmul_kernel(x_ref, w_ref, o_ref):
    o_ref[...] = jnp.dot(x_ref[...], w_ref[...], preferred_element_type=F32)


def _in_proj(xb, w):
    T, D = xb.shape
    N = w.shape[1]
    tm, tn = 512, 1024
    return pl.pallas_call(
        _matmul_kernel,
        grid=(N // tn, T // tm),
        in_specs=[pl.BlockSpec((tm, D), lambda n, m: (m, 0)),
                  pl.BlockSpec((D, tn), lambda n, m: (0, n))],
        out_specs=pl.BlockSpec((tm, tn), lambda n, m: (m, n)),
        out_shape=jax.ShapeDtypeStruct((T, N), F32),
        compiler_params=_params("arbitrary", "arbitrary"),
        name="in_proj",
    )(xb, w)


def _attn_prep_kernel(proj_ref, c_ref, s1_ref, s2_ref, q_ref, k_ref, vt_ref, pen_ref, km_ref, *, nb):
    i = pl.program_id(1)

    @pl.when(i == 0)
    def _():
        km_ref[...] = jnp.zeros_like(km_ref)

    c = c_ref[...]
    s1 = s1_ref[...]
    s2 = s2_ref[...]

    def rope(xt):
        return xt * c + pltpu.roll(xt, LANES - ROPE_DIMS // 2, 1) * s1 + pltpu.roll(xt, ROPE_DIMS // 2, 1) * s2

    q_tiles = []
    for ct in range(ATT_WIDTH // LANES):
        lo, hi = ct * LANES, (ct + 1) * LANES
        qr = rope(proj_ref[:, lo:hi])
        q_tiles.append(qr)
        q_ref[:, lo:hi] = (qr * (HEAD_DIM ** -0.5 * LOG2_E)).astype(BF16)
        kr = rope(proj_ref[:, ATT_WIDTH + lo:ATT_WIDTH + hi])
        k_ref[0, 0, :, lo:hi] = kr.astype(BF16)
        km_row = lax.broadcasted_iota(jnp.int32, (nb, LANES), 0)
        km_ref[:, lo:hi] = jnp.where(km_row == i, jnp.mean(kr, axis=0, keepdims=True), km_ref[:, lo:hi])
    vt_ref[0, 0] = proj_ref[:, 2 * ATT_WIDTH:3 * ATT_WIDTH].T.astype(BF16)

    q_rot = jnp.concatenate(q_tiles, axis=1)
    km = km_ref[...]
    km_rows = jnp.concatenate([km] * ATT_HEADS, axis=0)
    row_head = lax.broadcasted_iota(jnp.int32, km_rows.shape, 0) // nb
    lane_head = lax.broadcasted_iota(jnp.int32, km_rows.shape, 1) // HEAD_DIM
    g_all = lax.dot_general(jnp.where(row_head == lane_head, km_rows, 0.0), q_rot, NT_DIMS,
                            precision=HI, preferred_element_type=F32)
    n_iota = lax.broadcasted_iota(jnp.int32, (nb, MOBA_BLOCK), 0)
    past = n_iota < i
    for h in range(ATT_HEADS):
        g = g_all[h * nb:(h + 1) * nb, :]
        rank = jnp.zeros((nb, MOBA_BLOCK), F32)
        for m in range(nb):
            gm = g[m:m + 1, :]
            beats = jnp.where(gm > g, 1.0, jnp.where(gm == g, jnp.where(n_iota > m, 1.0, 0.0), 0.0))
            rank = rank + jnp.where(m < i, beats, 0.0)
        pen = jnp.where(past, jnp.where(rank < float(MOBA_TOPK), 0.0, NEG_INF), NEG_INF)
        half = (h % 2) * MOBA_BLOCK
        pen_ref[0, h // 2, :, half:half + MOBA_BLOCK] = pen


def _attn_prep(proj, rope_c, rope_s1, rope_s2, B, S):
    T = B * S
    nb = S // MOBA_BLOCK
    blk = MOBA_BLOCK
    return pl.pallas_call(
        functools.partial(_attn_prep_kernel, nb=nb),
        grid=(B, nb),
        in_specs=[pl.BlockSpec((blk, 3 * ATT_WIDTH), lambda b, i: (b * nb + i, 0)),
                  pl.BlockSpec((blk, LANES), lambda b, i: (b * nb + i, 0)),
                  pl.BlockSpec((blk, LANES), lambda b, i: (b * nb + i, 0)),
                  pl.BlockSpec((blk, LANES), lambda b, i: (b * nb + i, 0))],
        out_specs=[pl.BlockSpec((blk, ATT_WIDTH), lambda b, i: (b * nb + i, 0)),
                   pl.BlockSpec((1, 1, blk, ATT_WIDTH), lambda b, i: (b, i, 0, 0)),
                   pl.BlockSpec((1, 1, ATT_WIDTH, blk), lambda b, i: (b, i, 0, 0)),
                   pl.BlockSpec((1, ATT_HEADS // 2, nb, 2 * blk), lambda b, i: (b, 0, 0, i))],
        out_shape=[jax.ShapeDtypeStruct((T, ATT_WIDTH), BF16),
                   jax.ShapeDtypeStruct((B, nb, blk, ATT_WIDTH), BF16),
                   jax.ShapeDtypeStruct((B, nb, ATT_WIDTH, blk), BF16),
                   jax.ShapeDtypeStruct((B, ATT_HEADS // 2, nb, 2 * S), F32)],
        scratch_shapes=[pltpu.VMEM((nb, ATT_WIDTH), F32)],
        compiler_params=_params("arbitrary", "arbitrary"),
        name="attn_prep",
    )(proj, rope_c, rope_s1, rope_s2)


def _attn_kernel(q_ref, k_ref, vt_ref, pen_ref, o_ref):
    i = pl.program_id(2)
    blk = MOBA_BLOCK
    streams = range(ATTN_STREAMS)
    each = lambda fn, *xs: [fn(*(x[t] for x in xs)) for t in streams]
    lanes = lambda t: slice(t * LANES, (t + 1) * LANES)
    lane_head = lax.broadcasted_iota(jnp.int32, (blk, LANES), 1) // HEAD_DIM
    kidx = lax.broadcasted_iota(jnp.int32, (blk, 2 * blk), 0)
    qidx = lax.broadcasted_iota(jnp.int32, (blk, 2 * blk), 1) % blk
    nt = lambda a, b: lax.dot_general(a, b, NT_DIMS, preferred_element_type=F32)
    nn = lambda a, b: jnp.dot(a, b, preferred_element_type=F32)
    col_max = lambda x: jnp.max(x, axis=0, keepdims=True)
    col_sum = lambda x: jnp.sum(x, axis=0, keepdims=True)

    def stacked_q(t):
        q = q_ref[:, lanes(t)]
        zero = jnp.zeros_like(q)
        return jnp.concatenate([jnp.where(lane_head == 0, q, zero), jnp.where(lane_head == 1, q, zero)], axis=0)

    qs = [stacked_q(t) for t in streams]
    s = [jnp.where(kidx <= qidx, nt(k_ref[0, i, :, lanes(t)], qs[t]), NEG_INF) for t in streams]
    m0 = each(col_max, s)
    p = each(lambda x, m: jnp.exp2(x - m), s, m0)
    l0 = each(col_sum, p)
    acc0 = [nn(vt_ref[0, i, lanes(t), :], p[t].astype(BF16)) for t in streams]

    def body(jj, carry):
        m, l, acc = carry
        j0 = 2 * jj
        j1 = j0 + 1
        s0 = [nt(k_ref[0, j0, :, lanes(t)], qs[t]) + pen_ref[0, t, pl.ds(j0, 1), :] for t in streams]
        s1 = [nt(k_ref[0, j1, :, lanes(t)], qs[t]) + pen_ref[0, t, pl.ds(j1, 1), :] for t in streams]
        m_new = each(lambda mo, a, b: jnp.maximum(mo, jnp.maximum(col_max(a), col_max(b))), m, s0, s1)
        alpha = each(lambda mo, mn: jnp.exp2(mo - mn), m, m_new)
        p0 = each(lambda a, mn: jnp.exp2(a - mn), s0, m_new)
        p1 = each(lambda a, mn: jnp.exp2(a - mn), s1, m_new)
        l = each(lambda al, lo, a, b: al * lo + col_sum(a) + col_sum(b), alpha, l, p0, p1)
        pv0 = [nn(vt_ref[0, j0, lanes(t), :], p0[t].astype(BF16)) for t in streams]
        pv1 = [nn(vt_ref[0, j1, lanes(t), :], p1[t].astype(BF16)) for t in streams]
        acc = each(lambda ac, al, a, b: ac * al + a + b, acc, alpha, pv0, pv1)
        return tuple(m_new), tuple(l), tuple(acc)

    _, l, acc = lax.fori_loop(0, (i + 1) // 2, body, (tuple(m0), tuple(l0), tuple(acc0)))
    row = lax.broadcasted_iota(jnp.int32, (LANES, blk), 0)
    for t in streams:
        out = acc[t] / l[t]
        o_ref[:, lanes(t)] = jnp.where(row < HEAD_DIM, out[:, 0:blk], out[:, blk:2 * blk]).T


def _attention(q, k, vt, pen, B, S):
    T = B * S
    nb = S // MOBA_BLOCK
    blk = MOBA_BLOCK
    w = ATTN_STREAMS * LANES
    n_groups = ATT_WIDTH // w
    return pl.pallas_call(
        _attn_kernel,
        grid=(B, n_groups, nb),
        in_specs=[pl.BlockSpec((blk, w), lambda b, hp, i: (b * nb + i, hp)),
                  pl.BlockSpec((1, nb, blk, w), lambda b, hp, i: (b, 0, 0, hp)),
                  pl.BlockSpec((1, nb, w, blk), lambda b, hp, i: (b, 0, hp, 0)),
                  pl.BlockSpec((1, ATTN_STREAMS, nb, 2 * blk), lambda b, hp, i: (b, hp, 0, i))],
        out_specs=pl.BlockSpec((blk, w), lambda b, hp, i: (b * nb + i, hp)),
        out_shape=jax.ShapeDtypeStruct((T, ATT_WIDTH), F32),
        compiler_params=_params("arbitrary", "arbitrary", "arbitrary"),
        name="moba_attn",
    )(q, k, vt, pen)


def _lru_kernel(p_ref, cw_ref, cb_ref, ga_ref, gab_ref, gx_ref, gxb_ref, lam_ref, o_ref, xbuf, hc):
    t = pl.program_id(1)
    tc = LRU_CHUNK
    pad = 8

    @pl.when(t == 0)
    def _():
        xbuf[0:pad, :] = jnp.zeros((pad, LRU_WIDTH), F32)
        hc[...] = jnp.zeros_like(hc)

    @pl.when(t > 0)
    def _():
        xbuf[0:pad, :] = xbuf[tc:tc + pad, :]

    x = p_ref[:, 0:LRU_WIDTH]
    gate = p_ref[:, LRU_WIDTH:2 * LRU_WIDTH]
    xbuf[pad:pad + tc, :] = x
    xc = cb_ref[...] + x * cw_ref[CONV_WIDTH - 1:CONV_WIDTH, :]
    for j in range(CONV_WIDTH - 1):
        back = CONV_WIDTH - 1 - j
        xc = xc + xbuf[pad - back:pad - back + tc, :] * cw_ref[j:j + 1, :]

    r = _sigmoid(jnp.dot(xc, ga_ref[...], precision=HI, preferred_element_type=F32) + gab_ref[...])
    ig = _sigmoid(jnp.dot(xc, gx_ref[...], precision=HI, preferred_element_type=F32) + gxb_ref[...])
    log_a = -LRU_C * r * _softplus(-lam_ref[...])
    a = jnp.exp(log_a)
    u = jnp.sqrt(-_expm1(2.0 * log_a)) * (ig * xc)

    rows = lax.broadcasted_iota(jnp.int32, (tc, LRU_WIDTH), 0)
    d = 1
    while d < tc:
        keep = rows >= d
        a_prev = jnp.where(keep, pltpu.roll(a, d, 0), 1.0)
        u_prev = jnp.where(keep, pltpu.roll(u, d, 0), 0.0)
        u = a * u_prev + u
        a = a * a_prev
        d *= 2
    h = u + a * hc[...]
    hc[...] = h[tc - 1:tc, :]
    gl = 0.5 * gate * (1.0 + jnp.tanh(0.7978845608028654 * (gate + 0.044715 * gate * gate * gate)))
    o_ref[...] = h * gl


def _rglru(proj, cw, cb, ga, gab, gx, gxb, lam, B, S):
    T = B * S
    tc = LRU_CHUNK
    nt = S // tc
    col = (3 * ATT_WIDTH) // (2 * LRU_WIDTH)
    vec = lambda: pl.BlockSpec((1, LRU_WIDTH), lambda b, t: (0, 0))
    mat = lambda: pl.BlockSpec((LRU_WIDTH, LRU_WIDTH), lambda b, t: (0, 0))
    return pl.pallas_call(
        _lru_kernel,
        grid=(B, nt),
        in_specs=[pl.BlockSpec((tc, 2 * LRU_WIDTH), lambda b, t: (b * nt + t, col)),
                  pl.BlockSpec((CONV_WIDTH, LRU_WIDTH), lambda b, t: (0, 0)),
                  vec(), mat(), vec(), mat(), vec(), vec()],
        out_specs=pl.BlockSpec((tc, LRU_WIDTH), lambda b, t: (b * nt + t, 0)),
        out_shape=jax.ShapeDtypeStruct((T, LRU_WIDTH), F32),
        scratch_shapes=[pltpu.VMEM((tc + 8, LRU_WIDTH), F32), pltpu.VMEM((1, LRU_WIDTH), F32)],
        compiler_params=_params("arbitrary", "arbitrary"),
        name="rglru",
    )(proj, cw, cb, ga, gab, gx, gxb, lam)


def _rwkv_prep_kernel(p_ref, pp_ref, mu_ref, wlr_ref, w0_ref, a0_ref, kk_ref, ka_ref, ones_ref,
                      r_ref, ld_ref, k2_ref, v_ref, kn_ref, a_ref, g_ref):
    i = pl.program_id(1)
    W = RWKV_WIDTH
    p = p_ref[...]
    rows = lax.broadcasted_iota(jnp.int32, p.shape, 0)
    prev_last = jnp.where(i == 0, 0.0, pp_ref[7:8, :])
    p_prev = jnp.where(rows == 0, prev_last, pltpu.roll(p, 1, 0))
    pf = p + (p_prev - p) * mu_ref[...]
    r = pf[:, 0:W]
    k = pf[:, W:2 * W]
    v = pf[:, 2 * W:3 * W]
    z = pf[:, 3 * W:4 * W]
    lane = lax.broadcasted_iota(jnp.int32, z.shape, 1)
    zz = jnp.where(lane < 64, jnp.tanh(z), jnp.where(lane < 128, z, _sigmoid(z)))
    lr = jnp.dot(zz, wlr_ref[...], precision=HI, preferred_element_type=F32)
    w = -_softplus(-(w0_ref[...] + lr[:, 0:W])) - 0.5
    a = _sigmoid(a0_ref[...] + lr[:, W:2 * W])
    kk = k * kk_ref[...]
    ssq = jnp.dot(kk * kk, ones_ref[...], precision=HI, preferred_element_type=F32)
    kk = kk / jnp.maximum(jnp.sqrt(ssq), 1e-12)
    r_ref[...] = r
    ld_ref[...] = -jnp.exp(w)
    k2_ref[...] = k * (1.0 + (a - 1.0) * ka_ref[...])
    v_ref[...] = v
    kn_ref[...] = kk
    a_ref[...] = a
    g_ref[...] = lr[:, 2 * W:3 * W]


def _rwkv_prep(proj, mu, wlr, w0, a0, k_k, k_a, head_ones, B, S):
    T = B * S
    tr = RWKV_PREP_ROWS
    nt = S // tr
    W = RWKV_WIDTH
    col = 2
    vec = lambda: pl.BlockSpec((1, W), lambda b, t: (0, 0))
    out = lambda: pl.BlockSpec((tr, W), lambda b, t: (b * nt + t, 0))
    return pl.pallas_call(
        _rwkv_prep_kernel,
        grid=(B, nt),
        in_specs=[pl.BlockSpec((tr, 4 * W), lambda b, t: (b * nt + t, col)),
                  pl.BlockSpec((8, 4 * W), lambda b, t: (jnp.maximum((b * nt + t) * (tr // 8) - 1, 0), col)),
                  pl.BlockSpec((1, 4 * W), lambda b, t: (0, 0)),
                  pl.BlockSpec((W, 3 * W), lambda b, t: (0, 0)),
                  vec(), vec(), vec(), vec(),
                  pl.BlockSpec((W, W), lambda b, t: (0, 0))],
        out_specs=[out() for _ in range(7)],
        out_shape=[jax.ShapeDtypeStruct((T, W), F32) for _ in range(7)],
        compiler_params=_params("arbitrary", "arbitrary"),
        name="rwkv_prep",
    )(proj, proj, mu, wlr, w0, a0, k_k, k_a, head_ones)


def _wkv_kernel(r_ref, ld_ref, k2_ref, v_ref, kn_ref, a_ref, y_ref, st_ref):
    c = pl.program_id(1)
    C = WKV_CHUNK
    W = RWKV_WIDTH
    H = RWKV_HEADS

    @pl.when(c == 0)
    def _():
        st_ref[...] = jnp.zeros_like(st_ref)

    def mm(x, y):
        return jnp.dot(x.astype(BF16), y.astype(BF16), preferred_element_type=F32)

    def mm_nt(x, y):
        return lax.dot_general(x.astype(BF16), y.astype(BF16), NT_DIMS, preferred_element_type=F32)

    def mm_tn(x, y):
        return lax.dot_general(x.astype(BF16), y.astype(BF16), TN_DIMS, preferred_element_type=F32)

    tr = lax.broadcasted_iota(jnp.int32, (C, C), 0)
    tc = lax.broadcasted_iota(jnp.int32, (C, C), 1)
    tri = jnp.where(tr >= tc, 1.0, 0.0)
    row = lax.broadcasted_iota(jnp.int32, (W, W), 0)
    colm = lax.broadcasted_iota(jnp.int32, (W, W), 1)
    same_head = (row // C) == (colm // HEAD_DIM)
    strict = lambda x: jnp.where(same_head, jnp.where(row > colm, x, 0.0), 0.0)
    incl = lambda x: jnp.where(same_head, jnp.where(row >= colm, x, 0.0), 0.0)
    same16 = (row // 16) == (colm // 16)
    eye = jnp.where(row == colm, 1.0, 0.0)

    def stack(x):
        return jnp.where(same_head, jnp.concatenate([x] * H, axis=0), 0.0)

    G = range(WKV_GROUP)
    each = lambda fn, *xs: [fn(*(x[g] for x in xs)) for g in G]

    r = [r_ref[g] for g in G]
    ld = [ld_ref[g] for g in G]
    k2 = [k2_ref[g] for g in G]
    v_s = [stack(v_ref[g]) for g in G]
    kn = [kn_ref[g] for g in G]
    b_s = [kn[g] * a_ref[g] for g in G]
    cl = each(lambda x: jnp.dot(tri, x, precision=HI, preferred_element_type=F32), ld)
    cl_end = [x[C - 1:C, :] for x in cl]
    e_neg = [jnp.exp(-x) for x in cl]
    e_end = each(lambda ce, x: jnp.exp(ce - x), cl_end, cl)
    ar_s = [jnp.concatenate([stack(-kn[g] * jnp.exp(cl[g] - ld[g])), stack(r[g] * jnp.exp(cl[g]))], axis=0) for g in G]
    bk_s = [jnp.concatenate([stack(b_s[g] * e_neg[g]), stack(k2[g] * e_neg[g])], axis=0) for g in G]
    end_s = [jnp.concatenate([stack(b_s[g] * e_end[g]), stack(k2[g] * e_end[g])], axis=0) for g in G]

    prod = each(mm_nt, ar_s, bk_s)
    n_ab = [strict(p[0:W, 0:W]) for p in prod]
    a_ak = [strict(p[0:W, W:2 * W]) for p in prod]
    a_rb = [incl(p[W:2 * W, 0:W]) for p in prod]
    a_rk = [incl(p[W:2 * W, W:2 * W]) for p in prod]

    nd = [jnp.where(same16, n, 0.0) for n in n_ab]
    lo = each(lambda n, d: n - d, n_ab, nd)
    n2 = each(mm, nd, nd)
    p1 = [eye + d for d in nd]
    n4 = each(mm, n2, n2)
    p2 = each(lambda p, n: p + mm(p, n), p1, n2)
    n8 = each(mm, n4, n4)
    p3 = each(lambda p, n: p + mm(p, n), p2, n4)
    dinv = each(lambda p, n: p + mm(p, n), p3, n8)
    x1 = each(mm, dinv, lo)
    x2 = each(mm, x1, x1)
    y1 = [eye + x for x in x1]
    m_inv = each(lambda y, x: y + mm(y, x), y1, x2)
    tinv = each(mm, m_inv, dinv)

    st = [st_ref[g] for g in G]
    ar_st = each(mm_nt, ar_s, st)
    akv = each(mm, a_ak, v_s)
    u_s = each(lambda t, a, b: mm(t, a[0:W] + b), tinv, ar_st, akv)
    rkv = each(mm, a_rk, v_s)
    y_s = each(lambda a, m, u, b: a[W:2 * W] + mm(m, u) + b, ar_st, a_rb, u_s, rkv)
    upd = each(lambda u, v, e: mm_tn(jnp.concatenate([u, v], axis=0), e), u_s, v_s, end_s)
    for g in G:
        y = y_s[g][0:C, :]
        for h in range(1, H):
            y = y + y_s[g][h * C:(h + 1) * C, :]
        y_ref[g] = y
        st_ref[g] = st[g] * jnp.exp(cl_end[g]) + upd[g]


def _wkv(r, ld, k2, v, kn, a, B, S):
    C = WKV_CHUNK
    nc = S // C
    W = RWKV_WIDTH
    G = WKV_GROUP
    blk = lambda: pl.BlockSpec((G, C, W), lambda b, c: (b, c, 0))
    seq = lambda t: t.reshape(B, S, W)
    y = pl.pallas_call(
        _wkv_kernel,
        grid=(B // G, nc),
        in_specs=[blk() for _ in range(6)],
        out_specs=blk(),
        out_shape=jax.ShapeDtypeStruct((B, S, W), F32),
        scratch_shapes=[pltpu.VMEM((G, W, W), F32)],
        compiler_params=_params("arbitrary", "arbitrary"),
        name="wkv7",
    )(seq(r), seq(ld), seq(k2), seq(v), seq(kn), seq(a))
    return y.reshape(B * S, W)


def _wkv_post_kernel(y_ref, r_ref, k2_ref, v_ref, g_ref, rk_ref, lg_ref, lb_ref, ones_ref, o_ref):
    ones = ones_ref[...]

    def head_sum(x):
        hi = x.astype(BF16)
        lo = (x - hi.astype(F32)).astype(BF16)
        return (jnp.dot(hi, ones, preferred_element_type=F32) + jnp.dot(lo, ones, preferred_element_type=F32))

    y = y_ref[...]
    r = r_ref[...]
    k2 = k2_ref[...]
    inv_n = 1.0 / HEAD_DIM
    d = y - head_sum(y) * inv_n
    var = head_sum(d * d) * inv_n
    yn = d * lax.rsqrt(var + RWKV_LN_EPS) * lg_ref[...] + lb_ref[...]
    bonus = head_sum(r * k2 * rk_ref[...]) * v_ref[...]
    o_ref[...] = (yn + bonus) * g_ref[...]


def _wkv_post(y, r, k2, v, g, r_k, lnx_g, lnx_b, head_ones_bf16):
    T, W = y.shape
    tm = 512
    blk = lambda: pl.BlockSpec((tm, W), lambda m: (m, 0))
    vec = lambda: pl.BlockSpec((1, W), lambda m: (0, 0))
    return pl.pallas_call(
        _wkv_post_kernel,
        grid=(T // tm,),
        in_specs=[blk() for _ in range(5)] + [vec(), vec(), vec(), pl.BlockSpec((W, W), lambda m: (0, 0))],
        out_specs=blk(),
        out_shape=jax.ShapeDtypeStruct((T, W), F32),
        compiler_params=_params("arbitrary"),
        name="wkv_post",
    )(y, r, k2, v, g, r_k, lnx_g, lnx_b, head_ones_bf16)


def _outproj_ln_kernel(att_ref, lru_ref, rwk_ref, x_ref, wa_ref, wl_ref, wr_ref, g_ref, b_ref, o_ref, ob_ref):
    h = jnp.dot(att_ref[...].astype(BF16), wa_ref[...], preferred_element_type=F32)
    h = h + jnp.dot(lru_ref[...].astype(BF16), wl_ref[...], preferred_element_type=F32)
    h = h + jnp.dot(rwk_ref[...].astype(BF16), wr_ref[...], preferred_element_type=F32)
    y = _layer_norm(DEEPNORM_ALPHA * x_ref[...] + h, g_ref[...], b_ref[...])
    o_ref[...] = y
    ob_ref[...] = y.astype(BF16)


def _outproj_ln(att, lru, rwk, x, wa, wl, wr, g, b):
    T, D = x.shape
    tm = 512
    row = lambda w: pl.BlockSpec((tm, w), lambda m: (m, 0))
    full = lambda a: pl.BlockSpec(a.shape, lambda m: (0, 0))
    return pl.pallas_call(
        _outproj_ln_kernel,
        grid=(T // tm,),
        in_specs=[row(ATT_WIDTH), row(LRU_WIDTH), row(RWKV_WIDTH), row(D), full(wa), full(wl), full(wr), full(g), full(b)],
        out_specs=[row(D), row(D)],
        out_shape=[jax.ShapeDtypeStruct((T, D), F32), jax.ShapeDtypeStruct((T, D), BF16)],
        compiler_params=_params("arbitrary"),
        name="outproj_ln",
    )(att, lru, rwk, x, wa, wl, wr, g, b)


def _ffn_kernel(xb_ref, x_ref, wg_ref, wu_ref, wd_ref, g_ref, b_ref, o_ref, ob_ref, acc_ref, *, nf):
    f = pl.program_id(1)
    xb = xb_ref[...]
    hg = jnp.dot(xb, wg_ref[...], preferred_element_type=F32)
    hu = jnp.dot(xb, wu_ref[...], preferred_element_type=F32)
    hh = (hg * _sigmoid(hg) * hu).astype(BF16)
    contrib = jnp.dot(hh, wd_ref[...], preferred_element_type=F32)

    @pl.when(f == 0)
    def _():
        acc_ref[...] = contrib

    @pl.when(f > 0)
    def _():
        acc_ref[...] = acc_ref[...] + contrib

    @pl.when(f == nf - 1)
    def _():
        y = _layer_norm(DEEPNORM_ALPHA * x_ref[...] + acc_ref[...], g_ref[...], b_ref[...])
        o_ref[...] = y
        ob_ref[...] = y.astype(BF16)


def _ffn_ln(xb, x, wg, wu, wd, g, b):
    T, D = x.shape
    F = wg.shape[1]
    tm = 512
    nf = 2
    tf = F // nf
    return pl.pallas_call(
        functools.partial(_ffn_kernel, nf=nf),
        grid=(T // tm, nf),
        in_specs=[pl.BlockSpec((tm, D), lambda m, f: (m, 0)),
                  pl.BlockSpec((tm, D), lambda m, f: (m, 0)),
                  pl.BlockSpec((D, tf), lambda m, f: (0, f)),
                  pl.BlockSpec((D, tf), lambda m, f: (0, f)),
                  pl.BlockSpec((tf, D), lambda m, f: (f, 0)),
                  pl.BlockSpec((1, D), lambda m, f: (0, 0)),
                  pl.BlockSpec((1, D), lambda m, f: (0, 0))],
        out_specs=[pl.BlockSpec((tm, D), lambda m, f: (m, 0)), pl.BlockSpec((tm, D), lambda m, f: (m, 0))],
        out_shape=[jax.ShapeDtypeStruct((T, D), F32), jax.ShapeDtypeStruct((T, D), BF16)],
        scratch_shapes=[pltpu.VMEM((tm, D), F32)],
        compiler_params=_params("arbitrary", "arbitrary"),
        name="ffn_ln",
    )(xb, x, wg, wu, wd, g, b)


def _router_kernel(x_ref, wrt_ref, tri_ref, g_ref, pos_ref, cnt_ref):
    tt = x_ref.shape[0]
    logits = lax.dot_general(wrt_ref[...], x_ref[...], NT_DIMS, precision=HI, preferred_element_type=F32)
    row = lax.broadcasted_iota(jnp.int32, logits.shape, 0).astype(F32)
    m1 = jnp.max(logits, axis=0, keepdims=True)
    i1 = jnp.min(jnp.where(logits == m1, row, float(N_EXPERTS)), axis=0, keepdims=True)
    rest = jnp.where(row == i1, -jnp.inf, logits)
    m2 = jnp.max(rest, axis=0, keepdims=True)
    i2 = jnp.min(jnp.where(rest == m2, row, float(N_EXPERTS)), axis=0, keepdims=True)
    e = jnp.exp(m2 - m1)
    g1 = 1.0 / (1.0 + e)
    g2 = e / (1.0 + e)
    g_ref[0] = jnp.where(row == i1, g1, jnp.where(row == i2, g2, 0.0))
    ind = jnp.where(row == i1, 1.0, jnp.where(row == i2, 1.0, 0.0))
    csum = jnp.dot(ind.astype(BF16), tri_ref[...], preferred_element_type=F32)
    pos_ref[0] = jnp.where(ind > 0.5, csum - 1.0, -1.0)
    cnt_ref[0] = jnp.broadcast_to(csum[:, tt - 1:tt], (N_EXPERTS, LANES))


def _router(x, wrt, tri):
    T, D = x.shape
    tt = MOE_TILE
    nt = T // tt
    tile = lambda: pl.BlockSpec((1, N_EXPERTS, tt), lambda m: (m, 0, 0))
    return pl.pallas_call(
        _router_kernel,
        grid=(nt,),
        in_specs=[pl.BlockSpec((tt, D), lambda m: (m, 0)), pl.BlockSpec((N_EXPERTS, D), lambda m: (0, 0)),
                  pl.BlockSpec((tt, tt), lambda m: (0, 0))],
        out_specs=[tile(), tile(), pl.BlockSpec((1, N_EXPERTS, LANES), lambda m: (m, 0, 0))],
        out_shape=[jax.ShapeDtypeStruct((nt, N_EXPERTS, tt), F32), jax.ShapeDtypeStruct((nt, N_EXPERTS, tt), F32),
                   jax.ShapeDtypeStruct((nt, N_EXPERTS, LANES), F32)],
        compiler_params=_params("arbitrary"),
        name="moe_router",
    )(x, wrt, tri)


def _moe_kernel(cnt_ref, xb_ref, gate_ref, pos_ref, wg_ref, wu_ref, wd_ref, o_ref, xg_ref, yacc_ref, gcol_ref, *, nf):
    t = pl.program_id(0)
    e = pl.program_id(1)
    f = pl.program_id(2)
    tt = xb_ref.shape[0]
    R = MOE_ROWS
    half = R // 2
    n = cnt_ref[t * N_EXPERTS + e]
    rem = n % R
    n_full = n // R + jnp.where(rem > half, 1, 0)
    has_tail = jnp.logical_and(rem > 0, rem <= half)
    tail_start = pl.multiple_of(n_full * R, half)
    pos_row = pos_ref[0, pl.ds(e, 1), :]

    def selection(start, rows):
        slot = lax.broadcasted_iota(jnp.int32, (rows, tt), 0) + start
        return jnp.where(pos_row == slot.astype(F32), 1.0, 0.0)

    def gather(start, rows):
        sel = selection(start, rows)
        dst = pl.ds(start, rows)
        xg_ref[dst, :] = jnp.dot(sel.astype(BF16), xb_ref[...], preferred_element_type=F32).astype(BF16)
        gate_row = gate_ref[0, pl.ds(e, 1), :]
        gcol_ref[dst, :] = jnp.broadcast_to(jnp.sum(sel * gate_row, axis=1, keepdims=True), (rows, LANES))

    def expert(start, rows):
        dst = pl.ds(start, rows)
        xc = xg_ref[dst, :]
        hg = jnp.dot(xc, wg_ref[0], preferred_element_type=F32)
        hu = jnp.dot(xc, wu_ref[0], preferred_element_type=F32)
        contrib = jnp.dot((hg * _sigmoid(hg) * hu).astype(BF16), wd_ref[0], preferred_element_type=F32)

        @pl.when(f == 0)
        def _():
            yacc_ref[dst, :] = contrib

        @pl.when(f > 0)
        def _():
            yacc_ref[dst, :] = yacc_ref[dst, :] + contrib

        @pl.when(f == nf - 1)
        def _():
            gate = jnp.concatenate([gcol_ref[dst, :]] * (o_ref.shape[1] // LANES), axis=1)
            ys = (yacc_ref[dst, :] * gate).astype(BF16)
            o_ref[...] = o_ref[...] + lax.dot_general(selection(start, rows).astype(BF16), ys, TN_DIMS,
                                                      preferred_element_type=F32)

    def over_chunks(fn):
        def body(r, carry):
            fn(pl.multiple_of(r * R, R), R)
            return carry
        lax.fori_loop(0, n_full, body, 0)

        @pl.when(has_tail)
        def _():
            fn(tail_start, half)

    @pl.when(jnp.logical_and(e == 0, f == 0))
    def _():
        o_ref[...] = jnp.zeros_like(o_ref)

    @pl.when(f == 0)
    def _():
        over_chunks(gather)

    over_chunks(expert)


def _moe(xb, gates_t, pos_t, counts, wg, wu, wd):
    T, D = xb.shape
    F = wg.shape[2]
    tt = MOE_TILE
    nf = 4
    tf = F // nf
    grid_spec = pltpu.PrefetchScalarGridSpec(
        num_scalar_prefetch=1,
        grid=(T // tt, N_EXPERTS, nf),
        in_specs=[pl.BlockSpec((tt, D), lambda m, e, f, c: (m, 0)),
                  pl.BlockSpec((1, N_EXPERTS, tt), lambda m, e, f, c: (m, 0, 0)),
                  pl.BlockSpec((1, N_EXPERTS, tt), lambda m, e, f, c: (m, 0, 0)),
                  pl.BlockSpec((1, D, tf), lambda m, e, f, c: (e, 0, f)),
                  pl.BlockSpec((1, D, tf), lambda m, e, f, c: (e, 0, f)),
                  pl.BlockSpec((1, tf, D), lambda m, e, f, c: (e, f, 0))],
        out_specs=pl.BlockSpec((tt, D), lambda m, e, f, c: (m, 0)),
        scratch_shapes=[pltpu.VMEM((tt, D), BF16), pltpu.VMEM((tt, D), F32), pltpu.VMEM((tt, LANES), F32)],
    )
    return pl.pallas_call(
        functools.partial(_moe_kernel, nf=nf),
        grid_spec=grid_spec,
        out_shape=jax.ShapeDtypeStruct((T, D), F32),
        compiler_params=_params("arbitrary", "arbitrary", "arbitrary"),
        name="moe_experts",
    )(counts, xb, gates_t, pos_t, wg, wu, wd)


def _resid_ln_kernel(x_ref, f_ref, g_ref, b_ref, o_ref):
    o_ref[...] = _layer_norm(DEEPNORM_ALPHA * x_ref[...] + f_ref[...], g_ref[...], b_ref[...])


def _resid_ln(x, f, g, b):
    T, D = x.shape
    tm = 512
    row = lambda: pl.BlockSpec((tm, D), lambda m: (m, 0))
    vec = lambda: pl.BlockSpec((1, D), lambda m: (0, 0))
    return pl.pallas_call(
        _resid_ln_kernel,
        grid=(T // tm,),
        in_specs=[row(), row(), vec(), vec()],
        out_specs=row(),
        out_shape=jax.ShapeDtypeStruct((T, D), F32),
        compiler_params=_params("arbitrary"),
        name="resid_ln",
    )(x, f, g, b)


def _block_diag(w):
    n, d, _ = w.shape
    out = jnp.zeros((n * d, n * d), w.dtype)
    for i in range(n):
        out = out.at[i * d:(i + 1) * d, i * d:(i + 1) * d].set(w[i])
    return out


def _rope_tables(positions):
    B, S = positions.shape
    half = ROPE_DIMS // 2
    inv_freq = ROPE_THETA ** (-jnp.arange(0, ROPE_DIMS, 2, dtype=F32) / ROPE_DIMS)
    d = jnp.arange(LANES) % HEAD_DIM
    ang = positions.astype(F32).reshape(B * S, 1) * inv_freq[d % half][None, :]
    cos = jnp.cos(ang)
    sin = jnp.sin(ang)
    c = jnp.where(d < ROPE_DIMS, cos, 1.0)
    s1 = jnp.where(d < half, -sin, 0.0)
    s2 = jnp.where((d >= half) & (d < ROPE_DIMS), sin, 0.0)
    return c, s1, s2


def _mixer_layer(x, xb, rope, B, S, w_in, conv_w, conv_b, ga_w, ga_b, gx_w, gx_b, lam,
                 mu, w0, w_up, a0, a_up, g_up, k_k, k_a, r_k, lnx_g, lnx_b, w_out, ln_g, ln_b):
    W = RWKV_WIDTH
    row = lambda t: t.reshape(1, -1)
    proj = _in_proj(xb, w_in.astype(BF16))
    q, k, vt, pen = _attn_prep(proj, *rope, B, S)
    att = _attention(q, k, vt, pen, B, S)
    lru = _rglru(proj, conv_w, row(conv_b), _block_diag(ga_w), row(ga_b), _block_diag(gx_w), row(gx_b), row(lam), B, S)
    head_ones = _block_diag(jnp.ones((RWKV_HEADS, HEAD_DIM, HEAD_DIM), F32))
    wlr = jnp.zeros((W, 3 * W), F32)
    wlr = wlr.at[0:64, 0:W].set(w_up).at[64:128, W:2 * W].set(a_up).at[128:256, 2 * W:3 * W].set(g_up)
    r, ld, k2, v, kn, a, g = _rwkv_prep(proj, row(mu), wlr, row(w0), row(a0), row(k_k), row(k_a), head_ones, B, S)
    y = _wkv(r, ld, k2, v, kn, a, B, S)
    rwk = _wkv_post(y, r, k2, v, g, row(r_k), row(lnx_g), row(lnx_b), head_ones.astype(BF16))
    wo = w_out.astype(BF16)
    return _outproj_ln(att, lru, rwk, x, wo[0:ATT_WIDTH], wo[ATT_WIDTH:ATT_WIDTH + LRU_WIDTH],
                       wo[ATT_WIDTH + LRU_WIDTH:], row(ln_g), row(ln_b))


def kernel(x, positions, w_in, lru_conv_w, lru_conv_b, lru_ga_w, lru_ga_b, lru_gx_w, lru_gx_b, lru_lambda, rwkv_mu, rwkv_w0, rwkv_w_up, rwkv_a0, rwkv_a_up, rwkv_g_up, rwkv_k_k, rwkv_k_a, rwkv_r_k, rwkv_lnx_g, rwkv_lnx_b, w_out, ln1_g, ln1_b, ffn_w_gate, ffn_w_up, ffn_w_down, moe_router, moe_w_gate, moe_w_up, moe_w_down, ln2_g, ln2_b):
    B, S, D = x.shape
    T = B * S
    rope = _rope_tables(positions)
    xf = x.reshape(T, D)
    xb = xf.astype(BF16)
    for l in range(DEPTH):
        xf, xb = _mixer_layer(xf, xb, rope, B, S, w_in[l], lru_conv_w[l], lru_conv_b[l], lru_ga_w[l], lru_ga_b[l],
                              lru_gx_w[l], lru_gx_b[l], lru_lambda[l], rwkv_mu[l], rwkv_w0[l], rwkv_w_up[l],
                              rwkv_a0[l], rwkv_a_up[l], rwkv_g_up[l], rwkv_k_k[l], rwkv_k_a[l], rwkv_r_k[l],
                              rwkv_lnx_g[l], rwkv_lnx_b[l], w_out[l], ln1_g[l], ln1_b[l])
        g2 = ln2_g[l].reshape(1, D)
        b2 = ln2_b[l].reshape(1, D)
        if l % 2 == 0:
            i = l // 2
            xf, xb = _ffn_ln(xb, xf, ffn_w_gate[i].astype(BF16), ffn_w_up[i].astype(BF16),
                             ffn_w_down[i].astype(BF16), g2, b2)
        else:
            i = l // 2
            idx = jnp.arange(MOE_TILE)
            tri = (idx[:, None] <= idx[None, :]).astype(BF16)
            gates_t, pos_t, cnt = _router(xf, moe_router[i].T, tri)
            counts = cnt[:, :, 0].astype(jnp.int32).reshape(-1)
            f = _moe(xb, gates_t, pos_t, counts, moe_w_gate[i].astype(BF16), moe_w_up[i].astype(BF16),
                     moe_w_down[i].astype(BF16))
            xf = _resid_ln(xf, f, g2, b2)
            xb = xf.astype(BF16)
    return xf.reshape(B, S, D)
```

```python
import functools

import jax
import jax.numpy as jnp
from jax import lax
from jax.experimental import pallas as pl
from jax.experimental.pallas import tpu as pltpu

F32 = jnp.float32
BF16 = jnp.bfloat16
HI = lax.Precision.HIGHEST
NT_DIMS = (((1,), (1,)), ((), ()))
TN_DIMS = (((0,), (0,)), ((), ()))

HEAD_DIM = 64
ATT_HEADS = 8
ATT_WIDTH = ATT_HEADS * HEAD_DIM
LRU_WIDTH = 256
RWKV_HEADS = 4
RWKV_WIDTH = 256
ROPE_DIMS = 16
ROPE_THETA = 500000.0
MOBA_BLOCK = 256
MOBA_TOPK = 3
CONV_WIDTH = 4
LRU_C = 8.0
RWKV_LN_EPS = 64e-5
N_EXPERTS = 8
LN_EPS = 1e-5
DEPTH = 2
DEEPNORM_ALPHA = (2 * DEPTH) ** 0.25
NEG_INF = -1e30
LOG2_E = 1.4426950408889634

LANES = 128
VMEM_LIMIT = 56 * 1024 * 1024

ATTN_STREAMS = 4
WKV_CHUNK = 64
WKV_GROUP = 8
LRU_CHUNK = 256
RWKV_PREP_ROWS = 512
MOE_TILE = 1024
MOE_ROWS = 256


def _params(*sem):
    return pltpu.CompilerParams(dimension_semantics=sem, vmem_limit_bytes=VMEM_LIMIT)


def _sigmoid(x):
    return 1.0 / (1.0 + jnp.exp(-x))


def _softplus(x):
    return jnp.maximum(x, 0.0) + jnp.log1p(jnp.exp(-jnp.abs(x)))


def _expm1(z):
    u = jnp.exp(z)
    um1 = u - 1.0
    return jnp.where(u == 1.0, z, jnp.where(um1 == -1.0, -1.0, um1 * z / jnp.log(u)))


def _layer_norm(y, g, b):
    m = jnp.mean(y, axis=-1, keepdims=True)
    d = y - m
    var = jnp.mean(d * d, axis=-1, keepdims=True)
    return d * lax.rsqrt(var + LN_EPS) * g + b


def _matmul_kernel(x_ref, w_ref, o_ref):
    o_ref[...] = jnp.dot(x_ref[...], w_ref[...], preferred_element_type=F32)


def _in_proj(xb, w):
    T, D = xb.shape
    N = w.shape[1]
    tm, tn = 1024, 1024
    return pl.pallas_call(
        _matmul_kernel,
        grid=(N // tn, T // tm),
        in_specs=[pl.BlockSpec((tm, D), lambda n, m: (m, 0)),
                  pl.BlockSpec((D, tn), lambda n, m: (0, n))],
        out_specs=pl.BlockSpec((tm, tn), lambda n, m: (m, n)),
        out_shape=jax.ShapeDtypeStruct((T, N), F32),
        compiler_params=_params("arbitrary", "arbitrary"),
        name="in_proj",
    )(xb, w)


def _attn_prep_kernel(proj_ref, c_ref, s1_ref, s2_ref, q_ref, k_ref, vt_ref, pen_ref, km_ref, *, nb):
    i = pl.program_id(1)

    @pl.when(i == 0)
    def _():
        km_ref[...] = jnp.zeros_like(km_ref)

    c = c_ref[...]
    s1 = s1_ref[...]
    s2 = s2_ref[...]

    def rope(xt):
        return xt * c + pltpu.roll(xt, LANES - ROPE_DIMS // 2, 1) * s1 + pltpu.roll(xt, ROPE_DIMS // 2, 1) * s2

    q_tiles = []
    for ct in range(ATT_WIDTH // LANES):
        lo, hi = ct * LANES, (ct + 1) * LANES
        qr = rope(proj_ref[:, lo:hi])
        q_tiles.append(qr)
        q_ref[:, lo:hi] = (qr * (HEAD_DIM ** -0.5 * LOG2_E)).astype(BF16)
        kr = rope(proj_ref[:, ATT_WIDTH + lo:ATT_WIDTH + hi])
        k_ref[0, 0, :, lo:hi] = kr.astype(BF16)
        km_row = lax.broadcasted_iota(jnp.int32, (nb, LANES), 0)
        km_ref[:, lo:hi] = jnp.where(km_row == i, jnp.mean(kr, axis=0, keepdims=True), km_ref[:, lo:hi])
    vt_ref[0, 0] = proj_ref[:, 2 * ATT_WIDTH:3 * ATT_WIDTH].T.astype(BF16)

    q_rot = jnp.concatenate(q_tiles, axis=1)
    km = km_ref[...]
    km_rows = jnp.concatenate([km] * ATT_HEADS, axis=0)
    row_head = lax.broadcasted_iota(jnp.int32, km_rows.shape, 0) // nb
    lane_head = lax.broadcasted_iota(jnp.int32, km_rows.shape, 1) // HEAD_DIM
    g_all = lax.dot_general(jnp.where(row_head == lane_head, km_rows, 0.0), q_rot, NT_DIMS,
                            precision=HI, preferred_element_type=F32)
    n_iota = lax.broadcasted_iota(jnp.int32, (nb, MOBA_BLOCK), 0)
    past = n_iota < i
    for h in range(ATT_HEADS):
        g = g_all[h * nb:(h + 1) * nb, :]
        rank = jnp.zeros((nb, MOBA_BLOCK), F32)
        for m in range(nb):
            gm = g[m:m + 1, :]
            beats = jnp.where(gm > g, 1.0, jnp.where(gm == g, jnp.where(n_iota > m, 1.0, 0.0), 0.0))
            rank = rank + jnp.where(m < i, beats, 0.0)
        pen = jnp.where(past, jnp.where(rank < float(MOBA_TOPK), 0.0, NEG_INF), NEG_INF)
        half = (h % 2) * MOBA_BLOCK
        pen_ref[0, h // 2, :, half:half + MOBA_BLOCK] = pen


def _attn_prep(proj, rope_c, rope_s1, rope_s2, B, S):
    T = B * S
    nb = S // MOBA_BLOCK
    blk = MOBA_BLOCK
    return pl.pallas_call(
        functools.partial(_attn_prep_kernel, nb=nb),
        grid=(B, nb),
        in_specs=[pl.BlockSpec((blk, 3 * ATT_WIDTH), lambda b, i: (b * nb + i, 0)),
                  pl.BlockSpec((blk, LANES), lambda b, i: (b * nb + i, 0)),
                  pl.BlockSpec((blk, LANES), lambda b, i: (b * nb + i, 0)),
                  pl.BlockSpec((blk, LANES), lambda b, i: (b * nb + i, 0))],
        out_specs=[pl.BlockSpec((blk, ATT_WIDTH), lambda b, i: (b * nb + i, 0)),
                   pl.BlockSpec((1, 1, blk, ATT_WIDTH), lambda b, i: (b, i, 0, 0)),
                   pl.BlockSpec((1, 1, ATT_WIDTH, blk), lambda b, i: (b, i, 0, 0)),
                   pl.BlockSpec((1, ATT_HEADS // 2, nb, 2 * blk), lambda b, i: (b, 0, 0, i))],
        out_shape=[jax.ShapeDtypeStruct((T, ATT_WIDTH), BF16),
                   jax.ShapeDtypeStruct((B, nb, blk, ATT_WIDTH), BF16),
                   jax.ShapeDtypeStruct((B, nb, ATT_WIDTH, blk), BF16),
                   jax.ShapeDtypeStruct((B, ATT_HEADS // 2, nb, 2 * S), F32)],
        scratch_shapes=[pltpu.VMEM((nb, ATT_WIDTH), F32)],
        compiler_params=_params("arbitrary", "arbitrary"),
        name="attn_prep",
    )(proj, rope_c, rope_s1, rope_s2)


def _attn_kernel(q_ref, k_ref, vt_ref, pen_ref, o_ref):
    i = pl.program_id(2)
    blk = MOBA_BLOCK
    streams = range(ATTN_STREAMS)
    each = lambda fn, *xs: [fn(*(x[t] for x in xs)) for t in streams]
    lanes = lambda t: slice(t * LANES, (t + 1) * LANES)
    lane_head = lax.broadcasted_iota(jnp.int32, (blk, LANES), 1) // HEAD_DIM
    kidx = lax.broadcasted_iota(jnp.int32, (blk, 2 * blk), 0)
    qidx = lax.broadcasted_iota(jnp.int32, (blk, 2 * blk), 1) % blk
    nt = lambda a, b: lax.dot_general(a, b, NT_DIMS, preferred_element_type=F32)
    nn = lambda a, b: jnp.dot(a, b, preferred_element_type=F32)
    col_max = lambda x: jnp.max(x, axis=0, keepdims=True)
    col_sum = lambda x: jnp.sum(x, axis=0, keepdims=True)

    def stacked_q(t):
        q = q_ref[:, lanes(t)]
        zero = jnp.zeros_like(q)
        return jnp.concatenate([jnp.where(lane_head == 0, q, zero), jnp.where(lane_head == 1, q, zero)], axis=0)

    qs = [stacked_q(t) for t in streams]
    s = [jnp.where(kidx <= qidx, nt(k_ref[0, i, :, lanes(t)], qs[t]), NEG_INF) for t in streams]
    m0 = each(col_max, s)
    p = each(lambda x, m: jnp.exp2(x - m), s, m0)
    l0 = each(col_sum, p)
    acc0 = [nn(vt_ref[0, i, lanes(t), :], p[t].astype(BF16)) for t in streams]

    def body(jj, carry):
        m, l, acc = carry
        j0 = 2 * jj
        j1 = j0 + 1
        s0 = [nt(k_ref[0, j0, :, lanes(t)], qs[t]) + pen_ref[0, t, pl.ds(j0, 1), :] for t in streams]
        s1 = [nt(k_ref[0, j1, :, lanes(t)], qs[t]) + pen_ref[0, t, pl.ds(j1, 1), :] for t in streams]
        m_new = each(lambda mo, a, b: jnp.maximum(mo, jnp.maximum(col_max(a), col_max(b))), m, s0, s1)
        alpha = each(lambda mo, mn: jnp.exp2(mo - mn), m, m_new)
        p0 = each(lambda a, mn: jnp.exp2(a - mn), s0, m_new)
        p1 = each(lambda a, mn: jnp.exp2(a - mn), s1, m_new)
        l = each(lambda al, lo, a, b: al * lo + col_sum(a) + col_sum(b), alpha, l, p0, p1)
        pv0 = [nn(vt_ref[0, j0, lanes(t), :], p0[t].astype(BF16)) for t in streams]
        pv1 = [nn(vt_ref[0, j1, lanes(t), :], p1[t].astype(BF16)) for t in streams]
        acc = each(lambda ac, al, a, b: ac * al + a + b, acc, alpha, pv0, pv1)
        return tuple(m_new), tuple(l), tuple(acc)

    _, l, acc = lax.fori_loop(0, (i + 1) // 2, body, (tuple(m0), tuple(l0), tuple(acc0)))
    row = lax.broadcasted_iota(jnp.int32, (LANES, blk), 0)
    for t in streams:
        out = acc[t] / l[t]
        o_ref[:, lanes(t)] = jnp.where(row < HEAD_DIM, out[:, 0:blk], out[:, blk:2 * blk]).T


def _attention(q, k, vt, pen, B, S):
    T = B * S
    nb = S // MOBA_BLOCK
    blk = MOBA_BLOCK
    w = ATTN_STREAMS * LANES
    n_groups = ATT_WIDTH // w
    return pl.pallas_call(
        _attn_kernel,
        grid=(B, n_groups, nb),
        in_specs=[pl.BlockSpec((blk, w), lambda b, hp, i: (b * nb + i, hp)),
                  pl.BlockSpec((1, nb, blk, w), lambda b, hp, i: (b, 0, 0, hp)),
                  pl.BlockSpec((1, nb, w, blk), lambda b, hp, i: (b, 0, hp, 0)),
                  pl.BlockSpec((1, ATTN_STREAMS, nb, 2 * blk), lambda b, hp, i: (b, hp, 0, i))],
        out_specs=pl.BlockSpec((blk, w), lambda b, hp, i: (b * nb + i, hp)),
        out_shape=jax.ShapeDtypeStruct((T, ATT_WIDTH), F32),
        compiler_params=_params("arbitrary", "arbitrary", "arbitrary"),
        name="moba_attn",
    )(q, k, vt, pen)


def _lru_kernel(p_ref, cw_ref, cb_ref, ga_ref, gab_ref, gx_ref, gxb_ref, lam_ref, o_ref, xbuf, hc):
    t = pl.program_id(1)
    tc = LRU_CHUNK
    pad = 8

    @pl.when(t == 0)
    def _():
        xbuf[0:pad, :] = jnp.zeros((pad, LRU_WIDTH), F32)
        hc[...] = jnp.zeros_like(hc)

    @pl.when(t > 0)
    def _():
        xbuf[0:pad, :] = xbuf[tc:tc + pad, :]

    x = p_ref[:, 0:LRU_WIDTH]
    gate = p_ref[:, LRU_WIDTH:2 * LRU_WIDTH]
    xbuf[pad:pad + tc, :] = x
    xc = cb_ref[...] + x * cw_ref[CONV_WIDTH - 1:CONV_WIDTH, :]
    for j in range(CONV_WIDTH - 1):
        back = CONV_WIDTH - 1 - j
        xc = xc + xbuf[pad - back:pad - back + tc, :] * cw_ref[j:j + 1, :]

    r = _sigmoid(jnp.dot(xc, ga_ref[...], precision=HI, preferred_element_type=F32) + gab_ref[...])
    ig = _sigmoid(jnp.dot(xc, gx_ref[...], precision=HI, preferred_element_type=F32) + gxb_ref[...])
    log_a = -LRU_C * r * _softplus(-lam_ref[...])
    a = jnp.exp(log_a)
    u = jnp.sqrt(-_expm1(2.0 * log_a)) * (ig * xc)

    rows = lax.broadcasted_iota(jnp.int32, (tc, LRU_WIDTH), 0)
    d = 1
    while d < tc:
        keep = rows >= d
        a_prev = jnp.where(keep, pltpu.roll(a, d, 0), 1.0)
        u_prev = jnp.where(keep, pltpu.roll(u, d, 0), 0.0)
        u = a * u_prev + u
        a = a * a_prev
        d *= 2
    h = u + a * hc[...]
    hc[...] = h[tc - 1:tc, :]
    gl = 0.5 * gate * (1.0 + jnp.tanh(0.7978845608028654 * (gate + 0.044715 * gate * gate * gate)))
    o_ref[...] = h * gl


def _rglru(proj, cw, cb, ga, gab, gx, gxb, lam, B, S):
    T = B * S
    tc = LRU_CHUNK
    nt = S // tc
    col = (3 * ATT_WIDTH) // (2 * LRU_WIDTH)
    vec = lambda: pl.BlockSpec((1, LRU_WIDTH), lambda b, t: (0, 0))
    mat = lambda: pl.BlockSpec((LRU_WIDTH, LRU_WIDTH), lambda b, t: (0, 0))
    return pl.pallas_call(
        _lru_kernel,
        grid=(B, nt),
        in_specs=[pl.BlockSpec((tc, 2 * LRU_WIDTH), lambda b, t: (b * nt + t, col)),
                  pl.BlockSpec((CONV_WIDTH, LRU_WIDTH), lambda b, t: (0, 0)),
                  vec(), mat(), vec(), mat(), vec(), vec()],
        out_specs=pl.BlockSpec((tc, LRU_WIDTH), lambda b, t: (b * nt + t, 0)),
        out_shape=jax.ShapeDtypeStruct((T, LRU_WIDTH), F32),
        scratch_shapes=[pltpu.VMEM((tc + 8, LRU_WIDTH), F32), pltpu.VMEM((1, LRU_WIDTH), F32)],
        compiler_params=_params("arbitrary", "arbitrary"),
        name="rglru",
    )(proj, cw, cb, ga, gab, gx, gxb, lam)


def _rwkv_prep_kernel(p_ref, pp_ref, mu_ref, wlr_ref, w0_ref, a0_ref, kk_ref, ka_ref, ones_ref,
                      r_ref, ld_ref, k2_ref, v_ref, kn_ref, a_ref, g_ref):
    i = pl.program_id(1)
    W = RWKV_WIDTH
    p = p_ref[...]
    rows = lax.broadcasted_iota(jnp.int32, p.shape, 0)
    prev_last = jnp.where(i == 0, 0.0, pp_ref[7:8, :])
    p_prev = jnp.where(rows == 0, prev_last, pltpu.roll(p, 1, 0))
    pf = p + (p_prev - p) * mu_ref[...]
    r = pf[:, 0:W]
    k = pf[:, W:2 * W]
    v = pf[:, 2 * W:3 * W]
    z = pf[:, 3 * W:4 * W]
    lane = lax.broadcasted_iota(jnp.int32, z.shape, 1)
    zz = jnp.where(lane < 64, jnp.tanh(z), jnp.where(lane < 128, z, _sigmoid(z)))
    lr = jnp.dot(zz, wlr_ref[...], precision=HI, preferred_element_type=F32)
    w = -_softplus(-(w0_ref[...] + lr[:, 0:W])) - 0.5
    a = _sigmoid(a0_ref[...] + lr[:, W:2 * W])
    kk = k * kk_ref[...]
    ssq = jnp.dot(kk * kk, ones_ref[...], precision=HI, preferred_element_type=F32)
    kk = kk / jnp.maximum(jnp.sqrt(ssq), 1e-12)
    r_ref[...] = r
    ld_ref[...] = -jnp.exp(w)
    k2_ref[...] = k * (1.0 + (a - 1.0) * ka_ref[...])
    v_ref[...] = v
    kn_ref[...] = kk
    a_ref[...] = a
    g_ref[...] = lr[:, 2 * W:3 * W]


def _rwkv_prep(proj, mu, wlr, w0, a0, k_k, k_a, head_ones, B, S):
    T = B * S
    tr = RWKV_PREP_ROWS
    nt = S // tr
    W = RWKV_WIDTH
    col = 2
    vec = lambda: pl.BlockSpec((1, W), lambda b, t: (0, 0))
    out = lambda: pl.BlockSpec((tr, W), lambda b, t: (b * nt + t, 0))
    return pl.pallas_call(
        _rwkv_prep_kernel,
        grid=(B, nt),
        in_specs=[pl.BlockSpec((tr, 4 * W), lambda b, t: (b * nt + t, col)),
                  pl.BlockSpec((8, 4 * W), lambda b, t: (jnp.maximum((b * nt + t) * (tr // 8) - 1, 0), col)),
                  pl.BlockSpec((1, 4 * W), lambda b, t: (0, 0)),
                  pl.BlockSpec((W, 3 * W), lambda b, t: (0, 0)),
                  vec(), vec(), vec(), vec(),
                  pl.BlockSpec((W, W), lambda b, t: (0, 0))],
        out_specs=[out() for _ in range(7)],
        out_shape=[jax.ShapeDtypeStruct((T, W), F32) for _ in range(7)],
        compiler_params=_params("arbitrary", "arbitrary"),
        name="rwkv_prep",
    )(proj, proj, mu, wlr, w0, a0, k_k, k_a, head_ones)


def _wkv_kernel(r_ref, ld_ref, k2_ref, v_ref, kn_ref, a_ref, y_ref, st_ref):
    c = pl.program_id(1)
    C = WKV_CHUNK
    W = RWKV_WIDTH
    H = RWKV_HEADS

    @pl.when(c == 0)
    def _():
        st_ref[...] = jnp.zeros_like(st_ref)

    def mm(x, y):
        return jnp.dot(x.astype(BF16), y.astype(BF16), preferred_element_type=F32)

    def mm_nt(x, y):
        return lax.dot_general(x.astype(BF16), y.astype(BF16), NT_DIMS, preferred_element_type=F32)

    def mm_tn(x, y):
        return lax.dot_general(x.astype(BF16), y.astype(BF16), TN_DIMS, preferred_element_type=F32)

    tr = lax.broadcasted_iota(jnp.int32, (C, C), 0)
    tc = lax.broadcasted_iota(jnp.int32, (C, C), 1)
    tri = jnp.where(tr >= tc, 1.0, 0.0)
    row = lax.broadcasted_iota(jnp.int32, (W, W), 0)
    colm = lax.broadcasted_iota(jnp.int32, (W, W), 1)
    same_head = (row // C) == (colm // HEAD_DIM)
    strict = lambda x: jnp.where(same_head, jnp.where(row > colm, x, 0.0), 0.0)
    incl = lambda x: jnp.where(same_head, jnp.where(row >= colm, x, 0.0), 0.0)
    same16 = (row // 16) == (colm // 16)
    eye = jnp.where(row == colm, 1.0, 0.0)

    def stack(x):
        return jnp.where(same_head, jnp.concatenate([x] * H, axis=0), 0.0)

    G = range(WKV_GROUP)
    each = lambda fn, *xs: [fn(*(x[g] for x in xs)) for g in G]

    r = [r_ref[g] for g in G]
    ld = [ld_ref[g] for g in G]
    k2 = [k2_ref[g] for g in G]
    v_s = [stack(v_ref[g]) for g in G]
    kn = [kn_ref[g] for g in G]
    b_s = [kn[g] * a_ref[g] for g in G]
    cl = each(lambda x: jnp.dot(tri, x, precision=HI, preferred_element_type=F32), ld)
    cl_end = [x[C - 1:C, :] for x in cl]
    e_neg = [jnp.exp(-x) for x in cl]
    e_end = each(lambda ce, x: jnp.exp(ce - x), cl_end, cl)
    ar_s = [jnp.concatenate([stack(-kn[g] * jnp.exp(cl[g] - ld[g])), stack(r[g] * jnp.exp(cl[g]))], axis=0) for g in G]
    bk_s = [jnp.concatenate([stack(b_s[g] * e_neg[g]), stack(k2[g] * e_neg[g])], axis=0) for g in G]
    end_s = [jnp.concatenate([stack(b_s[g] * e_end[g]), stack(k2[g] * e_end[g])], axis=0) for g in G]

    prod = each(mm_nt, ar_s, bk_s)
    n_ab = [strict(p[0:W, 0:W]) for p in prod]
    a_ak = [strict(p[0:W, W:2 * W]) for p in prod]
    a_rb = [incl(p[W:2 * W, 0:W]) for p in prod]
    a_rk = [incl(p[W:2 * W, W:2 * W]) for p in prod]

    nd = [jnp.where(same16, n, 0.0) for n in n_ab]
    lo = each(lambda n, d: n - d, n_ab, nd)
    n2 = each(mm, nd, nd)
    p1 = [eye + d for d in nd]
    n4 = each(mm, n2, n2)
    p2 = each(lambda p, n: p + mm(p, n), p1, n2)
    n8 = each(mm, n4, n4)
    p3 = each(lambda p, n: p + mm(p, n), p2, n4)
    dinv = each(lambda p, n: p + mm(p, n), p3, n8)
    x1 = each(mm, dinv, lo)
    x2 = each(mm, x1, x1)
    y1 = [eye + x for x in x1]
    m_inv = each(lambda y, x: y + mm(y, x), y1, x2)
    tinv = each(mm, m_inv, dinv)

    st = [st_ref[g] for g in G]
    ar_st = each(mm_nt, ar_s, st)
    akv = each(mm, a_ak, v_s)
    u_s = each(lambda t, a, b: mm(t, a[0:W] + b), tinv, ar_st, akv)
    rkv = each(mm, a_rk, v_s)
    y_s = each(lambda a, m, u, b: a[W:2 * W] + mm(m, u) + b, ar_st, a_rb, u_s, rkv)
    upd = each(lambda u, v, e: mm_tn(jnp.concatenate([u, v], axis=0), e), u_s, v_s, end_s)
    for g in G:
        y = y_s[g][0:C, :]
        for h in range(1, H):
            y = y + y_s[g][h * C:(h + 1) * C, :]
        y_ref[g] = y
        st_ref[g] = st[g] * jnp.exp(cl_end[g]) + upd[g]


def _wkv(r, ld, k2, v, kn, a, B, S):
    C = WKV_CHUNK
    nc = S // C
    W = RWKV_WIDTH
    G = WKV_GROUP
    blk = lambda: pl.BlockSpec((G, C, W), lambda b, c: (b, c, 0))
    seq = lambda t: t.reshape(B, S, W)
    y = pl.pallas_call(
        _wkv_kernel,
        grid=(B // G, nc),
        in_specs=[blk() for _ in range(6)],
        out_specs=blk(),
        out_shape=jax.ShapeDtypeStruct((B, S, W), F32),
        scratch_shapes=[pltpu.VMEM((G, W, W), F32)],
        compiler_params=_params("arbitrary", "arbitrary"),
        name="wkv7",
    )(seq(r), seq(ld), seq(k2), seq(v), seq(kn), seq(a))
    return y.reshape(B * S, W)


def _wkv_post_kernel(y_ref, r_ref, k2_ref, v_ref, g_ref, rk_ref, lg_ref, lb_ref, ones_ref, o_ref):
    ones = ones_ref[...]

    def head_sum(x):
        hi = x.astype(BF16)
        lo = (x - hi.astype(F32)).astype(BF16)
        return (jnp.dot(hi, ones, preferred_element_type=F32) + jnp.dot(lo, ones, preferred_element_type=F32))

    y = y_ref[...]
    r = r_ref[...]
    k2 = k2_ref[...]
    inv_n = 1.0 / HEAD_DIM
    d = y - head_sum(y) * inv_n
    var = head_sum(d * d) * inv_n
    yn = d * lax.rsqrt(var + RWKV_LN_EPS) * lg_ref[...] + lb_ref[...]
    bonus = head_sum(r * k2 * rk_ref[...]) * v_ref[...]
    o_ref[...] = (yn + bonus) * g_ref[...]


def _wkv_post(y, r, k2, v, g, r_k, lnx_g, lnx_b, head_ones_bf16):
    T, W = y.shape
    tm = 512
    blk = lambda: pl.BlockSpec((tm, W), lambda m: (m, 0))
    vec = lambda: pl.BlockSpec((1, W), lambda m: (0, 0))
    return pl.pallas_call(
        _wkv_post_kernel,
        grid=(T // tm,),
        in_specs=[blk() for _ in range(5)] + [vec(), vec(), vec(), pl.BlockSpec((W, W), lambda m: (0, 0))],
        out_specs=blk(),
        out_shape=jax.ShapeDtypeStruct((T, W), F32),
        compiler_params=_params("arbitrary"),
        name="wkv_post",
    )(y, r, k2, v, g, r_k, lnx_g, lnx_b, head_ones_bf16)


def _outproj_ln_kernel(att_ref, lru_ref, rwk_ref, x_ref, wa_ref, wl_ref, wr_ref, g_ref, b_ref, o_ref, ob_ref):
    h = jnp.dot(att_ref[...].astype(BF16), wa_ref[...], preferred_element_type=F32)
    h = h + jnp.dot(lru_ref[...].astype(BF16), wl_ref[...], preferred_element_type=F32)
    h = h + jnp.dot(rwk_ref[...].astype(BF16), wr_ref[...], preferred_element_type=F32)
    y = _layer_norm(DEEPNORM_ALPHA * x_ref[...] + h, g_ref[...], b_ref[...])
    o_ref[...] = y
    ob_ref[...] = y.astype(BF16)


def _outproj_ln(att, lru, rwk, x, wa, wl, wr, g, b):
    T, D = x.shape
    tm = 512
    row = lambda w: pl.BlockSpec((tm, w), lambda m: (m, 0))
    full = lambda a: pl.BlockSpec(a.shape, lambda m: (0, 0))
    return pl.pallas_call(
        _outproj_ln_kernel,
        grid=(T // tm,),
        in_specs=[row(ATT_WIDTH), row(LRU_WIDTH), row(RWKV_WIDTH), row(D), full(wa), full(wl), full(wr), full(g), full(b)],
        out_specs=[row(D), row(D)],
        out_shape=[jax.ShapeDtypeStruct((T, D), F32), jax.ShapeDtypeStruct((T, D), BF16)],
        compiler_params=_params("arbitrary"),
        name="outproj_ln",
    )(att, lru, rwk, x, wa, wl, wr, g, b)


def _ffn_kernel(xb_ref, x_ref, wg_ref, wu_ref, wd_ref, g_ref, b_ref, o_ref, ob_ref, acc_ref, *, nf):
    f = pl.program_id(1)
    xb = xb_ref[...]
    hg = jnp.dot(xb, wg_ref[...], preferred_element_type=F32)
    hu = jnp.dot(xb, wu_ref[...], preferred_element_type=F32)
    hh = (hg * _sigmoid(hg) * hu).astype(BF16)
    contrib = jnp.dot(hh, wd_ref[...], preferred_element_type=F32)

    @pl.when(f == 0)
    def _():
        acc_ref[...] = contrib

    @pl.when(f > 0)
    def _():
        acc_ref[...] = acc_ref[...] + contrib

    @pl.when(f == nf - 1)
    def _():
        y = _layer_norm(DEEPNORM_ALPHA * x_ref[...] + acc_ref[...], g_ref[...], b_ref[...])
        o_ref[...] = y
        ob_ref[...] = y.astype(BF16)


def _ffn_ln(xb, x, wg, wu, wd, g, b):
    T, D = x.shape
    F = wg.shape[1]
    tm = 512
    nf = 2
    tf = F // nf
    return pl.pallas_call(
        functools.partial(_ffn_kernel, nf=nf),
        grid=(T // tm, nf),
        in_specs=[pl.BlockSpec((tm, D), lambda m, f: (m, 0)),
                  pl.BlockSpec((tm, D), lambda m, f: (m, 0)),
                  pl.BlockSpec((D, tf), lambda m, f: (0, f)),
                  pl.BlockSpec((D, tf), lambda m, f: (0, f)),
                  pl.BlockSpec((tf, D), lambda m, f: (f, 0)),
                  pl.BlockSpec((1, D), lambda m, f: (0, 0)),
                  pl.BlockSpec((1, D), lambda m, f: (0, 0))],
        out_specs=[pl.BlockSpec((tm, D), lambda m, f: (m, 0)), pl.BlockSpec((tm, D), lambda m, f: (m, 0))],
        out_shape=[jax.ShapeDtypeStruct((T, D), F32), jax.ShapeDtypeStruct((T, D), BF16)],
        scratch_shapes=[pltpu.VMEM((tm, D), F32)],
        compiler_params=_params("arbitrary", "arbitrary"),
        name="ffn_ln",
    )(xb, x, wg, wu, wd, g, b)


def _router_kernel(x_ref, wrt_ref, tri_ref, g_ref, pos_ref, cnt_ref):
    tt = x_ref.shape[0]
    logits = lax.dot_general(wrt_ref[...], x_ref[...], NT_DIMS, precision=HI, preferred_element_type=F32)
    row = lax.broadcasted_iota(jnp.int32, logits.shape, 0).astype(F32)
    m1 = jnp.max(logits, axis=0, keepdims=True)
    i1 = jnp.min(jnp.where(logits == m1, row, float(N_EXPERTS)), axis=0, keepdims=True)
    rest = jnp.where(row == i1, -jnp.inf, logits)
    m2 = jnp.max(rest, axis=0, keepdims=True)
    i2 = jnp.min(jnp.where(rest == m2, row, float(N_EXPERTS)), axis=0, keepdims=True)
    e = jnp.exp(m2 - m1)
    g1 = 1.0 / (1.0 + e)
    g2 = e / (1.0 + e)
    g_ref[0] = jnp.where(row == i1, g1, jnp.where(row == i2, g2, 0.0))
    ind = jnp.where(row == i1, 1.0, jnp.where(row == i2, 1.0, 0.0))
    csum = jnp.dot(ind.astype(BF16), tri_ref[...], preferred_element_type=F32)
    pos_ref[0] = jnp.where(ind > 0.5, csum - 1.0, -1.0)
    cnt_ref[0] = jnp.broadcast_to(csum[:, tt - 1:tt], (N_EXPERTS, LANES))


def _router(x, wrt, tri):
    T, D = x.shape
    tt = MOE_TILE
    nt = T // tt
    tile = lambda: pl.BlockSpec((1, N_EXPERTS, tt), lambda m: (m, 0, 0))
    return pl.pallas_call(
        _router_kernel,
        grid=(nt,),
        in_specs=[pl.BlockSpec((tt, D), lambda m: (m, 0)), pl.BlockSpec((N_EXPERTS, D), lambda m: (0, 0)),
                  pl.BlockSpec((tt, tt), lambda m: (0, 0))],
        out_specs=[tile(), tile(), pl.BlockSpec((1, N_EXPERTS, LANES), lambda m: (m, 0, 0))],
        out_shape=[jax.ShapeDtypeStruct((nt, N_EXPERTS, tt), F32), jax.ShapeDtypeStruct((nt, N_EXPERTS, tt), F32),
                   jax.ShapeDtypeStruct((nt, N_EXPERTS, LANES), F32)],
        compiler_params=_params("arbitrary"),
        name="moe_router",
    )(x, wrt, tri)


def _moe_kernel(cnt_ref, xb_ref, gate_ref, pos_ref, wg_ref, wu_ref, wd_ref, o_ref, xg_ref, yacc_ref, gcol_ref, *, nf):
    t = pl.program_id(0)
    e = pl.program_id(1)
    f = pl.program_id(2)
    tt = xb_ref.shape[0]
    R = MOE_ROWS
    half = R // 2
    n = cnt_ref[t * N_EXPERTS + e]
    rem = n % R
    n_full = n // R + jnp.where(rem > half, 1, 0)
    has_tail = jnp.logical_and(rem > 0, rem <= half)
    tail_start = pl.multiple_of(n_full * R, half)
    pos_row = pos_ref[0, pl.ds(e, 1), :]

    def selection(start, rows):
        slot = lax.broadcasted_iota(jnp.int32, (rows, tt), 0) + start
        return jnp.where(pos_row == slot.astype(F32), 1.0, 0.0)

    def gather(start, rows):
        sel = selection(start, rows)
        dst = pl.ds(start, rows)
        xg_ref[dst, :] = jnp.dot(sel.astype(BF16), xb_ref[...], preferred_element_type=F32).astype(BF16)
        gate_row = gate_ref[0, pl.ds(e, 1), :]
        gcol_ref[dst, :] = jnp.broadcast_to(jnp.sum(sel * gate_row, axis=1, keepdims=True), (rows, LANES))

    def expert(start, rows):
        dst = pl.ds(start, rows)
        xc = xg_ref[dst, :]
        hg = jnp.dot(xc, wg_ref[0], preferred_element_type=F32)
        hu = jnp.dot(xc, wu_ref[0], preferred_element_type=F32)
        contrib = jnp.dot((hg * _sigmoid(hg) * hu).astype(BF16), wd_ref[0], preferred_element_type=F32)

        @pl.when(f == 0)
        def _():
            yacc_ref[dst, :] = contrib

        @pl.when(f > 0)
        def _():
            yacc_ref[dst, :] = yacc_ref[dst, :] + contrib

        @pl.when(f == nf - 1)
        def _():
            gate = jnp.concatenate([gcol_ref[dst, :]] * (o_ref.shape[1] // LANES), axis=1)
            ys = (yacc_ref[dst, :] * gate).astype(BF16)
            o_ref[...] = o_ref[...] + lax.dot_general(selection(start, rows).astype(BF16), ys, TN_DIMS,
                                                      preferred_element_type=F32)

    def over_chunks(fn):
        def body(r, carry):
            fn(pl.multiple_of(r * R, R), R)
            return carry
        lax.fori_loop(0, n_full, body, 0)

        @pl.when(has_tail)
        def _():
            fn(tail_start, half)

    @pl.when(jnp.logical_and(e == 0, f == 0))
    def _():
        o_ref[...] = jnp.zeros_like(o_ref)

    @pl.when(f == 0)
    def _():
        over_chunks(gather)

    over_chunks(expert)


def _moe(xb, gates_t, pos_t, counts, wg, wu, wd):
    T, D = xb.shape
    F = wg.shape[2]
    tt = MOE_TILE
    nf = 2
    tf = F // nf
    grid_spec = pltpu.PrefetchScalarGridSpec(
        num_scalar_prefetch=1,
        grid=(T // tt, N_EXPERTS, nf),
        in_specs=[pl.BlockSpec((tt, D), lambda m, e, f, c: (m, 0)),
                  pl.BlockSpec((1, N_EXPERTS, tt), lambda m, e, f, c: (m, 0, 0)),
                  pl.BlockSpec((1, N_EXPERTS, tt), lambda m, e, f, c: (m, 0, 0)),
                  pl.BlockSpec((1, D, tf), lambda m, e, f, c: (e, 0, f)),
                  pl.BlockSpec((1, D, tf), lambda m, e, f, c: (e, 0, f)),
                  pl.BlockSpec((1, tf, D), lambda m, e, f, c: (e, f, 0))],
        out_specs=pl.BlockSpec((tt, D), lambda m, e, f, c: (m, 0)),
        scratch_shapes=[pltpu.VMEM((tt, D), BF16), pltpu.VMEM((tt, D), F32), pltpu.VMEM((tt, LANES), F32)],
    )
    return pl.pallas_call(
        functools.partial(_moe_kernel, nf=nf),
        grid_spec=grid_spec,
        out_shape=jax.ShapeDtypeStruct((T, D), F32),
        compiler_params=_params("arbitrary", "arbitrary", "arbitrary"),
        name="moe_experts",
    )(counts, xb, gates_t, pos_t, wg, wu, wd)


def _resid_ln_kernel(x_ref, f_ref, g_ref, b_ref, o_ref):
    o_ref[...] = _layer_norm(DEEPNORM_ALPHA * x_ref[...] + f_ref[...], g_ref[...], b_ref[...])


def _resid_ln(x, f, g, b):
    T, D = x.shape
    tm = 512
    row = lambda: pl.BlockSpec((tm, D), lambda m: (m, 0))
    vec = lambda: pl.BlockSpec((1, D), lambda m: (0, 0))
    return pl.pallas_call(
        _resid_ln_kernel,
        grid=(T // tm,),
        in_specs=[row(), row(), vec(), vec()],
        out_specs=row(),
        out_shape=jax.ShapeDtypeStruct((T, D), F32),
        compiler_params=_params("arbitrary"),
        name="resid_ln",
    )(x, f, g, b)


def _block_diag(w):
    n, d, _ = w.shape
    out = jnp.zeros((n * d, n * d), w.dtype)
    for i in range(n):
        out = out.at[i * d:(i + 1) * d, i * d:(i + 1) * d].set(w[i])
    return out


def _rope_tables(positions):
    B, S = positions.shape
    half = ROPE_DIMS // 2
    inv_freq = ROPE_THETA ** (-jnp.arange(0, ROPE_DIMS, 2, dtype=F32) / ROPE_DIMS)
    d = jnp.arange(LANES) % HEAD_DIM
    ang = positions.astype(F32).reshape(B * S, 1) * inv_freq[d % half][None, :]
    cos = jnp.cos(ang)
    sin = jnp.sin(ang)
    c = jnp.where(d < ROPE_DIMS, cos, 1.0)
    s1 = jnp.where(d < half, -sin, 0.0)
    s2 = jnp.where((d >= half) & (d < ROPE_DIMS), sin, 0.0)
    return c, s1, s2


def _mixer_layer(x, xb, rope, B, S, w_in, conv_w, conv_b, ga_w, ga_b, gx_w, gx_b, lam,
                 mu, w0, w_up, a0, a_up, g_up, k_k, k_a, r_k, lnx_g, lnx_b, w_out, ln_g, ln_b):
    W = RWKV_WIDTH
    row = lambda t: t.reshape(1, -1)
    proj = _in_proj(xb, w_in.astype(BF16))
    q, k, vt, pen = _attn_prep(proj, *rope, B, S)
    att = _attention(q, k, vt, pen, B, S)
    lru = _rglru(proj, conv_w, row(conv_b), _block_diag(ga_w), row(ga_b), _block_diag(gx_w), row(gx_b), row(lam), B, S)
    head_ones = _block_diag(jnp.ones((RWKV_HEADS, HEAD_DIM, HEAD_DIM), F32))
    wlr = jnp.zeros((W, 3 * W), F32)
    wlr = wlr.at[0:64, 0:W].set(w_up).at[64:128, W:2 * W].set(a_up).at[128:256, 2 * W:3 * W].set(g_up)
    r, ld, k2, v, kn, a, g = _rwkv_prep(proj, row(mu), wlr, row(w0), row(a0), row(k_k), row(k_a), head_ones, B, S)
    y = _wkv(r, ld, k2, v, kn, a, B, S)
    rwk = _wkv_post(y, r, k2, v, g, row(r_k), row(lnx_g), row(lnx_b), head_ones.astype(BF16))
    wo = w_out.astype(BF16)
    return _outproj_ln(att, lru, rwk, x, wo[0:ATT_WIDTH], wo[ATT_WIDTH:ATT_WIDTH + LRU_WIDTH],
                       wo[ATT_WIDTH + LRU_WIDTH:], row(ln_g), row(ln_b))


def kernel(x, positions, w_in, lru_conv_w, lru_conv_b, lru_ga_w, lru_ga_b, lru_gx_w, lru_gx_b, lru_lambda, rwkv_mu, rwkv_w0, rwkv_w_up, rwkv_a0, rwkv_a_up, rwkv_g_up, rwkv_k_k, rwkv_k_a, rwkv_r_k, rwkv_lnx_g, rwkv_lnx_b, w_out, ln1_g, ln1_b, ffn_w_gate, ffn_w_up, ffn_w_down, moe_router, moe_w_gate, moe_w_up, moe_w_down, ln2_g, ln2_b):
    B, S, D = x.shape
    T = B * S
    rope = _rope_tables(positions)
    xf = x.reshape(T, D)
    xb = xf.astype(BF16)
    for l in range(DEPTH):
        xf, xb = _mixer_layer(xf, xb, rope, B, S, w_in[l], lru_conv_w[l], lru_conv_b[l], lru_ga_w[l], lru_ga_b[l],
                              lru_gx_w[l], lru_gx_b[l], lru_lambda[l], rwkv_mu[l], rwkv_w0[l], rwkv_w_up[l],
                              rwkv_a0[l], rwkv_a_up[l], rwkv_g_up[l], rwkv_k_k[l], rwkv_k_a[l], rwkv_r_k[l],
                              rwkv_lnx_g[l], rwkv_lnx_b[l], w_out[l], ln1_g[l], ln1_b[l])
        g2 = ln2_g[l].reshape(1, D)
        b2 = ln2_b[l].reshape(1, D)
        if l % 2 == 0:
            i = l // 2
            xf, xb = _ffn_ln(xb, xf, ffn_w_gate[i].astype(BF16), ffn_w_up[i].astype(BF16),
                             ffn_w_down[i].astype(BF16), g2, b2)
        else:
            i = l // 2
            idx = jnp.arange(MOE_TILE)
            tri = (idx[:, None] <= idx[None, :]).astype(BF16)
            gates_t, pos_t, cnt = _router(xf, moe_router[i].T, tri)
            counts = cnt[:, :, 0].astype(jnp.int32).reshape(-1)
            f = _moe(xb, gates_t, pos_t, counts, moe_w_gate[i].astype(BF16), moe_w_up[i].astype(BF16),
                     moe_w_down[i].astype(BF16))
            xf = _resid_ln(xf, f, g2, b2)
            xb = xf.astype(BF16)
    return xf.reshape(B, S, D)
```

```python
import functools

import jax
import jax.numpy as jnp
from jax import lax
from jax.experimental import pallas as pl
from jax.experimental.pallas import tpu as pltpu

F32 = jnp.float32
BF16 = jnp.bfloat16
HI = lax.Precision.HIGHEST
NT_DIMS = (((1,), (1,)), ((), ()))
TN_DIMS = (((0,), (0,)), ((), ()))

HEAD_DIM = 64
ATT_HEADS = 8
ATT_WIDTH = ATT_HEADS * HEAD_DIM
LRU_WIDTH = 256
RWKV_HEADS = 4
RWKV_WIDTH = 256
ROPE_DIMS = 16
ROPE_THETA = 500000.0
MOBA_BLOCK = 256
MOBA_TOPK = 3
CONV_WIDTH = 4
LRU_C = 8.0
RWKV_LN_EPS = 64e-5
N_EXPERTS = 8
LN_EPS = 1e-5
DEPTH = 2
DEEPNORM_ALPHA = (2 * DEPTH) ** 0.25
NEG_INF = -1e30
LOG2_E = 1.4426950408889634

LANES = 128
VMEM_LIMIT = 56 * 1024 * 1024

ATTN_STREAMS = 4
WKV_CHUNK = 64
WKV_GROUP = 8
LRU_CHUNK = 256
RWKV_PREP_ROWS = 512
MOE_TILE = 1024
MOE_ROWS = 256


def _params(*sem):
    return pltpu.CompilerParams(dimension_semantics=sem, vmem_limit_bytes=VMEM_LIMIT)


def _sigmoid(x):
    return 1.0 / (1.0 + jnp.exp(-x))


def _softplus(x):
    return jnp.maximum(x, 0.0) + jnp.log1p(jnp.exp(-jnp.abs(x)))


def _expm1(z):
    u = jnp.exp(z)
    um1 = u - 1.0
    return jnp.where(u == 1.0, z, jnp.where(um1 == -1.0, -1.0, um1 * z / jnp.log(u)))


def _layer_norm(y, g, b):
    m = jnp.mean(y, axis=-1, keepdims=True)
    d = y - m
    var = jnp.mean(d * d, axis=-1, keepdims=True)
    return d * lax.rsqrt(var + LN_EPS) * g + b


def _matmul_kernel(x_ref, w_ref, o_ref):
    o_ref[...] = jnp.dot(x_ref[...], w_ref[...], preferred_element_type=F32)


def _in_proj(xb, w):
    T, D = xb.shape
    N = w.shape[1]
    tm, tn = 1024, 1024
    return pl.pallas_call(
        _matmul_kernel,
        grid=(N // tn, T // tm),
        in_specs=[pl.BlockSpec((tm, D), lambda n, m: (m, 0)),
                  pl.BlockSpec((D, tn), lambda n, m: (0, n))],
        out_specs=pl.BlockSpec((tm, tn), lambda n, m: (m, n)),
        out_shape=jax.ShapeDtypeStruct((T, N), F32),
        compiler_params=_params("arbitrary", "arbitrary"),
        name="in_proj",
    )(xb, w)


def _attn_prep_kernel(proj_ref, c_ref, s1_ref, s2_ref, q_ref, k_ref, vt_ref, pen_ref, km_ref, *, nb):
    i = pl.program_id(1)

    @pl.when(i == 0)
    def _():
        km_ref[...] = jnp.zeros_like(km_ref)

    c = c_ref[...]
    s1 = s1_ref[...]
    s2 = s2_ref[...]

    def rope(xt):
        return xt * c + pltpu.roll(xt, LANES - ROPE_DIMS // 2, 1) * s1 + pltpu.roll(xt, ROPE_DIMS // 2, 1) * s2

    q_tiles = []
    for ct in range(ATT_WIDTH // LANES):
        lo, hi = ct * LANES, (ct + 1) * LANES
        qr = rope(proj_ref[:, lo:hi])
        q_tiles.append(qr)
        q_ref[:, lo:hi] = (qr * (HEAD_DIM ** -0.5 * LOG2_E)).astype(BF16)
        kr = rope(proj_ref[:, ATT_WIDTH + lo:ATT_WIDTH + hi])
        k_ref[0, 0, :, lo:hi] = kr.astype(BF16)
        km_row = lax.broadcasted_iota(jnp.int32, (nb, LANES), 0)
        km_ref[:, lo:hi] = jnp.where(km_row == i, jnp.mean(kr, axis=0, keepdims=True), km_ref[:, lo:hi])
    vt_ref[0, 0] = proj_ref[:, 2 * ATT_WIDTH:3 * ATT_WIDTH].T.astype(BF16)

    q_rot = jnp.concatenate(q_tiles, axis=1)
    km = km_ref[...]
    km_rows = jnp.concatenate([km] * ATT_HEADS, axis=0)
    row_head = lax.broadcasted_iota(jnp.int32, km_rows.shape, 0) // nb
    lane_head = lax.broadcasted_iota(jnp.int32, km_rows.shape, 1) // HEAD_DIM
    g_all = lax.dot_general(jnp.where(row_head == lane_head, km_rows, 0.0), q_rot, NT_DIMS,
                            precision=HI, preferred_element_type=F32)
    n_iota = lax.broadcasted_iota(jnp.int32, (nb, MOBA_BLOCK), 0)
    past = n_iota < i
    for h in range(ATT_HEADS):
        g = g_all[h * nb:(h + 1) * nb, :]
        rank = jnp.zeros((nb, MOBA_BLOCK), F32)
        for m in range(nb):
            gm = g[m:m + 1, :]
            beats = jnp.where(gm > g, 1.0, jnp.where(gm == g, jnp.where(n_iota > m, 1.0, 0.0), 0.0))
            rank = rank + jnp.where(m < i, beats, 0.0)
        pen = jnp.where(past, jnp.where(rank < float(MOBA_TOPK), 0.0, NEG_INF), NEG_INF)
        half = (h % 2) * MOBA_BLOCK
        pen_ref[0, h // 2, :, half:half + MOBA_BLOCK] = pen


def _attn_prep(proj, rope_c, rope_s1, rope_s2, B, S):
    T = B * S
    nb = S // MOBA_BLOCK
    blk = MOBA_BLOCK
    return pl.pallas_call(
        functools.partial(_attn_prep_kernel, nb=nb),
        grid=(B, nb),
        in_specs=[pl.BlockSpec((blk, 3 * ATT_WIDTH), lambda b, i: (b * nb + i, 0)),
                  pl.BlockSpec((blk, LANES), lambda b, i: (b * nb + i, 0)),
                  pl.BlockSpec((blk, LANES), lambda b, i: (b * nb + i, 0)),
                  pl.BlockSpec((blk, LANES), lambda b, i: (b * nb + i, 0))],
        out_specs=[pl.BlockSpec((blk, ATT_WIDTH), lambda b, i: (b * nb + i, 0)),
                   pl.BlockSpec((1, 1, blk, ATT_WIDTH), lambda b, i: (b, i, 0, 0)),
                   pl.BlockSpec((1, 1, ATT_WIDTH, blk), lambda b, i: (b, i, 0, 0)),
                   pl.BlockSpec((1, ATT_HEADS // 2, nb, 2 * blk), lambda b, i: (b, 0, 0, i))],
        out_shape=[jax.ShapeDtypeStruct((T, ATT_WIDTH), BF16),
                   jax.ShapeDtypeStruct((B, nb, blk, ATT_WIDTH), BF16),
                   jax.ShapeDtypeStruct((B, nb, ATT_WIDTH, blk), BF16),
                   jax.ShapeDtypeStruct((B, ATT_HEADS // 2, nb, 2 * S), F32)],
        scratch_shapes=[pltpu.VMEM((nb, ATT_WIDTH), F32)],
        compiler_params=_params("arbitrary", "arbitrary"),
        name="attn_prep",
    )(proj, rope_c, rope_s1, rope_s2)


def _attn_kernel(q_ref, k_ref, vt_ref, pen_ref, o_ref):
    i = pl.program_id(2)
    blk = MOBA_BLOCK
    streams = range(ATTN_STREAMS)
    each = lambda fn, *xs: [fn(*(x[t] for x in xs)) for t in streams]
    lanes = lambda t: slice(t * LANES, (t + 1) * LANES)
    lane_head = lax.broadcasted_iota(jnp.int32, (blk, LANES), 1) // HEAD_DIM
    kidx = lax.broadcasted_iota(jnp.int32, (blk, 2 * blk), 0)
    qidx = lax.broadcasted_iota(jnp.int32, (blk, 2 * blk), 1) % blk
    nt = lambda a, b: lax.dot_general(a, b, NT_DIMS, preferred_element_type=F32)
    nn = lambda a, b: jnp.dot(a, b, preferred_element_type=F32)
    col_max = lambda x: jnp.max(x, axis=0, keepdims=True)
    ones_tile = jnp.ones((16, blk), BF16)

    def stacked_q(t):
        q = q_ref[:, lanes(t)]
        zero = jnp.zeros_like(q)
        return jnp.concatenate([jnp.where(lane_head == 0, q, zero), jnp.where(lane_head == 1, q, zero)], axis=0)

    qs = [stacked_q(t) for t in streams]
    s = [jnp.where(kidx <= qidx, nt(k_ref[0, i, :, lanes(t)], qs[t]), NEG_INF) for t in streams]
    m0 = each(col_max, s)
    p = each(lambda x, m: jnp.exp2(x - m), s, m0)
    pv = [nn(jnp.concatenate([vt_ref[0, i, lanes(t), :], ones_tile], axis=0), p[t].astype(BF16)) for t in streams]
    l0 = [x[LANES:LANES + 1] for x in pv]
    acc0 = [x[0:LANES] for x in pv]

    def body(jj, carry):
        m, l, acc = carry
        j0 = 2 * jj
        j1 = j0 + 1
        s0 = [nt(k_ref[0, j0, :, lanes(t)], qs[t]) + pen_ref[0, t, pl.ds(j0, 1), :] for t in streams]
        s1 = [nt(k_ref[0, j1, :, lanes(t)], qs[t]) + pen_ref[0, t, pl.ds(j1, 1), :] for t in streams]
        m_new = each(lambda mo, a, b: jnp.maximum(mo, jnp.maximum(col_max(a), col_max(b))), m, s0, s1)
        alpha = each(lambda mo, mn: jnp.exp2(mo - mn), m, m_new)
        p0 = each(lambda a, mn: jnp.exp2(a - mn), s0, m_new)
        p1 = each(lambda a, mn: jnp.exp2(a - mn), s1, m_new)
        pb0 = [x.astype(BF16) for x in p0]
        pb1 = [x.astype(BF16) for x in p1]
        pv0 = [nn(jnp.concatenate([vt_ref[0, j0, lanes(t), :], ones_tile], axis=0), pb0[t]) for t in streams]
        pv1 = [nn(jnp.concatenate([vt_ref[0, j1, lanes(t), :], ones_tile], axis=0), pb1[t]) for t in streams]
        l = each(lambda al, lo, a, b: al * lo + a[LANES:LANES + 1] + b[LANES:LANES + 1], alpha, l, pv0, pv1)
        acc = each(lambda ac, al, a, b: ac * al + a[0:LANES] + b[0:LANES], acc, alpha, pv0, pv1)
        return tuple(m_new), tuple(l), tuple(acc)

    _, l, acc = lax.fori_loop(0, (i + 1) // 2, body, (tuple(m0), tuple(l0), tuple(acc0)))
    row = lax.broadcasted_iota(jnp.int32, (LANES, blk), 0)
    for t in streams:
        out = acc[t] / l[t]
        o_ref[:, lanes(t)] = jnp.where(row < HEAD_DIM, out[:, 0:blk], out[:, blk:2 * blk]).T


def _attention(q, k, vt, pen, B, S):
    T = B * S
    nb = S // MOBA_BLOCK
    blk = MOBA_BLOCK
    w = ATTN_STREAMS * LANES
    n_groups = ATT_WIDTH // w
    return pl.pallas_call(
        _attn_kernel,
        grid=(B, n_groups, nb),
        in_specs=[pl.BlockSpec((blk, w), lambda b, hp, i: (b * nb + i, hp)),
                  pl.BlockSpec((1, nb, blk, w), lambda b, hp, i: (b, 0, 0, hp)),
                  pl.BlockSpec((1, nb, w, blk), lambda b, hp, i: (b, 0, hp, 0)),
                  pl.BlockSpec((1, ATTN_STREAMS, nb, 2 * blk), lambda b, hp, i: (b, hp, 0, i))],
        out_specs=pl.BlockSpec((blk, w), lambda b, hp, i: (b * nb + i, hp)),
        out_shape=jax.ShapeDtypeStruct((T, ATT_WIDTH), F32),
        compiler_params=_params("arbitrary", "arbitrary", "arbitrary"),
        name="moba_attn",
    )(q, k, vt, pen)


def _lru_kernel(p_ref, cw_ref, cb_ref, ga_ref, gab_ref, gx_ref, gxb_ref, lam_ref, o_ref, xbuf, hc):
    t = pl.program_id(1)
    tc = LRU_CHUNK
    pad = 8

    @pl.when(t == 0)
    def _():
        xbuf[0:pad, :] = jnp.zeros((pad, LRU_WIDTH), F32)
        hc[...] = jnp.zeros_like(hc)

    @pl.when(t > 0)
    def _():
        xbuf[0:pad, :] = xbuf[tc:tc + pad, :]

    x = p_ref[:, 0:LRU_WIDTH]
    gate = p_ref[:, LRU_WIDTH:2 * LRU_WIDTH]
    xbuf[pad:pad + tc, :] = x
    xc = cb_ref[...] + x * cw_ref[CONV_WIDTH - 1:CONV_WIDTH, :]
    for j in range(CONV_WIDTH - 1):
        back = CONV_WIDTH - 1 - j
        xc = xc + xbuf[pad - back:pad - back + tc, :] * cw_ref[j:j + 1, :]

    r = _sigmoid(jnp.dot(xc, ga_ref[...], precision=HI, preferred_element_type=F32) + gab_ref[...])
    ig = _sigmoid(jnp.dot(xc, gx_ref[...], precision=HI, preferred_element_type=F32) + gxb_ref[...])
    log_a = -LRU_C * r * _softplus(-lam_ref[...])
    a = jnp.exp(log_a)
    u = jnp.sqrt(-_expm1(2.0 * log_a)) * (ig * xc)

    rows = lax.broadcasted_iota(jnp.int32, (tc, LRU_WIDTH), 0)
    d = 1
    while d < tc:
        keep = rows >= d
        a_prev = jnp.where(keep, pltpu.roll(a, d, 0), 1.0)
        u_prev = jnp.where(keep, pltpu.roll(u, d, 0), 0.0)
        u = a * u_prev + u
        a = a * a_prev
        d *= 2
    h = u + a * hc[...]
    hc[...] = h[tc - 1:tc, :]
    gl = 0.5 * gate * (1.0 + jnp.tanh(0.7978845608028654 * (gate + 0.044715 * gate * gate * gate)))
    o_ref[...] = h * gl


def _rglru(proj, cw, cb, ga, gab, gx, gxb, lam, B, S):
    T = B * S
    tc = LRU_CHUNK
    nt = S // tc
    col = (3 * ATT_WIDTH) // (2 * LRU_WIDTH)
    vec = lambda: pl.BlockSpec((1, LRU_WIDTH), lambda b, t: (0, 0))
    mat = lambda: pl.BlockSpec((LRU_WIDTH, LRU_WIDTH), lambda b, t: (0, 0))
    return pl.pallas_call(
        _lru_kernel,
        grid=(B, nt),
        in_specs=[pl.BlockSpec((tc, 2 * LRU_WIDTH), lambda b, t: (b * nt + t, col)),
                  pl.BlockSpec((CONV_WIDTH, LRU_WIDTH), lambda b, t: (0, 0)),
                  vec(), mat(), vec(), mat(), vec(), vec()],
        out_specs=pl.BlockSpec((tc, LRU_WIDTH), lambda b, t: (b * nt + t, 0)),
        out_shape=jax.ShapeDtypeStruct((T, LRU_WIDTH), F32),
        scratch_shapes=[pltpu.VMEM((tc + 8, LRU_WIDTH), F32), pltpu.VMEM((1, LRU_WIDTH), F32)],
        compiler_params=_params("arbitrary", "arbitrary"),
        name="rglru",
    )(proj, cw, cb, ga, gab, gx, gxb, lam)


def _rwkv_prep_kernel(p_ref, pp_ref, mu_ref, wlr_ref, w0_ref, a0_ref, kk_ref, ka_ref, ones_ref,
                      r_ref, ld_ref, k2_ref, v_ref, kn_ref, a_ref, g_ref):
    i = pl.program_id(1)
    W = RWKV_WIDTH
    p = p_ref[...]
    rows = lax.broadcasted_iota(jnp.int32, p.shape, 0)
    prev_last = jnp.where(i == 0, 0.0, pp_ref[7:8, :])
    p_prev = jnp.where(rows == 0, prev_last, pltpu.roll(p, 1, 0))
    pf = p + (p_prev - p) * mu_ref[...]
    r = pf[:, 0:W]
    k = pf[:, W:2 * W]
    v = pf[:, 2 * W:3 * W]
    z = pf[:, 3 * W:4 * W]
    lane = lax.broadcasted_iota(jnp.int32, z.shape, 1)
    zz = jnp.where(lane < 64, jnp.tanh(z), jnp.where(lane < 128, z, _sigmoid(z)))
    lr = jnp.dot(zz, wlr_ref[...], precision=HI, preferred_element_type=F32)
    w = -_softplus(-(w0_ref[...] + lr[:, 0:W])) - 0.5
    a = _sigmoid(a0_ref[...] + lr[:, W:2 * W])
    kk = k * kk_ref[...]
    ssq = jnp.dot(kk * kk, ones_ref[...], precision=HI, preferred_element_type=F32)
    kk = kk / jnp.maximum(jnp.sqrt(ssq), 1e-12)
    r_ref[...] = r
    ld_ref[...] = -jnp.exp(w)
    k2_ref[...] = k * (1.0 + (a - 1.0) * ka_ref[...])
    v_ref[...] = v
    kn_ref[...] = kk
    a_ref[...] = a
    g_ref[...] = lr[:, 2 * W:3 * W]


def _rwkv_prep(proj, mu, wlr, w0, a0, k_k, k_a, head_ones, B, S):
    T = B * S
    tr = RWKV_PREP_ROWS
    nt = S // tr
    W = RWKV_WIDTH
    col = 2
    vec = lambda: pl.BlockSpec((1, W), lambda b, t: (0, 0))
    out = lambda: pl.BlockSpec((tr, W), lambda b, t: (b * nt + t, 0))
    return pl.pallas_call(
        _rwkv_prep_kernel,
        grid=(B, nt),
        in_specs=[pl.BlockSpec((tr, 4 * W), lambda b, t: (b * nt + t, col)),
                  pl.BlockSpec((8, 4 * W), lambda b, t: (jnp.maximum((b * nt + t) * (tr // 8) - 1, 0), col)),
                  pl.BlockSpec((1, 4 * W), lambda b, t: (0, 0)),
                  pl.BlockSpec((W, 3 * W), lambda b, t: (0, 0)),
                  vec(), vec(), vec(), vec(),
                  pl.BlockSpec((W, W), lambda b, t: (0, 0))],
        out_specs=[out() for _ in range(7)],
        out_shape=[jax.ShapeDtypeStruct((T, W), F32) for _ in range(7)],
        compiler_params=_params("arbitrary", "arbitrary"),
        name="rwkv_prep",
    )(proj, proj, mu, wlr, w0, a0, k_k, k_a, head_ones)


def _wkv_kernel(r_ref, ld_ref, k2_ref, v_ref, kn_ref, a_ref, y_ref, st_ref):
    c = pl.program_id(1)
    C = WKV_CHUNK
    W = RWKV_WIDTH
    H = RWKV_HEADS

    @pl.when(c == 0)
    def _():
        st_ref[...] = jnp.zeros_like(st_ref)

    def mm(x, y):
        return jnp.dot(x.astype(BF16), y.astype(BF16), preferred_element_type=F32)

    def mm_nt(x, y):
        return lax.dot_general(x.astype(BF16), y.astype(BF16), NT_DIMS, preferred_element_type=F32)

    def mm_tn(x, y):
        return lax.dot_general(x.astype(BF16), y.astype(BF16), TN_DIMS, preferred_element_type=F32)

    tr = lax.broadcasted_iota(jnp.int32, (C, C), 0)
    tc = lax.broadcasted_iota(jnp.int32, (C, C), 1)
    tri = jnp.where(tr >= tc, 1.0, 0.0)
    row = lax.broadcasted_iota(jnp.int32, (W, W), 0)
    colm = lax.broadcasted_iota(jnp.int32, (W, W), 1)
    same_head = (row // C) == (colm // HEAD_DIM)
    strict = lambda x: jnp.where(same_head, jnp.where(row > colm, x, 0.0), 0.0)
    incl = lambda x: jnp.where(same_head, jnp.where(row >= colm, x, 0.0), 0.0)
    same16 = (row // 16) == (colm // 16)
    eye = jnp.where(row == colm, 1.0, 0.0)

    def stack(x):
        return jnp.where(same_head, jnp.concatenate([x] * H, axis=0), 0.0)

    G = range(WKV_GROUP)
    each = lambda fn, *xs: [fn(*(x[g] for x in xs)) for g in G]

    r = [r_ref[g] for g in G]
    ld = [ld_ref[g] for g in G]
    k2 = [k2_ref[g] for g in G]
    v_s = [stack(v_ref[g]) for g in G]
    kn = [kn_ref[g] for g in G]
    b_s = [kn[g] * a_ref[g] for g in G]
    cl = each(lambda x: jnp.dot(tri, x, precision=HI, preferred_element_type=F32), ld)
    cl_end = [x[C - 1:C, :] for x in cl]
    e_neg = [jnp.exp(-x) for x in cl]
    e_end = each(lambda ce, x: jnp.exp(ce - x), cl_end, cl)
    ar_s = [jnp.concatenate([stack(-kn[g] * jnp.exp(cl[g] - ld[g])), stack(r[g] * jnp.exp(cl[g]))], axis=0) for g in G]
    bk_s = [jnp.concatenate([stack(b_s[g] * e_neg[g]), stack(k2[g] * e_neg[g])], axis=0) for g in G]
    end_s = [jnp.concatenate([stack(b_s[g] * e_end[g]), stack(k2[g] * e_end[g])], axis=0) for g in G]

    prod = each(mm_nt, ar_s, bk_s)
    n_ab = [strict(p[0:W, 0:W]) for p in prod]
    a_ak = [strict(p[0:W, W:2 * W]) for p in prod]
    a_rb = [incl(p[W:2 * W, 0:W]) for p in prod]
    a_rk = [incl(p[W:2 * W, W:2 * W]) for p in prod]

    nd = [jnp.where(same16, n, 0.0) for n in n_ab]
    lo = each(lambda n, d: n - d, n_ab, nd)
    n2 = each(mm, nd, nd)
    p1 = [eye + d for d in nd]
    n4 = each(mm, n2, n2)
    p2 = each(lambda p, n: p + mm(p, n), p1, n2)
    n8 = each(mm, n4, n4)
    p3 = each(lambda p, n: p + mm(p, n), p2, n4)
    dinv = each(lambda p, n: p + mm(p, n), p3, n8)
    x1 = each(mm, dinv, lo)
    x2 = each(mm, x1, x1)
    y1 = [eye + x for x in x1]
    m_inv = each(lambda y, x: y + mm(y, x), y1, x2)
    tinv = each(mm, m_inv, dinv)

    st = [st_ref[g] for g in G]
    ar_st = each(mm_nt, ar_s, st)
    akv = each(mm, a_ak, v_s)
    u_s = each(lambda t, a, b: mm(t, a[0:W] + b), tinv, ar_st, akv)
    rkv = each(mm, a_rk, v_s)
    y_s = each(lambda a, m, u, b: a[W:2 * W] + mm(m, u) + b, ar_st, a_rb, u_s, rkv)
    upd = each(lambda u, v, e: mm_tn(jnp.concatenate([u, v], axis=0), e), u_s, v_s, end_s)
    for g in G:
        y = y_s[g][0:C, :]
        for h in range(1, H):
            y = y + y_s[g][h * C:(h + 1) * C, :]
        y_ref[g] = y
        st_ref[g] = st[g] * jnp.exp(cl_end[g]) + upd[g]


def _wkv(r, ld, k2, v, kn, a, B, S):
    C = WKV_CHUNK
    nc = S // C
    W = RWKV_WIDTH
    G = WKV_GROUP
    blk = lambda: pl.BlockSpec((G, C, W), lambda b, c: (b, c, 0))
    seq = lambda t: t.reshape(B, S, W)
    y = pl.pallas_call(
        _wkv_kernel,
        grid=(B // G, nc),
        in_specs=[blk() for _ in range(6)],
        out_specs=blk(),
        out_shape=jax.ShapeDtypeStruct((B, S, W), F32),
        scratch_shapes=[pltpu.VMEM((G, W, W), F32)],
        compiler_params=_params("arbitrary", "arbitrary"),
        name="wkv7",
    )(seq(r), seq(ld), seq(k2), seq(v), seq(kn), seq(a))
    return y.reshape(B * S, W)


def _wkv_post_kernel(y_ref, r_ref, k2_ref, v_ref, g_ref, rk_ref, lg_ref, lb_ref, ones_ref, o_ref):
    ones = ones_ref[...]

    def head_sum(x):
        hi = x.astype(BF16)
        lo = (x - hi.astype(F32)).astype(BF16)
        return (jnp.dot(hi, ones, preferred_element_type=F32) + jnp.dot(lo, ones, preferred_element_type=F32))

    y = y_ref[...]
    r = r_ref[...]
    k2 = k2_ref[...]
    inv_n = 1.0 / HEAD_DIM
    d = y - head_sum(y) * inv_n
    var = head_sum(d * d) * inv_n
    yn = d * lax.rsqrt(var + RWKV_LN_EPS) * lg_ref[...] + lb_ref[...]
    bonus = head_sum(r * k2 * rk_ref[...]) * v_ref[...]
    o_ref[...] = (yn + bonus) * g_ref[...]


def _wkv_post(y, r, k2, v, g, r_k, lnx_g, lnx_b, head_ones_bf16):
    T, W = y.shape
    tm = 512
    blk = lambda: pl.BlockSpec((tm, W), lambda m: (m, 0))
    vec = lambda: pl.BlockSpec((1, W), lambda m: (0, 0))
    return pl.pallas_call(
        _wkv_post_kernel,
        grid=(T // tm,),
        in_specs=[blk() for _ in range(5)] + [vec(), vec(), vec(), pl.BlockSpec((W, W), lambda m: (0, 0))],
        out_specs=blk(),
        out_shape=jax.ShapeDtypeStruct((T, W), F32),
        compiler_params=_params("arbitrary"),
        name="wkv_post",
    )(y, r, k2, v, g, r_k, lnx_g, lnx_b, head_ones_bf16)


def _outproj_ln_kernel(att_ref, lru_ref, rwk_ref, x_ref, wa_ref, wl_ref, wr_ref, g_ref, b_ref, o_ref, ob_ref):
    h = jnp.dot(att_ref[...].astype(BF16), wa_ref[...], preferred_element_type=F32)
    h = h + jnp.dot(lru_ref[...].astype(BF16), wl_ref[...], preferred_element_type=F32)
    h = h + jnp.dot(rwk_ref[...].astype(BF16), wr_ref[...], preferred_element_type=F32)
    y = _layer_norm(DEEPNORM_ALPHA * x_ref[...] + h, g_ref[...], b_ref[...])
    o_ref[...] = y
    ob_ref[...] = y.astype(BF16)


def _outproj_ln(att, lru, rwk, x, wa, wl, wr, g, b):
    T, D = x.shape
    tm = 512
    row = lambda w: pl.BlockSpec((tm, w), lambda m: (m, 0))
    full = lambda a: pl.BlockSpec(a.shape, lambda m: (0, 0))
    return pl.pallas_call(
        _outproj_ln_kernel,
        grid=(T // tm,),
        in_specs=[row(ATT_WIDTH), row(LRU_WIDTH), row(RWKV_WIDTH), row(D), full(wa), full(wl), full(wr), full(g), full(b)],
        out_specs=[row(D), row(D)],
        out_shape=[jax.ShapeDtypeStruct((T, D), F32), jax.ShapeDtypeStruct((T, D), BF16)],
        compiler_params=_params("arbitrary"),
        name="outproj_ln",
    )(att, lru, rwk, x, wa, wl, wr, g, b)


def _ffn_kernel(xb_ref, x_ref, wg_ref, wu_ref, wd_ref, g_ref, b_ref, o_ref, ob_ref, acc_ref, *, nf):
    f = pl.program_id(1)
    xb = xb_ref[...]
    hg = jnp.dot(xb, wg_ref[...], preferred_element_type=F32)
    hu = jnp.dot(xb, wu_ref[...], preferred_element_type=F32)
    hh = (hg * _sigmoid(hg) * hu).astype(BF16)
    contrib = jnp.dot(hh, wd_ref[...], preferred_element_type=F32)

    @pl.when(f == 0)
    def _():
        acc_ref[...] = contrib

    @pl.when(f > 0)
    def _():
        acc_ref[...] = acc_ref[...] + contrib

    @pl.when(f == nf - 1)
    def _():
        y = _layer_norm(DEEPNORM_ALPHA * x_ref[...] + acc_ref[...], g_ref[...], b_ref[...])
        o_ref[...] = y
        ob_ref[...] = y.astype(BF16)


def _ffn_ln(xb, x, wg, wu, wd, g, b):
    T, D = x.shape
    F = wg.shape[1]
    tm = 512
    nf = 2
    tf = F // nf
    return pl.pallas_call(
        functools.partial(_ffn_kernel, nf=nf),
        grid=(T // tm, nf),
        in_specs=[pl.BlockSpec((tm, D), lambda m, f: (m, 0)),
                  pl.BlockSpec((tm, D), lambda m, f: (m, 0)),
                  pl.BlockSpec((D, tf), lambda m, f: (0, f)),
                  pl.BlockSpec((D, tf), lambda m, f: (0, f)),
                  pl.BlockSpec((tf, D), lambda m, f: (f, 0)),
                  pl.BlockSpec((1, D), lambda m, f: (0, 0)),
                  pl.BlockSpec((1, D), lambda m, f: (0, 0))],
        out_specs=[pl.BlockSpec((tm, D), lambda m, f: (m, 0)), pl.BlockSpec((tm, D), lambda m, f: (m, 0))],
        out_shape=[jax.ShapeDtypeStruct((T, D), F32), jax.ShapeDtypeStruct((T, D), BF16)],
        scratch_shapes=[pltpu.VMEM((tm, D), F32)],
        compiler_params=_params("arbitrary", "arbitrary"),
        name="ffn_ln",
    )(xb, x, wg, wu, wd, g, b)


def _router_kernel(x_ref, wrt_ref, tri_ref, g_ref, pos_ref, cnt_ref):
    tt = x_ref.shape[0]
    logits = lax.dot_general(wrt_ref[...], x_ref[...], NT_DIMS, precision=HI, preferred_element_type=F32)
    row = lax.broadcasted_iota(jnp.int32, logits.shape, 0).astype(F32)
    m1 = jnp.max(logits, axis=0, keepdims=True)
    i1 = jnp.min(jnp.where(logits == m1, row, float(N_EXPERTS)), axis=0, keepdims=True)
    rest = jnp.where(row == i1, -jnp.inf, logits)
    m2 = jnp.max(rest, axis=0, keepdims=True)
    i2 = jnp.min(jnp.where(rest == m2, row, float(N_EXPERTS)), axis=0, keepdims=True)
    e = jnp.exp(m2 - m1)
    g1 = 1.0 / (1.0 + e)
    g2 = e / (1.0 + e)
    g_ref[0] = jnp.where(row == i1, g1, jnp.where(row == i2, g2, 0.0))
    ind = jnp.where(row == i1, 1.0, jnp.where(row == i2, 1.0, 0.0))
    csum = jnp.dot(ind.astype(BF16), tri_ref[...], preferred_element_type=F32)
    pos_ref[0] = jnp.where(ind > 0.5, csum - 1.0, -1.0)
    cnt_ref[0] = jnp.broadcast_to(csum[:, tt - 1:tt], (N_EXPERTS, LANES))


def _router(x, wrt, tri):
    T, D = x.shape
    tt = MOE_TILE
    nt = T // tt
    tile = lambda: pl.BlockSpec((1, N_EXPERTS, tt), lambda m: (m, 0, 0))
    return pl.pallas_call(
        _router_kernel,
        grid=(nt,),
        in_specs=[pl.BlockSpec((tt, D), lambda m: (m, 0)), pl.BlockSpec((N_EXPERTS, D), lambda m: (0, 0)),
                  pl.BlockSpec((tt, tt), lambda m: (0, 0))],
        out_specs=[tile(), tile(), pl.BlockSpec((1, N_EXPERTS, LANES), lambda m: (m, 0, 0))],
        out_shape=[jax.ShapeDtypeStruct((nt, N_EXPERTS, tt), F32), jax.ShapeDtypeStruct((nt, N_EXPERTS, tt), F32),
                   jax.ShapeDtypeStruct((nt, N_EXPERTS, LANES), F32)],
        compiler_params=_params("arbitrary"),
        name="moe_router",
    )(x, wrt, tri)


def _moe_kernel(cnt_ref, xb_ref, gate_ref, pos_ref, wg_ref, wu_ref, wd_ref, o_ref, xg_ref, yacc_ref, gcol_ref, *, nf):
    t = pl.program_id(0)
    e = pl.program_id(1)
    f = pl.program_id(2)
    tt = xb_ref.shape[0]
    R = MOE_ROWS
    half = R // 2
    n = cnt_ref[t * N_EXPERTS + e]
    rem = n % R
    n_full = n // R + jnp.where(rem > half, 1, 0)
    has_tail = jnp.logical_and(rem > 0, rem <= half)
    tail_start = pl.multiple_of(n_full * R, half)
    pos_row = pos_ref[0, pl.ds(e, 1), :]

    def selection(start, rows):
        slot = lax.broadcasted_iota(jnp.int32, (rows, tt), 0) + start
        return jnp.where(pos_row == slot.astype(F32), 1.0, 0.0)

    def gather(start, rows):
        sel = selection(start, rows)
        dst = pl.ds(start, rows)
        xg_ref[dst, :] = jnp.dot(sel.astype(BF16), xb_ref[...], preferred_element_type=F32).astype(BF16)
        gate_row = gate_ref[0, pl.ds(e, 1), :]
        gcol_ref[dst, :] = jnp.broadcast_to(jnp.sum(sel * gate_row, axis=1, keepdims=True), (rows, LANES))

    def expert(start, rows):
        dst = pl.ds(start, rows)
        xc = xg_ref[dst, :]
        hg = jnp.dot(xc, wg_ref[0], preferred_element_type=F32)
        hu = jnp.dot(xc, wu_ref[0], preferred_element_type=F32)
        contrib = jnp.dot((hg * _sigmoid(hg) * hu).astype(BF16), wd_ref[0], preferred_element_type=F32)

        @pl.when(f == 0)
        def _():
            yacc_ref[dst, :] = contrib

        @pl.when(f > 0)
        def _():
            yacc_ref[dst, :] = yacc_ref[dst, :] + contrib

        @pl.when(f == nf - 1)
        def _():
            gate = jnp.concatenate([gcol_ref[dst, :]] * (o_ref.shape[1] // LANES), axis=1)
            ys = (yacc_ref[dst, :] * gate).astype(BF16)
            o_ref[...] = o_ref[...] + lax.dot_general(selection(start, rows).astype(BF16), ys, TN_DIMS,
                                                      preferred_element_type=F32)

    def over_chunks(fn):
        def body(r, carry):
            fn(pl.multiple_of(r * R, R), R)
            return carry
        lax.fori_loop(0, n_full, body, 0)

        @pl.when(has_tail)
        def _():
            fn(tail_start, half)

    @pl.when(jnp.logical_and(e == 0, f == 0))
    def _():
        o_ref[...] = jnp.zeros_like(o_ref)

    @pl.when(f == 0)
    def _():
        over_chunks(gather)

    over_chunks(expert)


def _moe(xb, gates_t, pos_t, counts, wg, wu, wd):
    T, D = xb.shape
    F = wg.shape[2]
    tt = MOE_TILE
    nf = 2
    tf = F // nf
    grid_spec = pltpu.PrefetchScalarGridSpec(
        num_scalar_prefetch=1,
        grid=(T // tt, N_EXPERTS, nf),
        in_specs=[pl.BlockSpec((tt, D), lambda m, e, f, c: (m, 0)),
                  pl.BlockSpec((1, N_EXPERTS, tt), lambda m, e, f, c: (m, 0, 0)),
                  pl.BlockSpec((1, N_EXPERTS, tt), lambda m, e, f, c: (m, 0, 0)),
                  pl.BlockSpec((1, D, tf), lambda m, e, f, c: (e, 0, f)),
                  pl.BlockSpec((1, D, tf), lambda m, e, f, c: (e, 0, f)),
                  pl.BlockSpec((1, tf, D), lambda m, e, f, c: (e, f, 0))],
        out_specs=pl.BlockSpec((tt, D), lambda m, e, f, c: (m, 0)),
        scratch_shapes=[pltpu.VMEM((tt, D), BF16), pltpu.VMEM((tt, D), F32), pltpu.VMEM((tt, LANES), F32)],
    )
    return pl.pallas_call(
        functools.partial(_moe_kernel, nf=nf),
        grid_spec=grid_spec,
        out_shape=jax.ShapeDtypeStruct((T, D), F32),
        compiler_params=_params("arbitrary", "arbitrary", "arbitrary"),
        name="moe_experts",
    )(counts, xb, gates_t, pos_t, wg, wu, wd)


def _resid_ln_kernel(x_ref, f_ref, g_ref, b_ref, o_ref):
    o_ref[...] = _layer_norm(DEEPNORM_ALPHA * x_ref[...] + f_ref[...], g_ref[...], b_ref[...])


def _resid_ln(x, f, g, b):
    T, D = x.shape
    tm = 512
    row = lambda: pl.BlockSpec((tm, D), lambda m: (m, 0))
    vec = lambda: pl.BlockSpec((1, D), lambda m: (0, 0))
    return pl.pallas_call(
        _resid_ln_kernel,
        grid=(T // tm,),
        in_specs=[row(), row(), vec(), vec()],
        out_specs=row(),
        out_shape=jax.ShapeDtypeStruct((T, D), F32),
        compiler_params=_params("arbitrary"),
        name="resid_ln",
    )(x, f, g, b)


def _block_diag(w):
    n, d, _ = w.shape
    out = jnp.zeros((n * d, n * d), w.dtype)
    for i in range(n):
        out = out.at[i * d:(i + 1) * d, i * d:(i + 1) * d].set(w[i])
    return out


def _rope_tables(positions):
    B, S = positions.shape
    half = ROPE_DIMS // 2
    inv_freq = ROPE_THETA ** (-jnp.arange(0, ROPE_DIMS, 2, dtype=F32) / ROPE_DIMS)
    d = jnp.arange(LANES) % HEAD_DIM
    ang = positions.astype(F32).reshape(B * S, 1) * inv_freq[d % half][None, :]
    cos = jnp.cos(ang)
    sin = jnp.sin(ang)
    c = jnp.where(d < ROPE_DIMS, cos, 1.0)
    s1 = jnp.where(d < half, -sin, 0.0)
    s2 = jnp.where((d >= half) & (d < ROPE_DIMS), sin, 0.0)
    return c, s1, s2


def _mixer_layer(x, xb, rope, B, S, w_in, conv_w, conv_b, ga_w, ga_b, gx_w, gx_b, lam,
                 mu, w0, w_up, a0, a_up, g_up, k_k, k_a, r_k, lnx_g, lnx_b, w_out, ln_g, ln_b):
    W = RWKV_WIDTH
    row = lambda t: t.reshape(1, -1)
    proj = _in_proj(xb, w_in.astype(BF16))
    q, k, vt, pen = _attn_prep(proj, *rope, B, S)
    att = _attention(q, k, vt, pen, B, S)
    lru = _rglru(proj, conv_w, row(conv_b), _block_diag(ga_w), row(ga_b), _block_diag(gx_w), row(gx_b), row(lam), B, S)
    head_ones = _block_diag(jnp.ones((RWKV_HEADS, HEAD_DIM, HEAD_DIM), F32))
    wlr = jnp.zeros((W, 3 * W), F32)
    wlr = wlr.at[0:64, 0:W].set(w_up).at[64:128, W:2 * W].set(a_up).at[128:256, 2 * W:3 * W].set(g_up)
    r, ld, k2, v, kn, a, g = _rwkv_prep(proj, row(mu), wlr, row(w0), row(a0), row(k_k), row(k_a), head_ones, B, S)
    y = _wkv(r, ld, k2, v, kn, a, B, S)
    rwk = _wkv_post(y, r, k2, v, g, row(r_k), row(lnx_g), row(lnx_b), head_ones.astype(BF16))
    wo = w_out.astype(BF16)
    return _outproj_ln(att, lru, rwk, x, wo[0:ATT_WIDTH], wo[ATT_WIDTH:ATT_WIDTH + LRU_WIDTH],
                       wo[ATT_WIDTH + LRU_WIDTH:], row(ln_g), row(ln_b))


def kernel(x, positions, w_in, lru_conv_w, lru_conv_b, lru_ga_w, lru_ga_b, lru_gx_w, lru_gx_b, lru_lambda, rwkv_mu, rwkv_w0, rwkv_w_up, rwkv_a0, rwkv_a_up, rwkv_g_up, rwkv_k_k, rwkv_k_a, rwkv_r_k, rwkv_lnx_g, rwkv_lnx_b, w_out, ln1_g, ln1_b, ffn_w_gate, ffn_w_up, ffn_w_down, moe_router, moe_w_gate, moe_w_up, moe_w_down, ln2_g, ln2_b):
    B, S, D = x.shape
    T = B * S
    rope = _rope_tables(positions)
    xf = x.reshape(T, D)
    xb = xf.astype(BF16)
    for l in range(DEPTH):
        xf, xb = _mixer_layer(xf, xb, rope, B, S, w_in[l], lru_conv_w[l], lru_conv_b[l], lru_ga_w[l], lru_ga_b[l],
                              lru_gx_w[l], lru_gx_b[l], lru_lambda[l], rwkv_mu[l], rwkv_w0[l], rwkv_w_up[l],
                              rwkv_a0[l], rwkv_a_up[l], rwkv_g_up[l], rwkv_k_k[l], rwkv_k_a[l], rwkv_r_k[l],
                              rwkv_lnx_g[l], rwkv_lnx_b[l], w_out[l], ln1_g[l], ln1_b[l])
        g2 = ln2_g[l].reshape(1, D)
        b2 = ln2_b[l].reshape(1, D)
        if l % 2 == 0:
            i = l // 2
            xf, xb = _ffn_ln(xb, xf, ffn_w_gate[i].astype(BF16), ffn_w_up[i].astype(BF16),
                             ffn_w_down[i].astype(BF16), g2, b2)
        else:
            i = l // 2
            idx = jnp.arange(MOE_TILE)
            tri = (idx[:, None] <= idx[None, :]).astype(BF16)
            gates_t, pos_t, cnt = _router(xf, moe_router[i].T, tri)
            counts = cnt[:, :, 0].astype(jnp.int32).reshape(-1)
            f = _moe(xb, gates_t, pos_t, counts, moe_w_gate[i].astype(BF16), moe_w_up[i].astype(BF16),
                     moe_w_down[i].astype(BF16))
            xf = _resid_ln(xf, f, g2, b2)
            xb = xf.astype(BF16)
    return xf.reshape(B, S, D)
```

```python
import functools

import jax
import jax.numpy as jnp
from jax import lax
from jax.experimental import pallas as pl
from jax.experimental.pallas import tpu as pltpu

F32 = jnp.float32
BF16 = jnp.bfloat16
HI = lax.Precision.HIGHEST
NT_DIMS = (((1,), (1,)), ((), ()))
TN_DIMS = (((0,), (0,)), ((), ()))

HEAD_DIM = 64
ATT_HEADS = 8
ATT_WIDTH = ATT_HEADS * HEAD_DIM
LRU_WIDTH = 256
RWKV_HEADS = 4
RWKV_WIDTH = 256
ROPE_DIMS = 16
ROPE_THETA = 500000.0
MOBA_BLOCK = 256
MOBA_TOPK = 3
CONV_WIDTH = 4
LRU_C = 8.0
RWKV_LN_EPS = 64e-5
N_EXPERTS = 8
LN_EPS = 1e-5
DEPTH = 2
DEEPNORM_ALPHA = (2 * DEPTH) ** 0.25
NEG_INF = -1e30
LOG2_E = 1.4426950408889634

LANES = 128
VMEM_LIMIT = 56 * 1024 * 1024

ATTN_STREAMS = 4
WKV_CHUNK = 64
WKV_GROUP = 8
LRU_CHUNK = 256
RWKV_PREP_ROWS = 512
MOE_TILE = 1024
MOE_ROWS = 256


def _params(*sem):
    return pltpu.CompilerParams(dimension_semantics=sem, vmem_limit_bytes=VMEM_LIMIT)


def _sigmoid(x):
    return 1.0 / (1.0 + jnp.exp(-x))


def _softplus(x):
    return jnp.maximum(x, 0.0) + jnp.log1p(jnp.exp(-jnp.abs(x)))


def _expm1(z):
    u = jnp.exp(z)
    um1 = u - 1.0
    return jnp.where(u == 1.0, z, jnp.where(um1 == -1.0, -1.0, um1 * z / jnp.log(u)))


def _layer_norm(y, g, b):
    m = jnp.mean(y, axis=-1, keepdims=True)
    d = y - m
    var = jnp.mean(d * d, axis=-1, keepdims=True)
    return d * lax.rsqrt(var + LN_EPS) * g + b


def _matmul_kernel(x_ref, w_ref, o_ref):
    o_ref[...] = jnp.dot(x_ref[...], w_ref[...], preferred_element_type=F32)


def _in_proj(xb, w):
    T, D = xb.shape
    N = w.shape[1]
    tm, tn = 1024, 1024
    return pl.pallas_call(
        _matmul_kernel,
        grid=(N // tn, T // tm),
        in_specs=[pl.BlockSpec((tm, D), lambda n, m: (m, 0)),
                  pl.BlockSpec((D, tn), lambda n, m: (0, n))],
        out_specs=pl.BlockSpec((tm, tn), lambda n, m: (m, n)),
        out_shape=jax.ShapeDtypeStruct((T, N), F32),
        compiler_params=_params("arbitrary", "arbitrary"),
        name="in_proj",
    )(xb, w)


def _attn_prep_kernel(proj_ref, c_ref, s1_ref, s2_ref, q_ref, k_ref, vt_ref, pen_ref, km_ref, *, nb):
    i = pl.program_id(1)

    @pl.when(i == 0)
    def _():
        km_ref[...] = jnp.zeros_like(km_ref)

    c = c_ref[...]
    s1 = s1_ref[...]
    s2 = s2_ref[...]

    def rope(xt):
        return xt * c + pltpu.roll(xt, LANES - ROPE_DIMS // 2, 1) * s1 + pltpu.roll(xt, ROPE_DIMS // 2, 1) * s2

    q_tiles = []
    for ct in range(ATT_WIDTH // LANES):
        lo, hi = ct * LANES, (ct + 1) * LANES
        qr = rope(proj_ref[:, lo:hi])
        q_tiles.append(qr)
        q_ref[:, lo:hi] = (qr * (HEAD_DIM ** -0.5 * LOG2_E)).astype(BF16)
        kr = rope(proj_ref[:, ATT_WIDTH + lo:ATT_WIDTH + hi])
        k_ref[0, 0, :, lo:hi] = kr.astype(BF16)
        km_row = lax.broadcasted_iota(jnp.int32, (nb, LANES), 0)
        km_ref[:, lo:hi] = jnp.where(km_row == i, jnp.mean(kr, axis=0, keepdims=True), km_ref[:, lo:hi])
    vt_ref[0, 0] = proj_ref[:, 2 * ATT_WIDTH:3 * ATT_WIDTH].T.astype(BF16)

    q_rot = jnp.concatenate(q_tiles, axis=1)
    km = km_ref[...]
    km_rows = jnp.concatenate([km] * ATT_HEADS, axis=0)
    row_head = lax.broadcasted_iota(jnp.int32, km_rows.shape, 0) // nb
    lane_head = lax.broadcasted_iota(jnp.int32, km_rows.shape, 1) // HEAD_DIM
    g_all = lax.dot_general(jnp.where(row_head == lane_head, km_rows, 0.0), q_rot, NT_DIMS,
                            precision=HI, preferred_element_type=F32)
    n_iota = lax.broadcasted_iota(jnp.int32, (nb, MOBA_BLOCK), 0)
    past = n_iota < i
    for h in range(ATT_HEADS):
        g = g_all[h * nb:(h + 1) * nb, :]
        rank = jnp.zeros((nb, MOBA_BLOCK), F32)
        for m in range(nb):
            gm = g[m:m + 1, :]
            beats = jnp.where(gm > g, 1.0, jnp.where(gm == g, jnp.where(n_iota > m, 1.0, 0.0), 0.0))
            rank = rank + jnp.where(m < i, beats, 0.0)
        pen = jnp.where(past, jnp.where(rank < float(MOBA_TOPK), 0.0, NEG_INF), NEG_INF)
        half = (h % 2) * MOBA_BLOCK
        pen_ref[0, h // 2, :, half:half + MOBA_BLOCK] = pen


def _attn_prep(proj, rope_c, rope_s1, rope_s2, B, S):
    T = B * S
    nb = S // MOBA_BLOCK
    blk = MOBA_BLOCK
    return pl.pallas_call(
        functools.partial(_attn_prep_kernel, nb=nb),
        grid=(B, nb),
        in_specs=[pl.BlockSpec((blk, 3 * ATT_WIDTH), lambda b, i: (b * nb + i, 0)),
                  pl.BlockSpec((blk, LANES), lambda b, i: (b * nb + i, 0)),
                  pl.BlockSpec((blk, LANES), lambda b, i: (b * nb + i, 0)),
                  pl.BlockSpec((blk, LANES), lambda b, i: (b * nb + i, 0))],
        out_specs=[pl.BlockSpec((blk, ATT_WIDTH), lambda b, i: (b * nb + i, 0)),
                   pl.BlockSpec((1, 1, blk, ATT_WIDTH), lambda b, i: (b, i, 0, 0)),
                   pl.BlockSpec((1, 1, ATT_WIDTH, blk), lambda b, i: (b, i, 0, 0)),
                   pl.BlockSpec((1, ATT_HEADS // 2, nb, 2 * blk), lambda b, i: (b, 0, 0, i))],
        out_shape=[jax.ShapeDtypeStruct((T, ATT_WIDTH), BF16),
                   jax.ShapeDtypeStruct((B, nb, blk, ATT_WIDTH), BF16),
                   jax.ShapeDtypeStruct((B, nb, ATT_WIDTH, blk), BF16),
                   jax.ShapeDtypeStruct((B, ATT_HEADS // 2, nb, 2 * S), F32)],
        scratch_shapes=[pltpu.VMEM((nb, ATT_WIDTH), F32)],
        compiler_params=_params("arbitrary", "arbitrary"),
        name="attn_prep",
    )(proj, rope_c, rope_s1, rope_s2)


def _attn_kernel(q_ref, k_ref, vt_ref, pen_ref, o_ref):
    i = pl.program_id(2)
    blk = MOBA_BLOCK
    streams = range(ATTN_STREAMS)
    each = lambda fn, *xs: [fn(*(x[t] for x in xs)) for t in streams]
    lanes = lambda t: slice(t * LANES, (t + 1) * LANES)
    lane_head = lax.broadcasted_iota(jnp.int32, (blk, LANES), 1) // HEAD_DIM
    kidx = lax.broadcasted_iota(jnp.int32, (blk, 2 * blk), 0)
    qidx = lax.broadcasted_iota(jnp.int32, (blk, 2 * blk), 1) % blk
    nt = lambda a, b: lax.dot_general(a, b, NT_DIMS, preferred_element_type=F32)
    nn = lambda a, b: jnp.dot(a, b, preferred_element_type=F32)
    col_max = lambda x: jnp.max(x, axis=0, keepdims=True)
    ones_tile = jnp.ones((16, blk), BF16)

    def stacked_q(t):
        q = q_ref[:, lanes(t)]
        zero = jnp.zeros_like(q)
        return jnp.concatenate([jnp.where(lane_head == 0, q, zero), jnp.where(lane_head == 1, q, zero)], axis=0)

    qs = [stacked_q(t) for t in streams]
    s = [jnp.where(kidx <= qidx, nt(k_ref[0, i, :, lanes(t)], qs[t]), NEG_INF) for t in streams]
    m0 = each(col_max, s)
    p = each(lambda x, m: jnp.exp2(x - m), s, m0)
    pv = [nn(jnp.concatenate([vt_ref[0, i, lanes(t), :], ones_tile], axis=0), p[t].astype(BF16)) for t in streams]
    l0 = [x[LANES:LANES + 1] for x in pv]
    acc0 = [x[0:LANES] for x in pv]

    def body(jj, carry):
        m, l, acc = carry
        j0 = 2 * jj
        j1 = j0 + 1
        s0 = [nt(k_ref[0, j0, :, lanes(t)], qs[t]) + pen_ref[0, t, pl.ds(j0, 1), :] for t in streams]
        s1 = [nt(k_ref[0, j1, :, lanes(t)], qs[t]) + pen_ref[0, t, pl.ds(j1, 1), :] for t in streams]
        m_new = each(lambda mo, a, b: jnp.maximum(mo, jnp.maximum(col_max(a), col_max(b))), m, s0, s1)
        alpha = each(lambda mo, mn: jnp.exp2(mo - mn), m, m_new)
        p0 = each(lambda a, mn: jnp.exp2(a - mn), s0, m_new)
        p1 = each(lambda a, mn: jnp.exp2(a - mn), s1, m_new)
        pb0 = [x.astype(BF16) for x in p0]
        pb1 = [x.astype(BF16) for x in p1]
        pv0 = [nn(jnp.concatenate([vt_ref[0, j0, lanes(t), :], ones_tile], axis=0), pb0[t]) for t in streams]
        pv1 = [nn(jnp.concatenate([vt_ref[0, j1, lanes(t), :], ones_tile], axis=0), pb1[t]) for t in streams]
        l = each(lambda al, lo, a, b: al * lo + a[LANES:LANES + 1] + b[LANES:LANES + 1], alpha, l, pv0, pv1)
        acc = each(lambda ac, al, a, b: ac * al + a[0:LANES] + b[0:LANES], acc, alpha, pv0, pv1)
        return tuple(m_new), tuple(l), tuple(acc)

    _, l, acc = lax.fori_loop(0, (i + 1) // 2, body, (tuple(m0), tuple(l0), tuple(acc0)))
    row = lax.broadcasted_iota(jnp.int32, (LANES, blk), 0)
    for t in streams:
        out = acc[t] / l[t]
        o_ref[:, lanes(t)] = jnp.where(row < HEAD_DIM, out[:, 0:blk], out[:, blk:2 * blk]).T


def _attention(q, k, vt, pen, B, S):
    T = B * S
    nb = S // MOBA_BLOCK
    blk = MOBA_BLOCK
    w = ATTN_STREAMS * LANES
    n_groups = ATT_WIDTH // w
    return pl.pallas_call(
        _attn_kernel,
        grid=(B, n_groups, nb),
        in_specs=[pl.BlockSpec((blk, w), lambda b, hp, i: (b * nb + i, hp)),
                  pl.BlockSpec((1, nb, blk, w), lambda b, hp, i: (b, 0, 0, hp)),
                  pl.BlockSpec((1, nb, w, blk), lambda b, hp, i: (b, 0, hp, 0)),
                  pl.BlockSpec((1, ATTN_STREAMS, nb, 2 * blk), lambda b, hp, i: (b, hp, 0, i))],
        out_specs=pl.BlockSpec((blk, w), lambda b, hp, i: (b * nb + i, hp)),
        out_shape=jax.ShapeDtypeStruct((T, ATT_WIDTH), F32),
        compiler_params=_params("arbitrary", "arbitrary", "arbitrary"),
        name="moba_attn",
    )(q, k, vt, pen)


def _lru_kernel(p_ref, cw_ref, cb_ref, ga_ref, gab_ref, gx_ref, gxb_ref, lam_ref, o_ref, xbuf, hc):
    t = pl.program_id(1)
    tc = LRU_CHUNK
    pad = 8

    @pl.when(t == 0)
    def _():
        xbuf[0:pad, :] = jnp.zeros((pad, LRU_WIDTH), F32)
        hc[...] = jnp.zeros_like(hc)

    @pl.when(t > 0)
    def _():
        xbuf[0:pad, :] = xbuf[tc:tc + pad, :]

    x = p_ref[:, 0:LRU_WIDTH]
    gate = p_ref[:, LRU_WIDTH:2 * LRU_WIDTH]
    xbuf[pad:pad + tc, :] = x
    xc = cb_ref[...] + x * cw_ref[CONV_WIDTH - 1:CONV_WIDTH, :]
    for j in range(CONV_WIDTH - 1):
        back = CONV_WIDTH - 1 - j
        xc = xc + xbuf[pad - back:pad - back + tc, :] * cw_ref[j:j + 1, :]

    r = _sigmoid(jnp.dot(xc, ga_ref[...], precision=HI, preferred_element_type=F32) + gab_ref[...])
    ig = _sigmoid(jnp.dot(xc, gx_ref[...], precision=HI, preferred_element_type=F32) + gxb_ref[...])
    log_a = -LRU_C * r * _softplus(-lam_ref[...])
    a = jnp.exp(log_a)
    u = jnp.sqrt(-_expm1(2.0 * log_a)) * (ig * xc)

    rows = lax.broadcasted_iota(jnp.int32, (tc, LRU_WIDTH), 0)
    d = 1
    while d < tc:
        keep = rows >= d
        a_prev = jnp.where(keep, pltpu.roll(a, d, 0), 1.0)
        u_prev = jnp.where(keep, pltpu.roll(u, d, 0), 0.0)
        u = a * u_prev + u
        a = a * a_prev
        d *= 2
    h = u + a * hc[...]
    hc[...] = h[tc - 1:tc, :]
    gl = 0.5 * gate * (1.0 + jnp.tanh(0.7978845608028654 * (gate + 0.044715 * gate * gate * gate)))
    o_ref[...] = h * gl


def _rglru(proj, cw, cb, ga, gab, gx, gxb, lam, B, S):
    T = B * S
    tc = LRU_CHUNK
    nt = S // tc
    col = (3 * ATT_WIDTH) // (2 * LRU_WIDTH)
    vec = lambda: pl.BlockSpec((1, LRU_WIDTH), lambda b, t: (0, 0))
    mat = lambda: pl.BlockSpec((LRU_WIDTH, LRU_WIDTH), lambda b, t: (0, 0))
    return pl.pallas_call(
        _lru_kernel,
        grid=(B, nt),
        in_specs=[pl.BlockSpec((tc, 2 * LRU_WIDTH), lambda b, t: (b * nt + t, col)),
                  pl.BlockSpec((CONV_WIDTH, LRU_WIDTH), lambda b, t: (0, 0)),
                  vec(), mat(), vec(), mat(), vec(), vec()],
        out_specs=pl.BlockSpec((tc, LRU_WIDTH), lambda b, t: (b * nt + t, 0)),
        out_shape=jax.ShapeDtypeStruct((T, LRU_WIDTH), F32),
        scratch_shapes=[pltpu.VMEM((tc + 8, LRU_WIDTH), F32), pltpu.VMEM((1, LRU_WIDTH), F32)],
        compiler_params=_params("arbitrary", "arbitrary"),
        name="rglru",
    )(proj, cw, cb, ga, gab, gx, gxb, lam)


def _rwkv_prep_kernel(p_ref, pp_ref, mu_ref, wlr_ref, w0_ref, a0_ref, kk_ref, ka_ref, ones_ref,
                      r_ref, ld_ref, k2_ref, v_ref, kn_ref, a_ref, g_ref):
    i = pl.program_id(1)
    W = RWKV_WIDTH
    p = p_ref[...]
    rows = lax.broadcasted_iota(jnp.int32, p.shape, 0)
    prev_last = jnp.where(i == 0, 0.0, pp_ref[7:8, :])
    p_prev = jnp.where(rows == 0, prev_last, pltpu.roll(p, 1, 0))
    pf = p + (p_prev - p) * mu_ref[...]
    r = pf[:, 0:W]
    k = pf[:, W:2 * W]
    v = pf[:, 2 * W:3 * W]
    z = pf[:, 3 * W:4 * W]
    lane = lax.broadcasted_iota(jnp.int32, z.shape, 1)
    zz = jnp.where(lane < 64, jnp.tanh(z), jnp.where(lane < 128, z, _sigmoid(z)))
    lr = jnp.dot(zz, wlr_ref[...], precision=HI, preferred_element_type=F32)
    w = -_softplus(-(w0_ref[...] + lr[:, 0:W])) - 0.5
    a = _sigmoid(a0_ref[...] + lr[:, W:2 * W])
    kk = k * kk_ref[...]
    ssq = jnp.dot(kk * kk, ones_ref[...], precision=HI, preferred_element_type=F32)
    kk = kk / jnp.maximum(jnp.sqrt(ssq), 1e-12)
    r_ref[...] = r
    ld_ref[...] = -jnp.exp(w)
    k2_ref[...] = k * (1.0 + (a - 1.0) * ka_ref[...])
    v_ref[...] = v
    kn_ref[...] = kk
    a_ref[...] = a
    g_ref[...] = lr[:, 2 * W:3 * W]


def _rwkv_prep(proj, mu, wlr, w0, a0, k_k, k_a, head_ones, B, S):
    T = B * S
    tr = RWKV_PREP_ROWS
    nt = S // tr
    W = RWKV_WIDTH
    col = 2
    vec = lambda: pl.BlockSpec((1, W), lambda b, t: (0, 0))
    out = lambda: pl.BlockSpec((tr, W), lambda b, t: (b * nt + t, 0))
    return pl.pallas_call(
        _rwkv_prep_kernel,
        grid=(B, nt),
        in_specs=[pl.BlockSpec((tr, 4 * W), lambda b, t: (b * nt + t, col)),
                  pl.BlockSpec((8, 4 * W), lambda b, t: (jnp.maximum((b * nt + t) * (tr // 8) - 1, 0), col)),
                  pl.BlockSpec((1, 4 * W), lambda b, t: (0, 0)),
                  pl.BlockSpec((W, 3 * W), lambda b, t: (0, 0)),
                  vec(), vec(), vec(), vec(),
                  pl.BlockSpec((W, W), lambda b, t: (0, 0))],
        out_specs=[out() for _ in range(7)],
        out_shape=[jax.ShapeDtypeStruct((T, W), F32) for _ in range(7)],
        compiler_params=_params("arbitrary", "arbitrary"),
        name="rwkv_prep",
    )(proj, proj, mu, wlr, w0, a0, k_k, k_a, head_ones)


def _wkv_kernel(r_ref, ld_ref, k2_ref, v_ref, kn_ref, a_ref, y_ref, st_ref):
    c = pl.program_id(1)
    C = WKV_CHUNK
    W = RWKV_WIDTH
    H = RWKV_HEADS

    @pl.when(c == 0)
    def _():
        st_ref[...] = jnp.zeros_like(st_ref)

    def mm(x, y):
        return jnp.dot(x.astype(BF16), y.astype(BF16), preferred_element_type=F32)

    def mm_nt(x, y):
        return lax.dot_general(x.astype(BF16), y.astype(BF16), NT_DIMS, preferred_element_type=F32)

    def mm_tn(x, y):
        return lax.dot_general(x.astype(BF16), y.astype(BF16), TN_DIMS, preferred_element_type=F32)

    tr = lax.broadcasted_iota(jnp.int32, (C, C), 0)
    tc = lax.broadcasted_iota(jnp.int32, (C, C), 1)
    tri = jnp.where(tr >= tc, 1.0, 0.0)
    row = lax.broadcasted_iota(jnp.int32, (W, W), 0)
    colm = lax.broadcasted_iota(jnp.int32, (W, W), 1)
    same_head = (row // C) == (colm // HEAD_DIM)
    strict = lambda x: jnp.where(same_head, jnp.where(row > colm, x, 0.0), 0.0)
    incl = lambda x: jnp.where(same_head, jnp.where(row >= colm, x, 0.0), 0.0)
    same16 = (row // 16) == (colm // 16)
    eye = jnp.where(row == colm, 1.0, 0.0)

    def stack(x):
        return jnp.where(same_head, jnp.concatenate([x] * H, axis=0), 0.0)

    G = range(WKV_GROUP)
    each = lambda fn, *xs: [fn(*(x[g] for x in xs)) for g in G]

    r = [r_ref[g] for g in G]
    ld = [ld_ref[g] for g in G]
    k2 = [k2_ref[g] for g in G]
    v_s = [stack(v_ref[g]) for g in G]
    kn = [kn_ref[g] for g in G]
    b_s = [kn[g] * a_ref[g] for g in G]
    cl = each(lambda x: jnp.dot(tri, x, precision=HI, preferred_element_type=F32), ld)
    cl_end = [x[C - 1:C, :] for x in cl]
    e_neg = [jnp.exp(-x) for x in cl]
    e_end = each(lambda ce, x: jnp.exp(ce - x), cl_end, cl)
    ar_s = [jnp.concatenate([stack(-kn[g] * jnp.exp(cl[g] - ld[g])), stack(r[g] * jnp.exp(cl[g]))], axis=0) for g in G]
    bk_s = [jnp.concatenate([stack(b_s[g] * e_neg[g]), stack(k2[g] * e_neg[g])], axis=0) for g in G]
    end_s = [jnp.concatenate([stack(b_s[g] * e_end[g]), stack(k2[g] * e_end[g])], axis=0) for g in G]

    prod = each(mm_nt, ar_s, bk_s)
    n_ab = [strict(p[0:W, 0:W]) for p in prod]
    a_ak = [strict(p[0:W, W:2 * W]) for p in prod]
    a_rb = [incl(p[W:2 * W, 0:W]) for p in prod]
    a_rk = [incl(p[W:2 * W, W:2 * W]) for p in prod]

    nd = [jnp.where(same16, n, 0.0) for n in n_ab]
    lo = each(lambda n, d: n - d, n_ab, nd)
    n2 = each(mm, nd, nd)
    p1 = [eye + d for d in nd]
    n4 = each(mm, n2, n2)
    p2 = each(lambda p, n: p + mm(p, n), p1, n2)
    n8 = each(mm, n4, n4)
    p3 = each(lambda p, n: p + mm(p, n), p2, n4)
    dinv = each(lambda p, n: p + mm(p, n), p3, n8)
    x1 = each(mm, dinv, lo)
    x2 = each(mm, x1, x1)
    y1 = [eye + x for x in x1]
    m_inv = each(lambda y, x: y + mm(y, x), y1, x2)
    tinv = each(mm, m_inv, dinv)

    st = [st_ref[g] for g in G]
    ar_st = each(mm_nt, ar_s, st)
    akv = each(mm, a_ak, v_s)
    u_s = each(lambda t, a, b: mm(t, a[0:W] + b), tinv, ar_st, akv)
    rkv = each(mm, a_rk, v_s)
    y_s = each(lambda a, m, u, b: a[W:2 * W] + mm(m, u) + b, ar_st, a_rb, u_s, rkv)
    upd = each(lambda u, v, e: mm_tn(jnp.concatenate([u, v], axis=0), e), u_s, v_s, end_s)
    for g in G:
        y = y_s[g][0:C, :]
        for h in range(1, H):
            y = y + y_s[g][h * C:(h + 1) * C, :]
        y_ref[g] = y
        st_ref[g] = st[g] * jnp.exp(cl_end[g]) + upd[g]


def _wkv(r, ld, k2, v, kn, a, B, S):
    C = WKV_CHUNK
    nc = S // C
    W = RWKV_WIDTH
    G = WKV_GROUP
    blk = lambda: pl.BlockSpec((G, C, W), lambda b, c: (b, c, 0))
    seq = lambda t: t.reshape(B, S, W)
    y = pl.pallas_call(
        _wkv_kernel,
        grid=(B // G, nc),
        in_specs=[blk() for _ in range(6)],
        out_specs=blk(),
        out_shape=jax.ShapeDtypeStruct((B, S, W), F32),
        scratch_shapes=[pltpu.VMEM((G, W, W), F32)],
        compiler_params=_params("arbitrary", "arbitrary"),
        name="wkv7",
    )(seq(r), seq(ld), seq(k2), seq(v), seq(kn), seq(a))
    return y.reshape(B * S, W)


def _wkv_post_kernel(y_ref, r_ref, k2_ref, v_ref, g_ref, rk_ref, lg_ref, lb_ref, ones_ref, o_ref):
    ones = ones_ref[...]

    def head_sum(x):
        hi = x.astype(BF16)
        lo = (x - hi.astype(F32)).astype(BF16)
        return (jnp.dot(hi, ones, preferred_element_type=F32) + jnp.dot(lo, ones, preferred_element_type=F32))

    y = y_ref[...]
    r = r_ref[...]
    k2 = k2_ref[...]
    inv_n = 1.0 / HEAD_DIM
    d = y - head_sum(y) * inv_n
    var = head_sum(d * d) * inv_n
    yn = d * lax.rsqrt(var + RWKV_LN_EPS) * lg_ref[...] + lb_ref[...]
    bonus = head_sum(r * k2 * rk_ref[...]) * v_ref[...]
    o_ref[...] = (yn + bonus) * g_ref[...]


def _wkv_post(y, r, k2, v, g, r_k, lnx_g, lnx_b, head_ones_bf16):
    T, W = y.shape
    tm = 512
    blk = lambda: pl.BlockSpec((tm, W), lambda m: (m, 0))
    vec = lambda: pl.BlockSpec((1, W), lambda m: (0, 0))
    return pl.pallas_call(
        _wkv_post_kernel,
        grid=(T // tm,),
        in_specs=[blk() for _ in range(5)] + [vec(), vec(), vec(), pl.BlockSpec((W, W), lambda m: (0, 0))],
        out_specs=blk(),
        out_shape=jax.ShapeDtypeStruct((T, W), F32),
        compiler_params=_params("arbitrary"),
        name="wkv_post",
    )(y, r, k2, v, g, r_k, lnx_g, lnx_b, head_ones_bf16)


def _outproj_ln_kernel(att_ref, lru_ref, rwk_ref, x_ref, wa_ref, wl_ref, wr_ref, g_ref, b_ref, o_ref, ob_ref):
    h = jnp.dot(att_ref[...].astype(BF16), wa_ref[...], preferred_element_type=F32)
    h = h + jnp.dot(lru_ref[...].astype(BF16), wl_ref[...], preferred_element_type=F32)
    h = h + jnp.dot(rwk_ref[...].astype(BF16), wr_ref[...], preferred_element_type=F32)
    y = _layer_norm(DEEPNORM_ALPHA * x_ref[...] + h, g_ref[...], b_ref[...])
    o_ref[...] = y
    ob_ref[...] = y.astype(BF16)


def _outproj_ln(att, lru, rwk, x, wa, wl, wr, g, b):
    T, D = x.shape
    tm = 512
    row = lambda w: pl.BlockSpec((tm, w), lambda m: (m, 0))
    full = lambda a: pl.BlockSpec(a.shape, lambda m: (0, 0))
    return pl.pallas_call(
        _outproj_ln_kernel,
        grid=(T // tm,),
        in_specs=[row(ATT_WIDTH), row(LRU_WIDTH), row(RWKV_WIDTH), row(D), full(wa), full(wl), full(wr), full(g), full(b)],
        out_specs=[row(D), row(D)],
        out_shape=[jax.ShapeDtypeStruct((T, D), F32), jax.ShapeDtypeStruct((T, D), BF16)],
        compiler_params=_params("arbitrary"),
        name="outproj_ln",
    )(att, lru, rwk, x, wa, wl, wr, g, b)


def _ffn_kernel(xb_ref, x_ref, wg_ref, wu_ref, wd_ref, g_ref, b_ref, o_ref, ob_ref, acc_ref, *, nf):
    f = pl.program_id(1)
    xb = xb_ref[...]
    hg = jnp.dot(xb, wg_ref[...], preferred_element_type=F32)
    hu = jnp.dot(xb, wu_ref[...], preferred_element_type=F32)
    hh = (hg * _sigmoid(hg) * hu).astype(BF16)
    contrib = jnp.dot(hh, wd_ref[...], preferred_element_type=F32)

    @pl.when(f == 0)
    def _():
        acc_ref[...] = contrib

    @pl.when(f > 0)
    def _():
        acc_ref[...] = acc_ref[...] + contrib

    @pl.when(f == nf - 1)
    def _():
        y = _layer_norm(DEEPNORM_ALPHA * x_ref[...] + acc_ref[...], g_ref[...], b_ref[...])
        o_ref[...] = y
        ob_ref[...] = y.astype(BF16)


def _ffn_ln(xb, x, wg, wu, wd, g, b):
    T, D = x.shape
    F = wg.shape[1]
    tm = 512
    nf = 2
    tf = F // nf
    return pl.pallas_call(
        functools.partial(_ffn_kernel, nf=nf),
        grid=(T // tm, nf),
        in_specs=[pl.BlockSpec((tm, D), lambda m, f: (m, 0)),
                  pl.BlockSpec((tm, D), lambda m, f: (m, 0)),
                  pl.BlockSpec((D, tf), lambda m, f: (0, f)),
                  pl.BlockSpec((D, tf), lambda m, f: (0, f)),
                  pl.BlockSpec((tf, D), lambda m, f: (f, 0)),
                  pl.BlockSpec((1, D), lambda m, f: (0, 0)),
                  pl.BlockSpec((1, D), lambda m, f: (0, 0))],
        out_specs=[pl.BlockSpec((tm, D), lambda m, f: (m, 0)), pl.BlockSpec((tm, D), lambda m, f: (m, 0))],
        out_shape=[jax.ShapeDtypeStruct((T, D), F32), jax.ShapeDtypeStruct((T, D), BF16)],
        scratch_shapes=[pltpu.VMEM((tm, D), F32)],
        compiler_params=_params("arbitrary", "arbitrary"),
        name="ffn_ln",
    )(xb, x, wg, wu, wd, g, b)


def _router_kernel(x_ref, wrt_ref, tri_ref, g_ref, pos_ref, cnt_ref):
    tt = x_ref.shape[0]
    logits = lax.dot_general(wrt_ref[...], x_ref[...], NT_DIMS, precision=HI, preferred_element_type=F32)
    row = lax.broadcasted_iota(jnp.int32, logits.shape, 0).astype(F32)
    m1 = jnp.max(logits, axis=0, keepdims=True)
    i1 = jnp.min(jnp.where(logits == m1, row, float(N_EXPERTS)), axis=0, keepdims=True)
    rest = jnp.where(row == i1, -jnp.inf, logits)
    m2 = jnp.max(rest, axis=0, keepdims=True)
    i2 = jnp.min(jnp.where(rest == m2, row, float(N_EXPERTS)), axis=0, keepdims=True)
    e = jnp.exp(m2 - m1)
    g1 = 1.0 / (1.0 + e)
    g2 = e / (1.0 + e)
    g_ref[0] = jnp.where(row == i1, g1, jnp.where(row == i2, g2, 0.0))
    ind = jnp.where(row == i1, 1.0, jnp.where(row == i2, 1.0, 0.0))
    csum = jnp.dot(ind.astype(BF16), tri_ref[...], preferred_element_type=F32)
    pos_ref[0] = jnp.where(ind > 0.5, csum - 1.0, -1.0)
    cnt_ref[0] = jnp.broadcast_to(csum[:, tt - 1:tt], (N_EXPERTS, LANES))


def _router(x, wrt, tri):
    T, D = x.shape
    tt = MOE_TILE
    nt = T // tt
    tile = lambda: pl.BlockSpec((1, N_EXPERTS, tt), lambda m: (m, 0, 0))
    return pl.pallas_call(
        _router_kernel,
        grid=(nt,),
        in_specs=[pl.BlockSpec((tt, D), lambda m: (m, 0)), pl.BlockSpec((N_EXPERTS, D), lambda m: (0, 0)),
                  pl.BlockSpec((tt, tt), lambda m: (0, 0))],
        out_specs=[tile(), tile(), pl.BlockSpec((1, N_EXPERTS, LANES), lambda m: (m, 0, 0))],
        out_shape=[jax.ShapeDtypeStruct((nt, N_EXPERTS, tt), F32), jax.ShapeDtypeStruct((nt, N_EXPERTS, tt), F32),
                   jax.ShapeDtypeStruct((nt, N_EXPERTS, LANES), F32)],
        compiler_params=_params("arbitrary"),
        name="moe_router",
    )(x, wrt, tri)


def _moe_kernel(cnt_ref, xb_ref, x_ref, gate_ref, pos_ref, wg_ref, wu_ref, wd_ref, g_ref, b_ref,
                o_ref, xg_ref, yacc_ref, gcol_ref, *, nf):
    t = pl.program_id(0)
    e = pl.program_id(1)
    f = pl.program_id(2)
    tt = xb_ref.shape[0]
    R = MOE_ROWS
    half = R // 2
    n = cnt_ref[t * N_EXPERTS + e]
    rem = n % R
    n_full = n // R + jnp.where(rem > half, 1, 0)
    has_tail = jnp.logical_and(rem > 0, rem <= half)
    tail_start = pl.multiple_of(n_full * R, half)
    pos_row = pos_ref[0, pl.ds(e, 1), :]

    def selection(start, rows):
        slot = lax.broadcasted_iota(jnp.int32, (rows, tt), 0) + start
        return jnp.where(pos_row == slot.astype(F32), 1.0, 0.0)

    def gather(start, rows):
        sel = selection(start, rows)
        dst = pl.ds(start, rows)
        xg_ref[dst, :] = jnp.dot(sel.astype(BF16), xb_ref[...], preferred_element_type=F32).astype(BF16)
        gate_row = gate_ref[0, pl.ds(e, 1), :]
        gcol_ref[dst, :] = jnp.broadcast_to(jnp.sum(sel * gate_row, axis=1, keepdims=True), (rows, LANES))

    def expert(start, rows):
        dst = pl.ds(start, rows)
        xc = xg_ref[dst, :]
        hg = jnp.dot(xc, wg_ref[0], preferred_element_type=F32)
        hu = jnp.dot(xc, wu_ref[0], preferred_element_type=F32)
        contrib = jnp.dot((hg * _sigmoid(hg) * hu).astype(BF16), wd_ref[0], preferred_element_type=F32)

        @pl.when(f == 0)
        def _():
            yacc_ref[dst, :] = contrib

        @pl.when(f > 0)
        def _():
            yacc_ref[dst, :] = yacc_ref[dst, :] + contrib

        @pl.when(f == nf - 1)
        def _():
            gate = jnp.concatenate([gcol_ref[dst, :]] * (o_ref.shape[1] // LANES), axis=1)
            ys = (yacc_ref[dst, :] * gate).astype(BF16)
            o_ref[...] = o_ref[...] + lax.dot_general(selection(start, rows).astype(BF16), ys, TN_DIMS,
                                                      preferred_element_type=F32)

    def over_chunks(fn):
        def body(r, carry):
            fn(pl.multiple_of(r * R, R), R)
            return carry
        lax.fori_loop(0, n_full, body, 0)

        @pl.when(has_tail)
        def _():
            fn(tail_start, half)

    @pl.when(jnp.logical_and(e == 0, f == 0))
    def _():
        o_ref[...] = jnp.zeros_like(o_ref)

    @pl.when(f == 0)
    def _():
        over_chunks(gather)

    over_chunks(expert)

    @pl.when(jnp.logical_and(e == N_EXPERTS - 1, f == nf - 1))
    def _():
        o_ref[...] = _layer_norm(DEEPNORM_ALPHA * x_ref[...] + o_ref[...], g_ref[...], b_ref[...])


def _moe_ln(xb, x, gates_t, pos_t, counts, wg, wu, wd, g, b):
    T, D = xb.shape
    F = wg.shape[2]
    tt = MOE_TILE
    nf = 2
    tf = F // nf
    grid_spec = pltpu.PrefetchScalarGridSpec(
        num_scalar_prefetch=1,
        grid=(T // tt, N_EXPERTS, nf),
        in_specs=[pl.BlockSpec((tt, D), lambda m, e, f, c: (m, 0)),
                  pl.BlockSpec((tt, D), lambda m, e, f, c: (m, 0)),
                  pl.BlockSpec((1, N_EXPERTS, tt), lambda m, e, f, c: (m, 0, 0)),
                  pl.BlockSpec((1, N_EXPERTS, tt), lambda m, e, f, c: (m, 0, 0)),
                  pl.BlockSpec((1, D, tf), lambda m, e, f, c: (e, 0, f)),
                  pl.BlockSpec((1, D, tf), lambda m, e, f, c: (e, 0, f)),
                  pl.BlockSpec((1, tf, D), lambda m, e, f, c: (e, f, 0)),
                  pl.BlockSpec((1, D), lambda m, e, f, c: (0, 0)),
                  pl.BlockSpec((1, D), lambda m, e, f, c: (0, 0))],
        out_specs=pl.BlockSpec((tt, D), lambda m, e, f, c: (m, 0)),
        scratch_shapes=[pltpu.VMEM((tt, D), BF16), pltpu.VMEM((tt, D), F32), pltpu.VMEM((tt, LANES), F32)],
    )
    return pl.pallas_call(
        functools.partial(_moe_kernel, nf=nf),
        grid_spec=grid_spec,
        out_shape=jax.ShapeDtypeStruct((T, D), F32),
        compiler_params=_params("arbitrary", "arbitrary", "arbitrary"),
        name="moe_experts",
    )(counts, xb, x, gates_t, pos_t, wg, wu, wd, g, b)


def _block_diag(w):
    n, d, _ = w.shape
    out = jnp.zeros((n * d, n * d), w.dtype)
    for i in range(n):
        out = out.at[i * d:(i + 1) * d, i * d:(i + 1) * d].set(w[i])
    return out


def _rope_tables(positions):
    B, S = positions.shape
    half = ROPE_DIMS // 2
    inv_freq = ROPE_THETA ** (-jnp.arange(0, ROPE_DIMS, 2, dtype=F32) / ROPE_DIMS)
    d = jnp.arange(LANES) % HEAD_DIM
    ang = positions.astype(F32).reshape(B * S, 1) * inv_freq[d % half][None, :]
    cos = jnp.cos(ang)
    sin = jnp.sin(ang)
    c = jnp.where(d < ROPE_DIMS, cos, 1.0)
    s1 = jnp.where(d < half, -sin, 0.0)
    s2 = jnp.where((d >= half) & (d < ROPE_DIMS), sin, 0.0)
    return c, s1, s2


def _mixer_layer(x, xb, rope, B, S, w_in, conv_w, conv_b, ga_w, ga_b, gx_w, gx_b, lam,
                 mu, w0, w_up, a0, a_up, g_up, k_k, k_a, r_k, lnx_g, lnx_b, w_out, ln_g, ln_b):
    W = RWKV_WIDTH
    row = lambda t: t.reshape(1, -1)
    proj = _in_proj(xb, w_in.astype(BF16))
    q, k, vt, pen = _attn_prep(proj, *rope, B, S)
    att = _attention(q, k, vt, pen, B, S)
    lru = _rglru(proj, conv_w, row(conv_b), _block_diag(ga_w), row(ga_b), _block_diag(gx_w), row(gx_b), row(lam), B, S)
    head_ones = _block_diag(jnp.ones((RWKV_HEADS, HEAD_DIM, HEAD_DIM), F32))
    wlr = jnp.zeros((W, 3 * W), F32)
    wlr = wlr.at[0:64, 0:W].set(w_up).at[64:128, W:2 * W].set(a_up).at[128:256, 2 * W:3 * W].set(g_up)
    r, ld, k2, v, kn, a, g = _rwkv_prep(proj, row(mu), wlr, row(w0), row(a0), row(k_k), row(k_a), head_ones, B, S)
    y = _wkv(r, ld, k2, v, kn, a, B, S)
    rwk = _wkv_post(y, r, k2, v, g, row(r_k), row(lnx_g), row(lnx_b), head_ones.astype(BF16))
    wo = w_out.astype(BF16)
    return _outproj_ln(att, lru, rwk, x, wo[0:ATT_WIDTH], wo[ATT_WIDTH:ATT_WIDTH + LRU_WIDTH],
                       wo[ATT_WIDTH + LRU_WIDTH:], row(ln_g), row(ln_b))


def kernel(x, positions, w_in, lru_conv_w, lru_conv_b, lru_ga_w, lru_ga_b, lru_gx_w, lru_gx_b, lru_lambda, rwkv_mu, rwkv_w0, rwkv_w_up, rwkv_a0, rwkv_a_up, rwkv_g_up, rwkv_k_k, rwkv_k_a, rwkv_r_k, rwkv_lnx_g, rwkv_lnx_b, w_out, ln1_g, ln1_b, ffn_w_gate, ffn_w_up, ffn_w_down, moe_router, moe_w_gate, moe_w_up, moe_w_down, ln2_g, ln2_b):
    B, S, D = x.shape
    T = B * S
    rope = _rope_tables(positions)
    xf = x.reshape(T, D)
    xb = xf.astype(BF16)
    for l in range(DEPTH):
        xf, xb = _mixer_layer(xf, xb, rope, B, S, w_in[l], lru_conv_w[l], lru_conv_b[l], lru_ga_w[l], lru_ga_b[l],
                              lru_gx_w[l], lru_gx_b[l], lru_lambda[l], rwkv_mu[l], rwkv_w0[l], rwkv_w_up[l],
                              rwkv_a0[l], rwkv_a_up[l], rwkv_g_up[l], rwkv_k_k[l], rwkv_k_a[l], rwkv_r_k[l],
                              rwkv_lnx_g[l], rwkv_lnx_b[l], w_out[l], ln1_g[l], ln1_b[l])
        g2 = ln2_g[l].reshape(1, D)
        b2 = ln2_b[l].reshape(1, D)
        if l % 2 == 0:
            i = l // 2
            xf, xb = _ffn_ln(xb, xf, ffn_w_gate[i].astype(BF16), ffn_w_up[i].astype(BF16),
                             ffn_w_down[i].astype(BF16), g2, b2)
        else:
            i = l // 2
            idx = jnp.arange(MOE_TILE)
            tri = (idx[:, None] <= idx[None, :]).astype(BF16)
            gates_t, pos_t, cnt = _router(xf, moe_router[i].T, tri)
            counts = cnt[:, :, 0].astype(jnp.int32).reshape(-1)
            xf = _moe_ln(xb, xf, gates_t, pos_t, counts, moe_w_gate[i].astype(BF16), moe_w_up[i].astype(BF16),
                         moe_w_down[i].astype(BF16), g2, b2)
    return xf.reshape(B, S, D)
```

```python
import functools

import jax
import jax.numpy as jnp
from jax import lax
from jax.experimental import pallas as pl
from jax.experimental.pallas import tpu as pltpu

F32 = jnp.float32
BF16 = jnp.bfloat16
HI = lax.Precision.HIGHEST
NT_DIMS = (((1,), (1,)), ((), ()))
TN_DIMS = (((0,), (0,)), ((), ()))

HEAD_DIM = 64
ATT_HEADS = 8
ATT_WIDTH = ATT_HEADS * HEAD_DIM
LRU_WIDTH = 256
RWKV_HEADS = 4
RWKV_WIDTH = 256
ROPE_DIMS = 16
ROPE_THETA = 500000.0
MOBA_BLOCK = 256
MOBA_TOPK = 3
CONV_WIDTH = 4
LRU_C = 8.0
RWKV_LN_EPS = 64e-5
N_EXPERTS = 8
LN_EPS = 1e-5
DEPTH = 2
DEEPNORM_ALPHA = (2 * DEPTH) ** 0.25
NEG_INF = -1e30
LOG2_E = 1.4426950408889634

LANES = 128
VMEM_LIMIT = 56 * 1024 * 1024

ATTN_STREAMS = 4
WKV_CHUNK = 64
WKV_GROUP = 8
LRU_CHUNK = 512
RWKV_PREP_ROWS = 1024
MOE_TILE = 1024
MOE_ROWS = 256


def _params(*sem):
    return pltpu.CompilerParams(dimension_semantics=sem, vmem_limit_bytes=VMEM_LIMIT)


def _sigmoid(x):
    return 1.0 / (1.0 + jnp.exp(-x))


def _softplus(x):
    return jnp.maximum(x, 0.0) + jnp.log1p(jnp.exp(-jnp.abs(x)))


def _expm1(z):
    u = jnp.exp(z)
    um1 = u - 1.0
    return jnp.where(u == 1.0, z, jnp.where(um1 == -1.0, -1.0, um1 * z / jnp.log(u)))


def _layer_norm(y, g, b):
    m = jnp.mean(y, axis=-1, keepdims=True)
    d = y - m
    var = jnp.mean(d * d, axis=-1, keepdims=True)
    return d * lax.rsqrt(var + LN_EPS) * g + b


def _matmul_kernel(x_ref, w_ref, o_ref):
    o_ref[...] = jnp.dot(x_ref[...], w_ref[...], preferred_element_type=F32)


def _in_proj(xb, w):
    T, D = xb.shape
    N = w.shape[1]
    tm, tn = 1024, 1024
    return pl.pallas_call(
        _matmul_kernel,
        grid=(N // tn, T // tm),
        in_specs=[pl.BlockSpec((tm, D), lambda n, m: (m, 0)),
                  pl.BlockSpec((D, tn), lambda n, m: (0, n))],
        out_specs=pl.BlockSpec((tm, tn), lambda n, m: (m, n)),
        out_shape=jax.ShapeDtypeStruct((T, N), F32),
        compiler_params=_params("arbitrary", "arbitrary"),
        name="in_proj",
    )(xb, w)


def _attn_prep_kernel(proj_ref, c_ref, s1_ref, s2_ref, q_ref, k_ref, vt_ref, pen_ref, km_ref, *, nb):
    i = pl.program_id(1)

    @pl.when(i == 0)
    def _():
        km_ref[...] = jnp.zeros_like(km_ref)

    c = c_ref[...]
    s1 = s1_ref[...]
    s2 = s2_ref[...]

    def rope(xt):
        return xt * c + pltpu.roll(xt, LANES - ROPE_DIMS // 2, 1) * s1 + pltpu.roll(xt, ROPE_DIMS // 2, 1) * s2

    q_tiles = []
    for ct in range(ATT_WIDTH // LANES):
        lo, hi = ct * LANES, (ct + 1) * LANES
        qr = rope(proj_ref[:, lo:hi])
        q_tiles.append(qr)
        q_ref[:, lo:hi] = (qr * (HEAD_DIM ** -0.5 * LOG2_E)).astype(BF16)
        kr = rope(proj_ref[:, ATT_WIDTH + lo:ATT_WIDTH + hi])
        k_ref[0, 0, :, lo:hi] = kr.astype(BF16)
        km_row = lax.broadcasted_iota(jnp.int32, (nb, LANES), 0)
        km_ref[:, lo:hi] = jnp.where(km_row == i, jnp.mean(kr, axis=0, keepdims=True), km_ref[:, lo:hi])
    vt_ref[0, 0] = proj_ref[:, 2 * ATT_WIDTH:3 * ATT_WIDTH].T.astype(BF16)

    q_rot = jnp.concatenate(q_tiles, axis=1)
    km = km_ref[...]
    km_rows = jnp.concatenate([km] * ATT_HEADS, axis=0)
    row_head = lax.broadcasted_iota(jnp.int32, km_rows.shape, 0) // nb
    lane_head = lax.broadcasted_iota(jnp.int32, km_rows.shape, 1) // HEAD_DIM
    g_all = lax.dot_general(jnp.where(row_head == lane_head, km_rows, 0.0), q_rot, NT_DIMS,
                            precision=HI, preferred_element_type=F32)
    n_iota = lax.broadcasted_iota(jnp.int32, (nb, MOBA_BLOCK), 0)
    past = n_iota < i
    for h in range(ATT_HEADS):
        g = g_all[h * nb:(h + 1) * nb, :]
        rank = jnp.zeros((nb, MOBA_BLOCK), F32)
        for m in range(nb):
            gm = g[m:m + 1, :]
            beats = jnp.where(gm > g, 1.0, jnp.where(gm == g, jnp.where(n_iota > m, 1.0, 0.0), 0.0))
            rank = rank + jnp.where(m < i, beats, 0.0)
        pen = jnp.where(past, jnp.where(rank < float(MOBA_TOPK), 0.0, NEG_INF), NEG_INF)
        half = (h % 2) * MOBA_BLOCK
        pen_ref[0, h // 2, :, half:half + MOBA_BLOCK] = pen


def _attn_prep(proj, rope_c, rope_s1, rope_s2, B, S):
    T = B * S
    nb = S // MOBA_BLOCK
    blk = MOBA_BLOCK
    return pl.pallas_call(
        functools.partial(_attn_prep_kernel, nb=nb),
        grid=(B, nb),
        in_specs=[pl.BlockSpec((blk, 3 * ATT_WIDTH), lambda b, i: (b * nb + i, 0)),
                  pl.BlockSpec((blk, LANES), lambda b, i: (b * nb + i, 0)),
                  pl.BlockSpec((blk, LANES), lambda b, i: (b * nb + i, 0)),
                  pl.BlockSpec((blk, LANES), lambda b, i: (b * nb + i, 0))],
        out_specs=[pl.BlockSpec((blk, ATT_WIDTH), lambda b, i: (b * nb + i, 0)),
                   pl.BlockSpec((1, 1, blk, ATT_WIDTH), lambda b, i: (b, i, 0, 0)),
                   pl.BlockSpec((1, 1, ATT_WIDTH, blk), lambda b, i: (b, i, 0, 0)),
                   pl.BlockSpec((1, ATT_HEADS // 2, nb, 2 * blk), lambda b, i: (b, 0, 0, i))],
        out_shape=[jax.ShapeDtypeStruct((T, ATT_WIDTH), BF16),
                   jax.ShapeDtypeStruct((B, nb, blk, ATT_WIDTH), BF16),
                   jax.ShapeDtypeStruct((B, nb, ATT_WIDTH, blk), BF16),
                   jax.ShapeDtypeStruct((B, ATT_HEADS // 2, nb, 2 * S), F32)],
        scratch_shapes=[pltpu.VMEM((nb, ATT_WIDTH), F32)],
        compiler_params=_params("arbitrary", "arbitrary"),
        name="attn_prep",
    )(proj, rope_c, rope_s1, rope_s2)


def _attn_kernel(q_ref, k_ref, vt_ref, pen_ref, o_ref):
    i = pl.program_id(2)
    blk = MOBA_BLOCK
    streams = range(ATTN_STREAMS)
    each = lambda fn, *xs: [fn(*(x[t] for x in xs)) for t in streams]
    lanes = lambda t: slice(t * LANES, (t + 1) * LANES)
    lane_head = lax.broadcasted_iota(jnp.int32, (blk, LANES), 1) // HEAD_DIM
    kidx = lax.broadcasted_iota(jnp.int32, (blk, 2 * blk), 0)
    qidx = lax.broadcasted_iota(jnp.int32, (blk, 2 * blk), 1) % blk
    nt = lambda a, b: lax.dot_general(a, b, NT_DIMS, preferred_element_type=F32)
    nn = lambda a, b: jnp.dot(a, b, preferred_element_type=F32)
    col_max = lambda x: jnp.max(x, axis=0, keepdims=True)
    ones_tile = jnp.ones((16, blk), BF16)

    def stacked_q(t):
        q = q_ref[:, lanes(t)]
        zero = jnp.zeros_like(q)
        return jnp.concatenate([jnp.where(lane_head == 0, q, zero), jnp.where(lane_head == 1, q, zero)], axis=0)

    qs = [stacked_q(t) for t in streams]
    s = [jnp.where(kidx <= qidx, nt(k_ref[0, i, :, lanes(t)], qs[t]), NEG_INF) for t in streams]
    m0 = each(col_max, s)
    p = each(lambda x, m: jnp.exp2(x - m), s, m0)
    pv = [nn(jnp.concatenate([vt_ref[0, i, lanes(t), :], ones_tile], axis=0), p[t].astype(BF16)) for t in streams]
    l0 = [x[LANES:LANES + 1] for x in pv]
    acc0 = [x[0:LANES] for x in pv]

    def body(jj, carry):
        m, l, acc = carry
        j0 = 2 * jj
        j1 = j0 + 1
        s0 = [nt(k_ref[0, j0, :, lanes(t)], qs[t]) + pen_ref[0, t, pl.ds(j0, 1), :] for t in streams]
        s1 = [nt(k_ref[0, j1, :, lanes(t)], qs[t]) + pen_ref[0, t, pl.ds(j1, 1), :] for t in streams]
        m_new = each(lambda mo, a, b: jnp.maximum(mo, jnp.maximum(col_max(a), col_max(b))), m, s0, s1)
        alpha = each(lambda mo, mn: jnp.exp2(mo - mn), m, m_new)
        p0 = each(lambda a, mn: jnp.exp2(a - mn), s0, m_new)
        p1 = each(lambda a, mn: jnp.exp2(a - mn), s1, m_new)
        pb0 = [x.astype(BF16) for x in p0]
        pb1 = [x.astype(BF16) for x in p1]
        pv0 = [nn(jnp.concatenate([vt_ref[0, j0, lanes(t), :], ones_tile], axis=0), pb0[t]) for t in streams]
        pv1 = [nn(jnp.concatenate([vt_ref[0, j1, lanes(t), :], ones_tile], axis=0), pb1[t]) for t in streams]
        l = each(lambda al, lo, a, b: al * lo + a[LANES:LANES + 1] + b[LANES:LANES + 1], alpha, l, pv0, pv1)
        acc = each(lambda ac, al, a, b: ac * al + a[0:LANES] + b[0:LANES], acc, alpha, pv0, pv1)
        return tuple(m_new), tuple(l), tuple(acc)

    _, l, acc = lax.fori_loop(0, (i + 1) // 2, body, (tuple(m0), tuple(l0), tuple(acc0)))
    row = lax.broadcasted_iota(jnp.int32, (LANES, blk), 0)
    for t in streams:
        out = acc[t] / l[t]
        o_ref[:, lanes(t)] = jnp.where(row < HEAD_DIM, out[:, 0:blk], out[:, blk:2 * blk]).T


def _attention(q, k, vt, pen, B, S):
    T = B * S
    nb = S // MOBA_BLOCK
    blk = MOBA_BLOCK
    w = ATTN_STREAMS * LANES
    n_groups = ATT_WIDTH // w
    return pl.pallas_call(
        _attn_kernel,
        grid=(B, n_groups, nb),
        in_specs=[pl.BlockSpec((blk, w), lambda b, hp, i: (b * nb + i, hp)),
                  pl.BlockSpec((1, nb, blk, w), lambda b, hp, i: (b, 0, 0, hp)),
                  pl.BlockSpec((1, nb, w, blk), lambda b, hp, i: (b, 0, hp, 0)),
                  pl.BlockSpec((1, ATTN_STREAMS, nb, 2 * blk), lambda b, hp, i: (b, hp, 0, i))],
        out_specs=pl.BlockSpec((blk, w), lambda b, hp, i: (b * nb + i, hp)),
        out_shape=jax.ShapeDtypeStruct((T, ATT_WIDTH), F32),
        compiler_params=_params("arbitrary", "arbitrary", "arbitrary"),
        name="moba_attn",
    )(q, k, vt, pen)


def _lru_kernel(p_ref, cw_ref, cb_ref, ga_ref, gab_ref, gx_ref, gxb_ref, lam_ref, o_ref, xbuf, hc):
    t = pl.program_id(1)
    tc = LRU_CHUNK
    pad = 8

    @pl.when(t == 0)
    def _():
        xbuf[0:pad, :] = jnp.zeros((pad, LRU_WIDTH), F32)
        hc[...] = jnp.zeros_like(hc)

    @pl.when(t > 0)
    def _():
        xbuf[0:pad, :] = xbuf[tc:tc + pad, :]

    x = p_ref[:, 0:LRU_WIDTH]
    gate = p_ref[:, LRU_WIDTH:2 * LRU_WIDTH]
    xbuf[pad:pad + tc, :] = x
    xc = cb_ref[...] + x * cw_ref[CONV_WIDTH - 1:CONV_WIDTH, :]
    for j in range(CONV_WIDTH - 1):
        back = CONV_WIDTH - 1 - j
        xc = xc + xbuf[pad - back:pad - back + tc, :] * cw_ref[j:j + 1, :]

    r = _sigmoid(jnp.dot(xc, ga_ref[...], precision=HI, preferred_element_type=F32) + gab_ref[...])
    ig = _sigmoid(jnp.dot(xc, gx_ref[...], precision=HI, preferred_element_type=F32) + gxb_ref[...])
    log_a = -LRU_C * r * _softplus(-lam_ref[...])
    a = jnp.exp(log_a)
    u = jnp.sqrt(-_expm1(2.0 * log_a)) * (ig * xc)

    rows = lax.broadcasted_iota(jnp.int32, (tc, LRU_WIDTH), 0)
    d = 1
    while d < tc:
        keep = rows >= d
        a_prev = jnp.where(keep, pltpu.roll(a, d, 0), 1.0)
        u_prev = jnp.where(keep, pltpu.roll(u, d, 0), 0.0)
        u = a * u_prev + u
        a = a * a_prev
        d *= 2
    h = u + a * hc[...]
    hc[...] = h[tc - 1:tc, :]
    gl = 0.5 * gate * (1.0 + jnp.tanh(0.7978845608028654 * (gate + 0.044715 * gate * gate * gate)))
    o_ref[...] = h * gl


def _rglru(proj, cw, cb, ga, gab, gx, gxb, lam, B, S):
    T = B * S
    tc = LRU_CHUNK
    nt = S // tc
    col = (3 * ATT_WIDTH) // (2 * LRU_WIDTH)
    vec = lambda: pl.BlockSpec((1, LRU_WIDTH), lambda b, t: (0, 0))
    mat = lambda: pl.BlockSpec((LRU_WIDTH, LRU_WIDTH), lambda b, t: (0, 0))
    return pl.pallas_call(
        _lru_kernel,
        grid=(B, nt),
        in_specs=[pl.BlockSpec((tc, 2 * LRU_WIDTH), lambda b, t: (b * nt + t, col)),
                  pl.BlockSpec((CONV_WIDTH, LRU_WIDTH), lambda b, t: (0, 0)),
                  vec(), mat(), vec(), mat(), vec(), vec()],
        out_specs=pl.BlockSpec((tc, LRU_WIDTH), lambda b, t: (b * nt + t, 0)),
        out_shape=jax.ShapeDtypeStruct((T, LRU_WIDTH), F32),
        scratch_shapes=[pltpu.VMEM((tc + 8, LRU_WIDTH), F32), pltpu.VMEM((1, LRU_WIDTH), F32)],
        compiler_params=_params("arbitrary", "arbitrary"),
        name="rglru",
    )(proj, cw, cb, ga, gab, gx, gxb, lam)


def _rwkv_prep_kernel(p_ref, pp_ref, mu_ref, wlr_ref, w0_ref, a0_ref, kk_ref, ka_ref, ones_ref,
                      r_ref, ld_ref, k2_ref, v_ref, kn_ref, a_ref, g_ref):
    i = pl.program_id(1)
    W = RWKV_WIDTH
    p = p_ref[...]
    rows = lax.broadcasted_iota(jnp.int32, p.shape, 0)
    prev_last = jnp.where(i == 0, 0.0, pp_ref[7:8, :])
    p_prev = jnp.where(rows == 0, prev_last, pltpu.roll(p, 1, 0))
    pf = p + (p_prev - p) * mu_ref[...]
    r = pf[:, 0:W]
    k = pf[:, W:2 * W]
    v = pf[:, 2 * W:3 * W]
    z = pf[:, 3 * W:4 * W]
    lane = lax.broadcasted_iota(jnp.int32, z.shape, 1)
    zz = jnp.where(lane < 64, jnp.tanh(z), jnp.where(lane < 128, z, _sigmoid(z)))
    lr = jnp.dot(zz, wlr_ref[...], precision=HI, preferred_element_type=F32)
    w = -_softplus(-(w0_ref[...] + lr[:, 0:W])) - 0.5
    a = _sigmoid(a0_ref[...] + lr[:, W:2 * W])
    kk = k * kk_ref[...]
    ssq = jnp.dot(kk * kk, ones_ref[...], precision=HI, preferred_element_type=F32)
    kk = kk / jnp.maximum(jnp.sqrt(ssq), 1e-12)
    r_ref[...] = r
    ld_ref[...] = -jnp.exp(w)
    k2_ref[...] = k * (1.0 + (a - 1.0) * ka_ref[...])
    v_ref[...] = v
    kn_ref[...] = kk
    a_ref[...] = a
    g_ref[...] = lr[:, 2 * W:3 * W]


def _rwkv_prep(proj, mu, wlr, w0, a0, k_k, k_a, head_ones, B, S):
    T = B * S
    tr = RWKV_PREP_ROWS
    nt = S // tr
    W = RWKV_WIDTH
    col = 2
    vec = lambda: pl.BlockSpec((1, W), lambda b, t: (0, 0))
    out = lambda: pl.BlockSpec((tr, W), lambda b, t: (b * nt + t, 0))
    return pl.pallas_call(
        _rwkv_prep_kernel,
        grid=(B, nt),
        in_specs=[pl.BlockSpec((tr, 4 * W), lambda b, t: (b * nt + t, col)),
                  pl.BlockSpec((8, 4 * W), lambda b, t: (jnp.maximum((b * nt + t) * (tr // 8) - 1, 0), col)),
                  pl.BlockSpec((1, 4 * W), lambda b, t: (0, 0)),
                  pl.BlockSpec((W, 3 * W), lambda b, t: (0, 0)),
                  vec(), vec(), vec(), vec(),
                  pl.BlockSpec((W, W), lambda b, t: (0, 0))],
        out_specs=[out() for _ in range(7)],
        out_shape=[jax.ShapeDtypeStruct((T, W), F32) for _ in range(7)],
        compiler_params=_params("arbitrary", "arbitrary"),
        name="rwkv_prep",
    )(proj, proj, mu, wlr, w0, a0, k_k, k_a, head_ones)


def _wkv_kernel(r_ref, ld_ref, k2_ref, v_ref, kn_ref, a_ref, y_ref, st_ref):
    c = pl.program_id(1)
    C = WKV_CHUNK
    W = RWKV_WIDTH
    H = RWKV_HEADS

    @pl.when(c == 0)
    def _():
        st_ref[...] = jnp.zeros_like(st_ref)

    def mm(x, y):
        return jnp.dot(x.astype(BF16), y.astype(BF16), preferred_element_type=F32)

    def mm_nt(x, y):
        return lax.dot_general(x.astype(BF16), y.astype(BF16), NT_DIMS, preferred_element_type=F32)

    def mm_tn(x, y):
        return lax.dot_general(x.astype(BF16), y.astype(BF16), TN_DIMS, preferred_element_type=F32)

    tr = lax.broadcasted_iota(jnp.int32, (C, C), 0)
    tc = lax.broadcasted_iota(jnp.int32, (C, C), 1)
    tri = jnp.where(tr >= tc, 1.0, 0.0)
    row = lax.broadcasted_iota(jnp.int32, (W, W), 0)
    colm = lax.broadcasted_iota(jnp.int32, (W, W), 1)
    same_head = (row // C) == (colm // HEAD_DIM)
    strict = lambda x: jnp.where(same_head, jnp.where(row > colm, x, 0.0), 0.0)
    incl = lambda x: jnp.where(same_head, jnp.where(row >= colm, x, 0.0), 0.0)
    same16 = (row // 16) == (colm // 16)
    eye = jnp.where(row == colm, 1.0, 0.0)

    def stack(x):
        return jnp.where(same_head, jnp.concatenate([x] * H, axis=0), 0.0)

    G = range(WKV_GROUP)
    each = lambda fn, *xs: [fn(*(x[g] for x in xs)) for g in G]

    r = [r_ref[g] for g in G]
    ld = [ld_ref[g] for g in G]
    k2 = [k2_ref[g] for g in G]
    v_s = [stack(v_ref[g]) for g in G]
    kn = [kn_ref[g] for g in G]
    b_s = [kn[g] * a_ref[g] for g in G]
    cl = each(lambda x: jnp.dot(tri, x, precision=HI, preferred_element_type=F32), ld)
    cl_end = [x[C - 1:C, :] for x in cl]
    e_neg = [jnp.exp(-x) for x in cl]
    e_end = each(lambda ce, x: jnp.exp(ce - x), cl_end, cl)
    ar_s = [jnp.concatenate([stack(-kn[g] * jnp.exp(cl[g] - ld[g])), stack(r[g] * jnp.exp(cl[g]))], axis=0) for g in G]
    bk_s = [jnp.concatenate([stack(b_s[g] * e_neg[g]), stack(k2[g] * e_neg[g])], axis=0) for g in G]
    end_s = [jnp.concatenate([stack(b_s[g] * e_end[g]), stack(k2[g] * e_end[g])], axis=0) for g in G]

    prod = each(mm_nt, ar_s, bk_s)
    n_ab = [strict(p[0:W, 0:W]) for p in prod]
    a_ak = [strict(p[0:W, W:2 * W]) for p in prod]
    a_rb = [incl(p[W:2 * W, 0:W]) for p in prod]
    a_rk = [incl(p[W:2 * W, W:2 * W]) for p in prod]

    nd = [jnp.where(same16, n, 0.0) for n in n_ab]
    lo = each(lambda n, d: n - d, n_ab, nd)
    n2 = each(mm, nd, nd)
    p1 = [eye + d for d in nd]
    n4 = each(mm, n2, n2)
    p2 = each(lambda p, n: p + mm(p, n), p1, n2)
    n8 = each(mm, n4, n4)
    p3 = each(lambda p, n: p + mm(p, n), p2, n4)
    dinv = each(lambda p, n: p + mm(p, n), p3, n8)
    x1 = each(mm, dinv, lo)
    x2 = each(mm, x1, x1)
    y1 = [eye + x for x in x1]
    m_inv = each(lambda y, x: y + mm(y, x), y1, x2)
    tinv = each(mm, m_inv, dinv)

    st = [st_ref[g] for g in G]
    ar_st = each(mm_nt, ar_s, st)
    akv = each(mm, a_ak, v_s)
    u_s = each(lambda t, a, b: mm(t, a[0:W] + b), tinv, ar_st, akv)
    rkv = each(mm, a_rk, v_s)
    y_s = each(lambda a, m, u, b: a[W:2 * W] + mm(m, u) + b, ar_st, a_rb, u_s, rkv)
    upd = each(lambda u, v, e: mm_tn(jnp.concatenate([u, v], axis=0), e), u_s, v_s, end_s)
    for g in G:
        y = y_s[g][0:C, :]
        for h in range(1, H):
            y = y + y_s[g][h * C:(h + 1) * C, :]
        y_ref[g] = y
        st_ref[g] = st[g] * jnp.exp(cl_end[g]) + upd[g]


def _wkv(r, ld, k2, v, kn, a, B, S):
    C = WKV_CHUNK
    nc = S // C
    W = RWKV_WIDTH
    G = WKV_GROUP
    blk = lambda: pl.BlockSpec((G, C, W), lambda b, c: (b, c, 0))
    seq = lambda t: t.reshape(B, S, W)
    y = pl.pallas_call(
        _wkv_kernel,
        grid=(B // G, nc),
        in_specs=[blk() for _ in range(6)],
        out_specs=blk(),
        out_shape=jax.ShapeDtypeStruct((B, S, W), F32),
        scratch_shapes=[pltpu.VMEM((G, W, W), F32)],
        compiler_params=_params("arbitrary", "arbitrary"),
        name="wkv7",
    )(seq(r), seq(ld), seq(k2), seq(v), seq(kn), seq(a))
    return y.reshape(B * S, W)


def _wkv_post_kernel(y_ref, r_ref, k2_ref, v_ref, g_ref, rk_ref, lg_ref, lb_ref, ones_ref, o_ref):
    ones = ones_ref[...]

    def head_sum(x):
        hi = x.astype(BF16)
        lo = (x - hi.astype(F32)).astype(BF16)
        return (jnp.dot(hi, ones, preferred_element_type=F32) + jnp.dot(lo, ones, preferred_element_type=F32))

    y = y_ref[...]
    r = r_ref[...]
    k2 = k2_ref[...]
    inv_n = 1.0 / HEAD_DIM
    d = y - head_sum(y) * inv_n
    var = head_sum(d * d) * inv_n
    yn = d * lax.rsqrt(var + RWKV_LN_EPS) * lg_ref[...] + lb_ref[...]
    bonus = head_sum(r * k2 * rk_ref[...]) * v_ref[...]
    o_ref[...] = (yn + bonus) * g_ref[...]


def _wkv_post(y, r, k2, v, g, r_k, lnx_g, lnx_b, head_ones_bf16):
    T, W = y.shape
    tm = 1024
    blk = lambda: pl.BlockSpec((tm, W), lambda m: (m, 0))
    vec = lambda: pl.BlockSpec((1, W), lambda m: (0, 0))
    return pl.pallas_call(
        _wkv_post_kernel,
        grid=(T // tm,),
        in_specs=[blk() for _ in range(5)] + [vec(), vec(), vec(), pl.BlockSpec((W, W), lambda m: (0, 0))],
        out_specs=blk(),
        out_shape=jax.ShapeDtypeStruct((T, W), F32),
        compiler_params=_params("arbitrary"),
        name="wkv_post",
    )(y, r, k2, v, g, r_k, lnx_g, lnx_b, head_ones_bf16)


def _outproj_ln_kernel(att_ref, lru_ref, rwk_ref, x_ref, wa_ref, wl_ref, wr_ref, g_ref, b_ref, o_ref, ob_ref):
    h = jnp.dot(att_ref[...].astype(BF16), wa_ref[...], preferred_element_type=F32)
    h = h + jnp.dot(lru_ref[...].astype(BF16), wl_ref[...], preferred_element_type=F32)
    h = h + jnp.dot(rwk_ref[...].astype(BF16), wr_ref[...], preferred_element_type=F32)
    y = _layer_norm(DEEPNORM_ALPHA * x_ref[...] + h, g_ref[...], b_ref[...])
    o_ref[...] = y
    ob_ref[...] = y.astype(BF16)


def _outproj_ln(att, lru, rwk, x, wa, wl, wr, g, b):
    T, D = x.shape
    tm = 1024
    row = lambda w: pl.BlockSpec((tm, w), lambda m: (m, 0))
    full = lambda a: pl.BlockSpec(a.shape, lambda m: (0, 0))
    return pl.pallas_call(
        _outproj_ln_kernel,
        grid=(T // tm,),
        in_specs=[row(ATT_WIDTH), row(LRU_WIDTH), row(RWKV_WIDTH), row(D), full(wa), full(wl), full(wr), full(g), full(b)],
        out_specs=[row(D), row(D)],
        out_shape=[jax.ShapeDtypeStruct((T, D), F32), jax.ShapeDtypeStruct((T, D), BF16)],
        compiler_params=_params("arbitrary"),
        name="outproj_ln",
    )(att, lru, rwk, x, wa, wl, wr, g, b)


def _ffn_kernel(xb_ref, x_ref, wg_ref, wu_ref, wd_ref, g_ref, b_ref, o_ref, ob_ref, acc_ref, *, nf):
    f = pl.program_id(1)
    xb = xb_ref[...]
    hg = jnp.dot(xb, wg_ref[...], preferred_element_type=F32)
    hu = jnp.dot(xb, wu_ref[...], preferred_element_type=F32)
    hh = (hg * _sigmoid(hg) * hu).astype(BF16)
    contrib = jnp.dot(hh, wd_ref[...], preferred_element_type=F32)

    @pl.when(f == 0)
    def _():
        acc_ref[...] = contrib

    @pl.when(f > 0)
    def _():
        acc_ref[...] = acc_ref[...] + contrib

    @pl.when(f == nf - 1)
    def _():
        y = _layer_norm(DEEPNORM_ALPHA * x_ref[...] + acc_ref[...], g_ref[...], b_ref[...])
        o_ref[...] = y
        ob_ref[...] = y.astype(BF16)


def _ffn_ln(xb, x, wg, wu, wd, g, b):
    T, D = x.shape
    F = wg.shape[1]
    tm = 512
    nf = 2
    tf = F // nf
    return pl.pallas_call(
        functools.partial(_ffn_kernel, nf=nf),
        grid=(T // tm, nf),
        in_specs=[pl.BlockSpec((tm, D), lambda m, f: (m, 0)),
                  pl.BlockSpec((tm, D), lambda m, f: (m, 0)),
                  pl.BlockSpec((D, tf), lambda m, f: (0, f)),
                  pl.BlockSpec((D, tf), lambda m, f: (0, f)),
                  pl.BlockSpec((tf, D), lambda m, f: (f, 0)),
                  pl.BlockSpec((1, D), lambda m, f: (0, 0)),
                  pl.BlockSpec((1, D), lambda m, f: (0, 0))],
        out_specs=[pl.BlockSpec((tm, D), lambda m, f: (m, 0)), pl.BlockSpec((tm, D), lambda m, f: (m, 0))],
        out_shape=[jax.ShapeDtypeStruct((T, D), F32), jax.ShapeDtypeStruct((T, D), BF16)],
        scratch_shapes=[pltpu.VMEM((tm, D), F32)],
        compiler_params=_params("arbitrary", "arbitrary"),
        name="ffn_ln",
    )(xb, x, wg, wu, wd, g, b)


def _router_kernel(x_ref, wrt_ref, tri_ref, g_ref, pos_ref, cnt_ref):
    tt = x_ref.shape[0]
    logits = lax.dot_general(wrt_ref[...], x_ref[...], NT_DIMS, precision=HI, preferred_element_type=F32)
    row = lax.broadcasted_iota(jnp.int32, logits.shape, 0).astype(F32)
    m1 = jnp.max(logits, axis=0, keepdims=True)
    i1 = jnp.min(jnp.where(logits == m1, row, float(N_EXPERTS)), axis=0, keepdims=True)
    rest = jnp.where(row == i1, -jnp.inf, logits)
    m2 = jnp.max(rest, axis=0, keepdims=True)
    i2 = jnp.min(jnp.where(rest == m2, row, float(N_EXPERTS)), axis=0, keepdims=True)
    e = jnp.exp(m2 - m1)
    g1 = 1.0 / (1.0 + e)
    g2 = e / (1.0 + e)
    g_ref[0] = jnp.where(row == i1, g1, jnp.where(row == i2, g2, 0.0))
    ind = jnp.where(row == i1, 1.0, jnp.where(row == i2, 1.0, 0.0))
    csum = jnp.dot(ind.astype(BF16), tri_ref[...], preferred_element_type=F32)
    pos_ref[0] = jnp.where(ind > 0.5, csum - 1.0, -1.0)
    cnt_ref[0] = jnp.broadcast_to(csum[:, tt - 1:tt], (N_EXPERTS, LANES))


def _router(x, wrt, tri):
    T, D = x.shape
    tt = MOE_TILE
    nt = T // tt
    tile = lambda: pl.BlockSpec((1, N_EXPERTS, tt), lambda m: (m, 0, 0))
    return pl.pallas_call(
        _router_kernel,
        grid=(nt,),
        in_specs=[pl.BlockSpec((tt, D), lambda m: (m, 0)), pl.BlockSpec((N_EXPERTS, D), lambda m: (0, 0)),
                  pl.BlockSpec((tt, tt), lambda m: (0, 0))],
        out_specs=[tile(), tile(), pl.BlockSpec((1, N_EXPERTS, LANES), lambda m: (m, 0, 0))],
        out_shape=[jax.ShapeDtypeStruct((nt, N_EXPERTS, tt), F32), jax.ShapeDtypeStruct((nt, N_EXPERTS, tt), F32),
                   jax.ShapeDtypeStruct((nt, N_EXPERTS, LANES), F32)],
        compiler_params=_params("arbitrary"),
        name="moe_router",
    )(x, wrt, tri)


def _moe_kernel(cnt_ref, xb_ref, x_ref, gate_ref, pos_ref, wg_ref, wu_ref, wd_ref, g_ref, b_ref,
                o_ref, xg_ref, yacc_ref, gcol_ref, *, nf):
    t = pl.program_id(0)
    e = pl.program_id(1)
    f = pl.program_id(2)
    tt = xb_ref.shape[0]
    R = MOE_ROWS
    half = R // 2
    n = cnt_ref[t * N_EXPERTS + e]
    rem = n % R
    n_full = n // R + jnp.where(rem > half, 1, 0)
    has_tail = jnp.logical_and(rem > 0, rem <= half)
    tail_start = pl.multiple_of(n_full * R, half)
    pos_row = pos_ref[0, pl.ds(e, 1), :]

    def selection(start, rows):
        slot = lax.broadcasted_iota(jnp.int32, (rows, tt), 0) + start
        return jnp.where(pos_row == slot.astype(F32), 1.0, 0.0)

    def gather(start, rows):
        sel = selection(start, rows)
        dst = pl.ds(start, rows)
        xg_ref[dst, :] = jnp.dot(sel.astype(BF16), xb_ref[...], preferred_element_type=F32).astype(BF16)
        gate_row = gate_ref[0, pl.ds(e, 1), :]
        gcol_ref[dst, :] = jnp.broadcast_to(jnp.sum(sel * gate_row, axis=1, keepdims=True), (rows, LANES))

    def expert(start, rows):
        dst = pl.ds(start, rows)
        xc = xg_ref[dst, :]
        hg = jnp.dot(xc, wg_ref[0], preferred_element_type=F32)
        hu = jnp.dot(xc, wu_ref[0], preferred_element_type=F32)
        contrib = jnp.dot((hg * _sigmoid(hg) * hu).astype(BF16), wd_ref[0], preferred_element_type=F32)

        @pl.when(f == 0)
        def _():
            yacc_ref[dst, :] = contrib

        @pl.when(f > 0)
        def _():
            yacc_ref[dst, :] = yacc_ref[dst, :] + contrib

        @pl.when(f == nf - 1)
        def _():
            gate = jnp.concatenate([gcol_ref[dst, :]] * (o_ref.shape[1] // LANES), axis=1)
            ys = (yacc_ref[dst, :] * gate).astype(BF16)
            o_ref[...] = o_ref[...] + lax.dot_general(selection(start, rows).astype(BF16), ys, TN_DIMS,
                                                      preferred_element_type=F32)

    def over_chunks(fn):
        def body(r, carry):
            fn(pl.multiple_of(r * R, R), R)
            return carry
        lax.fori_loop(0, n_full, body, 0)

        @pl.when(has_tail)
        def _():
            fn(tail_start, half)

    @pl.when(jnp.logical_and(e == 0, f == 0))
    def _():
        o_ref[...] = jnp.zeros_like(o_ref)

    @pl.when(f == 0)
    def _():
        over_chunks(gather)

    over_chunks(expert)

    @pl.when(jnp.logical_and(e == N_EXPERTS - 1, f == nf - 1))
    def _():
        o_ref[...] = _layer_norm(DEEPNORM_ALPHA * x_ref[...] + o_ref[...], g_ref[...], b_ref[...])


def _moe_ln(xb, x, gates_t, pos_t, counts, wg, wu, wd, g, b):
    T, D = xb.shape
    F = wg.shape[2]
    tt = MOE_TILE
    nf = 2
    tf = F // nf
    grid_spec = pltpu.PrefetchScalarGridSpec(
        num_scalar_prefetch=1,
        grid=(T // tt, N_EXPERTS, nf),
        in_specs=[pl.BlockSpec((tt, D), lambda m, e, f, c: (m, 0)),
                  pl.BlockSpec((tt, D), lambda m, e, f, c: (m, 0)),
                  pl.BlockSpec((1, N_EXPERTS, tt), lambda m, e, f, c: (m, 0, 0)),
                  pl.BlockSpec((1, N_EXPERTS, tt), lambda m, e, f, c: (m, 0, 0)),
                  pl.BlockSpec((1, D, tf), lambda m, e, f, c: (e, 0, f)),
                  pl.BlockSpec((1, D, tf), lambda m, e, f, c: (e, 0, f)),
                  pl.BlockSpec((1, tf, D), lambda m, e, f, c: (e, f, 0)),
                  pl.BlockSpec((1, D), lambda m, e, f, c: (0, 0)),
                  pl.BlockSpec((1, D), lambda m, e, f, c: (0, 0))],
        out_specs=pl.BlockSpec((tt, D), lambda m, e, f, c: (m, 0)),
        scratch_shapes=[pltpu.VMEM((tt, D), BF16), pltpu.VMEM((tt, D), F32), pltpu.VMEM((tt, LANES), F32)],
    )
    return pl.pallas_call(
        functools.partial(_moe_kernel, nf=nf),
        grid_spec=grid_spec,
        out_shape=jax.ShapeDtypeStruct((T, D), F32),
        compiler_params=_params("arbitrary", "arbitrary", "arbitrary"),
        name="moe_experts",
    )(counts, xb, x, gates_t, pos_t, wg, wu, wd, g, b)


def _block_diag(w):
    n, d, _ = w.shape
    out = jnp.zeros((n * d, n * d), w.dtype)
    for i in range(n):
        out = out.at[i * d:(i + 1) * d, i * d:(i + 1) * d].set(w[i])
    return out


def _rope_tables(positions):
    B, S = positions.shape
    half = ROPE_DIMS // 2
    inv_freq = ROPE_THETA ** (-jnp.arange(0, ROPE_DIMS, 2, dtype=F32) / ROPE_DIMS)
    d = jnp.arange(LANES) % HEAD_DIM
    ang = positions.astype(F32).reshape(B * S, 1) * inv_freq[d % half][None, :]
    cos = jnp.cos(ang)
    sin = jnp.sin(ang)
    c = jnp.where(d < ROPE_DIMS, cos, 1.0)
    s1 = jnp.where(d < half, -sin, 0.0)
    s2 = jnp.where((d >= half) & (d < ROPE_DIMS), sin, 0.0)
    return c, s1, s2


def _mixer_layer(x, xb, rope, B, S, w_in, conv_w, conv_b, ga_w, ga_b, gx_w, gx_b, lam,
                 mu, w0, w_up, a0, a_up, g_up, k_k, k_a, r_k, lnx_g, lnx_b, w_out, ln_g, ln_b):
    W = RWKV_WIDTH
    row = lambda t: t.reshape(1, -1)
    proj = _in_proj(xb, w_in.astype(BF16))
    q, k, vt, pen = _attn_prep(proj, *rope, B, S)
    att = _attention(q, k, vt, pen, B, S)
    lru = _rglru(proj, conv_w, row(conv_b), _block_diag(ga_w), row(ga_b), _block_diag(gx_w), row(gx_b), row(lam), B, S)
    head_ones = _block_diag(jnp.ones((RWKV_HEADS, HEAD_DIM, HEAD_DIM), F32))
    wlr = jnp.zeros((W, 3 * W), F32)
    wlr = wlr.at[0:64, 0:W].set(w_up).at[64:128, W:2 * W].set(a_up).at[128:256, 2 * W:3 * W].set(g_up)
    r, ld, k2, v, kn, a, g = _rwkv_prep(proj, row(mu), wlr, row(w0), row(a0), row(k_k), row(k_a), head_ones, B, S)
    y = _wkv(r, ld, k2, v, kn, a, B, S)
    rwk = _wkv_post(y, r, k2, v, g, row(r_k), row(lnx_g), row(lnx_b), head_ones.astype(BF16))
    wo = w_out.astype(BF16)
    return _outproj_ln(att, lru, rwk, x, wo[0:ATT_WIDTH], wo[ATT_WIDTH:ATT_WIDTH + LRU_WIDTH],
                       wo[ATT_WIDTH + LRU_WIDTH:], row(ln_g), row(ln_b))


def kernel(x, positions, w_in, lru_conv_w, lru_conv_b, lru_ga_w, lru_ga_b, lru_gx_w, lru_gx_b, lru_lambda, rwkv_mu, rwkv_w0, rwkv_w_up, rwkv_a0, rwkv_a_up, rwkv_g_up, rwkv_k_k, rwkv_k_a, rwkv_r_k, rwkv_lnx_g, rwkv_lnx_b, w_out, ln1_g, ln1_b, ffn_w_gate, ffn_w_up, ffn_w_down, moe_router, moe_w_gate, moe_w_up, moe_w_down, ln2_g, ln2_b):
    B, S, D = x.shape
    T = B * S
    rope = _rope_tables(positions)
    xf = x.reshape(T, D)
    xb = xf.astype(BF16)
    for l in range(DEPTH):
        xf, xb = _mixer_layer(xf, xb, rope, B, S, w_in[l], lru_conv_w[l], lru_conv_b[l], lru_ga_w[l], lru_ga_b[l],
                              lru_gx_w[l], lru_gx_b[l], lru_lambda[l], rwkv_mu[l], rwkv_w0[l], rwkv_w_up[l],
                              rwkv_a0[l], rwkv_a_up[l], rwkv_g_up[l], rwkv_k_k[l], rwkv_k_a[l], rwkv_r_k[l],
                              rwkv_lnx_g[l], rwkv_lnx_b[l], w_out[l], ln1_g[l], ln1_b[l])
        g2 = ln2_g[l].reshape(1, D)
        b2 = ln2_b[l].reshape(1, D)
        if l % 2 == 0:
            i = l // 2
            xf, xb = _ffn_ln(xb, xf, ffn_w_gate[i].astype(BF16), ffn_w_up[i].astype(BF16),
                             ffn_w_down[i].astype(BF16), g2, b2)
        else:
            i = l // 2
            idx = jnp.arange(MOE_TILE)
            tri = (idx[:, None] <= idx[None, :]).astype(BF16)
            gates_t, pos_t, cnt = _router(xf, moe_router[i].T, tri)
            counts = cnt[:, :, 0].astype(jnp.int32).reshape(-1)
            xf = _moe_ln(xb, xf, gates_t, pos_t, counts, moe_w_gate[i].astype(BF16), moe_w_up[i].astype(BF16),
                         moe_w_down[i].astype(BF16), g2, b2)
    return xf.reshape(B, S, D)
```

```python
import functools

import jax
import jax.numpy as jnp
from jax import lax
from jax.experimental import pallas as pl
from jax.experimental.pallas import tpu as pltpu

F32 = jnp.float32
BF16 = jnp.bfloat16
HI = lax.Precision.HIGHEST
NT_DIMS = (((1,), (1,)), ((), ()))
TN_DIMS = (((0,), (0,)), ((), ()))

HEAD_DIM = 64
ATT_HEADS = 8
ATT_WIDTH = ATT_HEADS * HEAD_DIM
LRU_WIDTH = 256
RWKV_HEADS = 4
RWKV_WIDTH = 256
ROPE_DIMS = 16
ROPE_THETA = 500000.0
MOBA_BLOCK = 256
MOBA_TOPK = 3
CONV_WIDTH = 4
LRU_C = 8.0
RWKV_LN_EPS = 64e-5
N_EXPERTS = 8
LN_EPS = 1e-5
DEPTH = 2
DEEPNORM_ALPHA = (2 * DEPTH) ** 0.25
NEG_INF = -1e30
LOG2_E = 1.4426950408889634

LANES = 128
VMEM_LIMIT = 56 * 1024 * 1024

ATTN_STREAMS = 4
WKV_CHUNK = 64
WKV_GROUP = 8
LRU_CHUNK = 512
RWKV_PREP_ROWS = 1024
MOE_TILE = 1024
MOE_ROWS = 288
MOE_ROWS_MAX = (MOE_TILE // MOE_ROWS + 1) * MOE_ROWS


def _params(*sem):
    return pltpu.CompilerParams(dimension_semantics=sem, vmem_limit_bytes=VMEM_LIMIT)


def _sigmoid(x):
    return 1.0 / (1.0 + jnp.exp(-x))


def _softplus(x):
    return jnp.maximum(x, 0.0) + jnp.log1p(jnp.exp(-jnp.abs(x)))


def _expm1(z):
    u = jnp.exp(z)
    um1 = u - 1.0
    return jnp.where(u == 1.0, z, jnp.where(um1 == -1.0, -1.0, um1 * z / jnp.log(u)))


def _layer_norm(y, g, b):
    m = jnp.mean(y, axis=-1, keepdims=True)
    d = y - m
    var = jnp.mean(d * d, axis=-1, keepdims=True)
    return d * lax.rsqrt(var + LN_EPS) * g + b


def _matmul_kernel(x_ref, w_ref, o_ref):
    o_ref[...] = jnp.dot(x_ref[...], w_ref[...], preferred_element_type=F32)


def _in_proj(xb, w):
    T, D = xb.shape
    N = w.shape[1]
    tm, tn = 1024, 1024
    return pl.pallas_call(
        _matmul_kernel,
        grid=(N // tn, T // tm),
        in_specs=[pl.BlockSpec((tm, D), lambda n, m: (m, 0)),
                  pl.BlockSpec((D, tn), lambda n, m: (0, n))],
        out_specs=pl.BlockSpec((tm, tn), lambda n, m: (m, n)),
        out_shape=jax.ShapeDtypeStruct((T, N), F32),
        compiler_params=_params("arbitrary", "arbitrary"),
        name="in_proj",
    )(xb, w)


def _attn_prep_kernel(proj_ref, c_ref, s1_ref, s2_ref, q_ref, k_ref, vt_ref, pen_ref, km_ref, *, nb):
    i = pl.program_id(1)

    @pl.when(i == 0)
    def _():
        km_ref[...] = jnp.zeros_like(km_ref)

    c = c_ref[...]
    s1 = s1_ref[...]
    s2 = s2_ref[...]

    def rope(xt):
        return xt * c + pltpu.roll(xt, LANES - ROPE_DIMS // 2, 1) * s1 + pltpu.roll(xt, ROPE_DIMS // 2, 1) * s2

    q_tiles = []
    for ct in range(ATT_WIDTH // LANES):
        lo, hi = ct * LANES, (ct + 1) * LANES
        qr = rope(proj_ref[:, lo:hi])
        q_tiles.append(qr)
        q_ref[:, lo:hi] = (qr * (HEAD_DIM ** -0.5 * LOG2_E)).astype(BF16)
        kr = rope(proj_ref[:, ATT_WIDTH + lo:ATT_WIDTH + hi])
        k_ref[0, 0, :, lo:hi] = kr.astype(BF16)
        km_row = lax.broadcasted_iota(jnp.int32, (nb, LANES), 0)
        km_ref[:, lo:hi] = jnp.where(km_row == i, jnp.mean(kr, axis=0, keepdims=True), km_ref[:, lo:hi])
    vt_ref[0, 0] = proj_ref[:, 2 * ATT_WIDTH:3 * ATT_WIDTH].T.astype(BF16)

    q_rot = jnp.concatenate(q_tiles, axis=1)
    km = km_ref[...]
    km_rows = jnp.concatenate([km] * ATT_HEADS, axis=0)
    row_head = lax.broadcasted_iota(jnp.int32, km_rows.shape, 0) // nb
    lane_head = lax.broadcasted_iota(jnp.int32, km_rows.shape, 1) // HEAD_DIM
    g_all = lax.dot_general(jnp.where(row_head == lane_head, km_rows, 0.0), q_rot, NT_DIMS,
                            precision=HI, preferred_element_type=F32)
    n_iota = lax.broadcasted_iota(jnp.int32, (nb, MOBA_BLOCK), 0)
    past = n_iota < i
    for h in range(ATT_HEADS):
        g = g_all[h * nb:(h + 1) * nb, :]
        rank = jnp.zeros((nb, MOBA_BLOCK), F32)
        for m in range(nb):
            gm = g[m:m + 1, :]
            beats = jnp.where(gm > g, 1.0, jnp.where(gm == g, jnp.where(n_iota > m, 1.0, 0.0), 0.0))
            rank = rank + jnp.where(m < i, beats, 0.0)
        pen = jnp.where(past, jnp.where(rank < float(MOBA_TOPK), 0.0, NEG_INF), NEG_INF)
        half = (h % 2) * MOBA_BLOCK
        pen_ref[0, h // 2, :, half:half + MOBA_BLOCK] = pen


def _attn_prep(proj, rope_c, rope_s1, rope_s2, B, S):
    T = B * S
    nb = S // MOBA_BLOCK
    blk = MOBA_BLOCK
    return pl.pallas_call(
        functools.partial(_attn_prep_kernel, nb=nb),
        grid=(B, nb),
        in_specs=[pl.BlockSpec((blk, 3 * ATT_WIDTH), lambda b, i: (b * nb + i, 0)),
                  pl.BlockSpec((blk, LANES), lambda b, i: (b * nb + i, 0)),
                  pl.BlockSpec((blk, LANES), lambda b, i: (b * nb + i, 0)),
                  pl.BlockSpec((blk, LANES), lambda b, i: (b * nb + i, 0))],
        out_specs=[pl.BlockSpec((blk, ATT_WIDTH), lambda b, i: (b * nb + i, 0)),
                   pl.BlockSpec((1, 1, blk, ATT_WIDTH), lambda b, i: (b, i, 0, 0)),
                   pl.BlockSpec((1, 1, ATT_WIDTH, blk), lambda b, i: (b, i, 0, 0)),
                   pl.BlockSpec((1, ATT_HEADS // 2, nb, 2 * blk), lambda b, i: (b, 0, 0, i))],
        out_shape=[jax.ShapeDtypeStruct((T, ATT_WIDTH), BF16),
                   jax.ShapeDtypeStruct((B, nb, blk, ATT_WIDTH), BF16),
                   jax.ShapeDtypeStruct((B, nb, ATT_WIDTH, blk), BF16),
                   jax.ShapeDtypeStruct((B, ATT_HEADS // 2, nb, 2 * S), F32)],
        scratch_shapes=[pltpu.VMEM((nb, ATT_WIDTH), F32)],
        compiler_params=_params("arbitrary", "arbitrary"),
        name="attn_prep",
    )(proj, rope_c, rope_s1, rope_s2)


def _attn_kernel(q_ref, k_ref, vt_ref, pen_ref, o_ref):
    i = pl.program_id(2)
    blk = MOBA_BLOCK
    streams = range(ATTN_STREAMS)
    each = lambda fn, *xs: [fn(*(x[t] for x in xs)) for t in streams]
    lanes = lambda t: slice(t * LANES, (t + 1) * LANES)
    lane_head = lax.broadcasted_iota(jnp.int32, (blk, LANES), 1) // HEAD_DIM
    kidx = lax.broadcasted_iota(jnp.int32, (blk, 2 * blk), 0)
    qidx = lax.broadcasted_iota(jnp.int32, (blk, 2 * blk), 1) % blk
    nt = lambda a, b: lax.dot_general(a, b, NT_DIMS, preferred_element_type=F32)
    nn = lambda a, b: jnp.dot(a, b, preferred_element_type=F32)
    col_max = lambda x: jnp.max(x, axis=0, keepdims=True)
    ones_tile = jnp.ones((16, blk), BF16)

    def stacked_q(t):
        q = q_ref[:, lanes(t)]
        zero = jnp.zeros_like(q)
        return jnp.concatenate([jnp.where(lane_head == 0, q, zero), jnp.where(lane_head == 1, q, zero)], axis=0)

    qs = [stacked_q(t) for t in streams]
    s = [jnp.where(kidx <= qidx, nt(k_ref[0, i, :, lanes(t)], qs[t]), NEG_INF) for t in streams]
    m0 = each(col_max, s)
    p = each(lambda x, m: jnp.exp2(x - m), s, m0)
    pv = [nn(jnp.concatenate([vt_ref[0, i, lanes(t), :], ones_tile], axis=0), p[t].astype(BF16)) for t in streams]
    l0 = [x[LANES:LANES + 1] for x in pv]
    acc0 = [x[0:LANES] for x in pv]

    def body(jj, carry):
        m, l, acc = carry
        j0 = 2 * jj
        j1 = j0 + 1
        s0 = [nt(k_ref[0, j0, :, lanes(t)], qs[t]) + pen_ref[0, t, pl.ds(j0, 1), :] for t in streams]
        s1 = [nt(k_ref[0, j1, :, lanes(t)], qs[t]) + pen_ref[0, t, pl.ds(j1, 1), :] for t in streams]
        m_new = each(lambda mo, a, b: jnp.maximum(mo, jnp.maximum(col_max(a), col_max(b))), m, s0, s1)
        alpha = each(lambda mo, mn: jnp.exp2(mo - mn), m, m_new)
        p0 = each(lambda a, mn: jnp.exp2(a - mn), s0, m_new)
        p1 = each(lambda a, mn: jnp.exp2(a - mn), s1, m_new)
        pb0 = [x.astype(BF16) for x in p0]
        pb1 = [x.astype(BF16) for x in p1]
        pv0 = [nn(jnp.concatenate([vt_ref[0, j0, lanes(t), :], ones_tile], axis=0), pb0[t]) for t in streams]
        pv1 = [nn(jnp.concatenate([vt_ref[0, j1, lanes(t), :], ones_tile], axis=0), pb1[t]) for t in streams]
        l = each(lambda al, lo, a, b: al * lo + a[LANES:LANES + 1] + b[LANES:LANES + 1], alpha, l, pv0, pv1)
        acc = each(lambda ac, al, a, b: ac * al + a[0:LANES] + b[0:LANES], acc, alpha, pv0, pv1)
        return tuple(m_new), tuple(l), tuple(acc)

    _, l, acc = lax.fori_loop(0, (i + 1) // 2, body, (tuple(m0), tuple(l0), tuple(acc0)))
    row = lax.broadcasted_iota(jnp.int32, (LANES, blk), 0)
    for t in streams:
        out = acc[t] / l[t]
        o_ref[:, lanes(t)] = jnp.where(row < HEAD_DIM, out[:, 0:blk], out[:, blk:2 * blk]).T


def _attention(q, k, vt, pen, B, S):
    T = B * S
    nb = S // MOBA_BLOCK
    blk = MOBA_BLOCK
    w = ATTN_STREAMS * LANES
    n_groups = ATT_WIDTH // w
    return pl.pallas_call(
        _attn_kernel,
        grid=(B, n_groups, nb),
        in_specs=[pl.BlockSpec((blk, w), lambda b, hp, i: (b * nb + i, hp)),
                  pl.BlockSpec((1, nb, blk, w), lambda b, hp, i: (b, 0, 0, hp)),
                  pl.BlockSpec((1, nb, w, blk), lambda b, hp, i: (b, 0, hp, 0)),
                  pl.BlockSpec((1, ATTN_STREAMS, nb, 2 * blk), lambda b, hp, i: (b, hp, 0, i))],
        out_specs=pl.BlockSpec((blk, w), lambda b, hp, i: (b * nb + i, hp)),
        out_shape=jax.ShapeDtypeStruct((T, ATT_WIDTH), F32),
        compiler_params=_params("arbitrary", "arbitrary", "arbitrary"),
        name="moba_attn",
    )(q, k, vt, pen)


def _lru_kernel(p_ref, cw_ref, cb_ref, ga_ref, gab_ref, gx_ref, gxb_ref, lam_ref, o_ref, xbuf, hc):
    t = pl.program_id(1)
    tc = LRU_CHUNK
    pad = 8

    @pl.when(t == 0)
    def _():
        xbuf[0:pad, :] = jnp.zeros((pad, LRU_WIDTH), F32)
        hc[...] = jnp.zeros_like(hc)

    @pl.when(t > 0)
    def _():
        xbuf[0:pad, :] = xbuf[tc:tc + pad, :]

    x = p_ref[:, 0:LRU_WIDTH]
    gate = p_ref[:, LRU_WIDTH:2 * LRU_WIDTH]
    xbuf[pad:pad + tc, :] = x
    xc = cb_ref[...] + x * cw_ref[CONV_WIDTH - 1:CONV_WIDTH, :]
    for j in range(CONV_WIDTH - 1):
        back = CONV_WIDTH - 1 - j
        xc = xc + xbuf[pad - back:pad - back + tc, :] * cw_ref[j:j + 1, :]

    r = _sigmoid(jnp.dot(xc, ga_ref[...], precision=HI, preferred_element_type=F32) + gab_ref[...])
    ig = _sigmoid(jnp.dot(xc, gx_ref[...], precision=HI, preferred_element_type=F32) + gxb_ref[...])
    log_a = -LRU_C * r * _softplus(-lam_ref[...])
    a = jnp.exp(log_a)
    u = jnp.sqrt(-_expm1(2.0 * log_a)) * (ig * xc)

    rows = lax.broadcasted_iota(jnp.int32, (tc, LRU_WIDTH), 0)
    d = 1
    while d < tc:
        keep = rows >= d
        a_prev = jnp.where(keep, pltpu.roll(a, d, 0), 1.0)
        u_prev = jnp.where(keep, pltpu.roll(u, d, 0), 0.0)
        u = a * u_prev + u
        a = a * a_prev
        d *= 2
    h = u + a * hc[...]
    hc[...] = h[tc - 1:tc, :]
    gl = 0.5 * gate * (1.0 + jnp.tanh(0.7978845608028654 * (gate + 0.044715 * gate * gate * gate)))
    o_ref[...] = h * gl


def _rglru(proj, cw, cb, ga, gab, gx, gxb, lam, B, S):
    T = B * S
    tc = LRU_CHUNK
    nt = S // tc
    col = (3 * ATT_WIDTH) // (2 * LRU_WIDTH)
    vec = lambda: pl.BlockSpec((1, LRU_WIDTH), lambda b, t: (0, 0))
    mat = lambda: pl.BlockSpec((LRU_WIDTH, LRU_WIDTH), lambda b, t: (0, 0))
    return pl.pallas_call(
        _lru_kernel,
        grid=(B, nt),
        in_specs=[pl.BlockSpec((tc, 2 * LRU_WIDTH), lambda b, t: (b * nt + t, col)),
                  pl.BlockSpec((CONV_WIDTH, LRU_WIDTH), lambda b, t: (0, 0)),
                  vec(), mat(), vec(), mat(), vec(), vec()],
        out_specs=pl.BlockSpec((tc, LRU_WIDTH), lambda b, t: (b * nt + t, 0)),
        out_shape=jax.ShapeDtypeStruct((T, LRU_WIDTH), F32),
        scratch_shapes=[pltpu.VMEM((tc + 8, LRU_WIDTH), F32), pltpu.VMEM((1, LRU_WIDTH), F32)],
        compiler_params=_params("arbitrary", "arbitrary"),
        name="rglru",
    )(proj, cw, cb, ga, gab, gx, gxb, lam)


def _rwkv_prep_kernel(p_ref, pp_ref, mu_ref, wlr_ref, w0_ref, a0_ref, kk_ref, ka_ref, ones_ref,
                      r_ref, ld_ref, k2_ref, v_ref, kn_ref, a_ref, g_ref):
    i = pl.program_id(1)
    W = RWKV_WIDTH
    p = p_ref[...]
    rows = lax.broadcasted_iota(jnp.int32, p.shape, 0)
    prev_last = jnp.where(i == 0, 0.0, pp_ref[7:8, :])
    p_prev = jnp.where(rows == 0, prev_last, pltpu.roll(p, 1, 0))
    pf = p + (p_prev - p) * mu_ref[...]
    r = pf[:, 0:W]
    k = pf[:, W:2 * W]
    v = pf[:, 2 * W:3 * W]
    z = pf[:, 3 * W:4 * W]
    lane = lax.broadcasted_iota(jnp.int32, z.shape, 1)
    zz = jnp.where(lane < 64, jnp.tanh(z), jnp.where(lane < 128, z, _sigmoid(z)))
    lr = jnp.dot(zz, wlr_ref[...], precision=HI, preferred_element_type=F32)
    w = -_softplus(-(w0_ref[...] + lr[:, 0:W])) - 0.5
    a = _sigmoid(a0_ref[...] + lr[:, W:2 * W])
    kk = k * kk_ref[...]
    ssq = jnp.dot(kk * kk, ones_ref[...], precision=HI, preferred_element_type=F32)
    kk = kk / jnp.maximum(jnp.sqrt(ssq), 1e-12)
    r_ref[...] = r
    ld_ref[...] = -jnp.exp(w)
    k2_ref[...] = k * (1.0 + (a - 1.0) * ka_ref[...])
    v_ref[...] = v
    kn_ref[...] = kk
    a_ref[...] = a
    g_ref[...] = lr[:, 2 * W:3 * W]


def _rwkv_prep(proj, mu, wlr, w0, a0, k_k, k_a, head_ones, B, S):
    T = B * S
    tr = RWKV_PREP_ROWS
    nt = S // tr
    W = RWKV_WIDTH
    col = 2
    vec = lambda: pl.BlockSpec((1, W), lambda b, t: (0, 0))
    out = lambda: pl.BlockSpec((tr, W), lambda b, t: (b * nt + t, 0))
    return pl.pallas_call(
        _rwkv_prep_kernel,
        grid=(B, nt),
        in_specs=[pl.BlockSpec((tr, 4 * W), lambda b, t: (b * nt + t, col)),
                  pl.BlockSpec((8, 4 * W), lambda b, t: (jnp.maximum((b * nt + t) * (tr // 8) - 1, 0), col)),
                  pl.BlockSpec((1, 4 * W), lambda b, t: (0, 0)),
                  pl.BlockSpec((W, 3 * W), lambda b, t: (0, 0)),
                  vec(), vec(), vec(), vec(),
                  pl.BlockSpec((W, W), lambda b, t: (0, 0))],
        out_specs=[out() for _ in range(7)],
        out_shape=[jax.ShapeDtypeStruct((T, W), F32) for _ in range(7)],
        compiler_params=_params("arbitrary", "arbitrary"),
        name="rwkv_prep",
    )(proj, proj, mu, wlr, w0, a0, k_k, k_a, head_ones)


def _wkv_kernel(r_ref, ld_ref, k2_ref, v_ref, kn_ref, a_ref, y_ref, st_ref):
    c = pl.program_id(1)
    C = WKV_CHUNK
    W = RWKV_WIDTH
    H = RWKV_HEADS

    @pl.when(c == 0)
    def _():
        st_ref[...] = jnp.zeros_like(st_ref)

    def mm(x, y):
        return jnp.dot(x.astype(BF16), y.astype(BF16), preferred_element_type=F32)

    def mm_nt(x, y):
        return lax.dot_general(x.astype(BF16), y.astype(BF16), NT_DIMS, preferred_element_type=F32)

    def mm_tn(x, y):
        return lax.dot_general(x.astype(BF16), y.astype(BF16), TN_DIMS, preferred_element_type=F32)

    tr = lax.broadcasted_iota(jnp.int32, (C, C), 0)
    tc = lax.broadcasted_iota(jnp.int32, (C, C), 1)
    tri = jnp.where(tr >= tc, 1.0, 0.0)
    row = lax.broadcasted_iota(jnp.int32, (W, W), 0)
    colm = lax.broadcasted_iota(jnp.int32, (W, W), 1)
    same_head = (row // C) == (colm // HEAD_DIM)
    strict = lambda x: jnp.where(same_head, jnp.where(row > colm, x, 0.0), 0.0)
    incl = lambda x: jnp.where(same_head, jnp.where(row >= colm, x, 0.0), 0.0)
    same16 = (row // 16) == (colm // 16)
    eye = jnp.where(row == colm, 1.0, 0.0)

    def stack(x):
        return jnp.where(same_head, jnp.concatenate([x] * H, axis=0), 0.0)

    G = range(WKV_GROUP)
    each = lambda fn, *xs: [fn(*(x[g] for x in xs)) for g in G]

    r = [r_ref[g] for g in G]
    ld = [ld_ref[g] for g in G]
    k2 = [k2_ref[g] for g in G]
    v_s = [stack(v_ref[g]) for g in G]
    kn = [kn_ref[g] for g in G]
    b_s = [kn[g] * a_ref[g] for g in G]
    cl = each(lambda x: jnp.dot(tri, x, precision=HI, preferred_element_type=F32), ld)
    cl_end = [x[C - 1:C, :] for x in cl]
    e_neg = [jnp.exp(-x) for x in cl]
    e_end = each(lambda ce, x: jnp.exp(ce - x), cl_end, cl)
    ar_s = [jnp.concatenate([stack(-kn[g] * jnp.exp(cl[g] - ld[g])), stack(r[g] * jnp.exp(cl[g]))], axis=0) for g in G]
    bk_s = [jnp.concatenate([stack(b_s[g] * e_neg[g]), stack(k2[g] * e_neg[g])], axis=0) for g in G]
    end_s = [jnp.concatenate([stack(b_s[g] * e_end[g]), stack(k2[g] * e_end[g])], axis=0) for g in G]

    prod = each(mm_nt, ar_s, bk_s)
    n_ab = [strict(p[0:W, 0:W]) for p in prod]
    a_ak = [strict(p[0:W, W:2 * W]) for p in prod]
    a_rb = [incl(p[W:2 * W, 0:W]) for p in prod]
    a_rk = [incl(p[W:2 * W, W:2 * W]) for p in prod]

    nd = [jnp.where(same16, n, 0.0) for n in n_ab]
    lo = each(lambda n, d: n - d, n_ab, nd)
    n2 = each(mm, nd, nd)
    p1 = [eye + d for d in nd]
    n4 = each(mm, n2, n2)
    p2 = each(lambda p, n: p + mm(p, n), p1, n2)
    n8 = each(mm, n4, n4)
    p3 = each(lambda p, n: p + mm(p, n), p2, n4)
    dinv = each(lambda p, n: p + mm(p, n), p3, n8)
    x1 = each(mm, dinv, lo)
    x2 = each(mm, x1, x1)
    y1 = [eye + x for x in x1]
    m_inv = each(lambda y, x: y + mm(y, x), y1, x2)
    tinv = each(mm, m_inv, dinv)

    st = [st_ref[g] for g in G]
    ar_st = each(mm_nt, ar_s, st)
    akv = each(mm, a_ak, v_s)
    u_s = each(lambda t, a, b: mm(t, a[0:W] + b), tinv, ar_st, akv)
    rkv = each(mm, a_rk, v_s)
    y_s = each(lambda a, m, u, b: a[W:2 * W] + mm(m, u) + b, ar_st, a_rb, u_s, rkv)
    upd = each(lambda u, v, e: mm_tn(jnp.concatenate([u, v], axis=0), e), u_s, v_s, end_s)
    for g in G:
        y = y_s[g][0:C, :]
        for h in range(1, H):
            y = y + y_s[g][h * C:(h + 1) * C, :]
        y_ref[g] = y
        st_ref[g] = st[g] * jnp.exp(cl_end[g]) + upd[g]


def _wkv(r, ld, k2, v, kn, a, B, S):
    C = WKV_CHUNK
    nc = S // C
    W = RWKV_WIDTH
    G = WKV_GROUP
    blk = lambda: pl.BlockSpec((G, C, W), lambda b, c: (b, c, 0))
    seq = lambda t: t.reshape(B, S, W)
    y = pl.pallas_call(
        _wkv_kernel,
        grid=(B // G, nc),
        in_specs=[blk() for _ in range(6)],
        out_specs=blk(),
        out_shape=jax.ShapeDtypeStruct((B, S, W), F32),
        scratch_shapes=[pltpu.VMEM((G, W, W), F32)],
        compiler_params=_params("arbitrary", "arbitrary"),
        name="wkv7",
    )(seq(r), seq(ld), seq(k2), seq(v), seq(kn), seq(a))
    return y.reshape(B * S, W)


def _wkv_post_kernel(y_ref, r_ref, k2_ref, v_ref, g_ref, rk_ref, lg_ref, lb_ref, ones_ref, o_ref):
    ones = ones_ref[...]

    def head_sum(x):
        hi = x.astype(BF16)
        lo = (x - hi.astype(F32)).astype(BF16)
        return (jnp.dot(hi, ones, preferred_element_type=F32) + jnp.dot(lo, ones, preferred_element_type=F32))

    y = y_ref[...]
    r = r_ref[...]
    k2 = k2_ref[...]
    inv_n = 1.0 / HEAD_DIM
    d = y - head_sum(y) * inv_n
    var = head_sum(d * d) * inv_n
    yn = d * lax.rsqrt(var + RWKV_LN_EPS) * lg_ref[...] + lb_ref[...]
    bonus = head_sum(r * k2 * rk_ref[...]) * v_ref[...]
    o_ref[...] = (yn + bonus) * g_ref[...]


def _wkv_post(y, r, k2, v, g, r_k, lnx_g, lnx_b, head_ones_bf16):
    T, W = y.shape
    tm = 1024
    blk = lambda: pl.BlockSpec((tm, W), lambda m: (m, 0))
    vec = lambda: pl.BlockSpec((1, W), lambda m: (0, 0))
    return pl.pallas_call(
        _wkv_post_kernel,
        grid=(T // tm,),
        in_specs=[blk() for _ in range(5)] + [vec(), vec(), vec(), pl.BlockSpec((W, W), lambda m: (0, 0))],
        out_specs=blk(),
        out_shape=jax.ShapeDtypeStruct((T, W), F32),
        compiler_params=_params("arbitrary"),
        name="wkv_post",
    )(y, r, k2, v, g, r_k, lnx_g, lnx_b, head_ones_bf16)


def _outproj_ln_kernel(att_ref, lru_ref, rwk_ref, x_ref, wa_ref, wl_ref, wr_ref, g_ref, b_ref, o_ref, ob_ref):
    h = jnp.dot(att_ref[...].astype(BF16), wa_ref[...], preferred_element_type=F32)
    h = h + jnp.dot(lru_ref[...].astype(BF16), wl_ref[...], preferred_element_type=F32)
    h = h + jnp.dot(rwk_ref[...].astype(BF16), wr_ref[...], preferred_element_type=F32)
    y = _layer_norm(DEEPNORM_ALPHA * x_ref[...] + h, g_ref[...], b_ref[...])
    o_ref[...] = y
    ob_ref[...] = y.astype(BF16)


def _outproj_ln(att, lru, rwk, x, wa, wl, wr, g, b):
    T, D = x.shape
    tm = 1024
    row = lambda w: pl.BlockSpec((tm, w), lambda m: (m, 0))
    full = lambda a: pl.BlockSpec(a.shape, lambda m: (0, 0))
    return pl.pallas_call(
        _outproj_ln_kernel,
        grid=(T // tm,),
        in_specs=[row(ATT_WIDTH), row(LRU_WIDTH), row(RWKV_WIDTH), row(D), full(wa), full(wl), full(wr), full(g), full(b)],
        out_specs=[row(D), row(D)],
        out_shape=[jax.ShapeDtypeStruct((T, D), F32), jax.ShapeDtypeStruct((T, D), BF16)],
        compiler_params=_params("arbitrary"),
        name="outproj_ln",
    )(att, lru, rwk, x, wa, wl, wr, g, b)


def _ffn_kernel(xb_ref, x_ref, wg_ref, wu_ref, wd_ref, g_ref, b_ref, o_ref, ob_ref, acc_ref, *, nf):
    f = pl.program_id(1)
    xb = xb_ref[...]
    hg = jnp.dot(xb, wg_ref[...], preferred_element_type=F32)
    hu = jnp.dot(xb, wu_ref[...], preferred_element_type=F32)
    hh = (hg * _sigmoid(hg) * hu).astype(BF16)
    contrib = jnp.dot(hh, wd_ref[...], preferred_element_type=F32)

    @pl.when(f == 0)
    def _():
        acc_ref[...] = contrib

    @pl.when(f > 0)
    def _():
        acc_ref[...] = acc_ref[...] + contrib

    @pl.when(f == nf - 1)
    def _():
        y = _layer_norm(DEEPNORM_ALPHA * x_ref[...] + acc_ref[...], g_ref[...], b_ref[...])
        o_ref[...] = y
        ob_ref[...] = y.astype(BF16)


def _ffn_ln(xb, x, wg, wu, wd, g, b):
    T, D = x.shape
    F = wg.shape[1]
    tm = 512
    nf = 2
    tf = F // nf
    return pl.pallas_call(
        functools.partial(_ffn_kernel, nf=nf),
        grid=(T // tm, nf),
        in_specs=[pl.BlockSpec((tm, D), lambda m, f: (m, 0)),
                  pl.BlockSpec((tm, D), lambda m, f: (m, 0)),
                  pl.BlockSpec((D, tf), lambda m, f: (0, f)),
                  pl.BlockSpec((D, tf), lambda m, f: (0, f)),
                  pl.BlockSpec((tf, D), lambda m, f: (f, 0)),
                  pl.BlockSpec((1, D), lambda m, f: (0, 0)),
                  pl.BlockSpec((1, D), lambda m, f: (0, 0))],
        out_specs=[pl.BlockSpec((tm, D), lambda m, f: (m, 0)), pl.BlockSpec((tm, D), lambda m, f: (m, 0))],
        out_shape=[jax.ShapeDtypeStruct((T, D), F32), jax.ShapeDtypeStruct((T, D), BF16)],
        scratch_shapes=[pltpu.VMEM((tm, D), F32)],
        compiler_params=_params("arbitrary", "arbitrary"),
        name="ffn_ln",
    )(xb, x, wg, wu, wd, g, b)


def _router_kernel(x_ref, wrt_ref, tri_ref, g_ref, pos_ref, cnt_ref):
    tt = x_ref.shape[0]
    logits = lax.dot_general(wrt_ref[...], x_ref[...], NT_DIMS, precision=HI, preferred_element_type=F32)
    row = lax.broadcasted_iota(jnp.int32, logits.shape, 0).astype(F32)
    m1 = jnp.max(logits, axis=0, keepdims=True)
    i1 = jnp.min(jnp.where(logits == m1, row, float(N_EXPERTS)), axis=0, keepdims=True)
    rest = jnp.where(row == i1, -jnp.inf, logits)
    m2 = jnp.max(rest, axis=0, keepdims=True)
    i2 = jnp.min(jnp.where(rest == m2, row, float(N_EXPERTS)), axis=0, keepdims=True)
    e = jnp.exp(m2 - m1)
    g1 = 1.0 / (1.0 + e)
    g2 = e / (1.0 + e)
    g_ref[0] = jnp.where(row == i1, g1, jnp.where(row == i2, g2, 0.0))
    ind = jnp.where(row == i1, 1.0, jnp.where(row == i2, 1.0, 0.0))
    csum = jnp.dot(ind.astype(BF16), tri_ref[...], preferred_element_type=F32)
    pos_ref[0] = jnp.where(ind > 0.5, csum - 1.0, -1.0)
    cnt_ref[0] = jnp.broadcast_to(csum[:, tt - 1:tt], (N_EXPERTS, LANES))


def _router(x, wrt, tri):
    T, D = x.shape
    tt = MOE_TILE
    nt = T // tt
    tile = lambda: pl.BlockSpec((1, N_EXPERTS, tt), lambda m: (m, 0, 0))
    return pl.pallas_call(
        _router_kernel,
        grid=(nt,),
        in_specs=[pl.BlockSpec((tt, D), lambda m: (m, 0)), pl.BlockSpec((N_EXPERTS, D), lambda m: (0, 0)),
                  pl.BlockSpec((tt, tt), lambda m: (0, 0))],
        out_specs=[tile(), tile(), pl.BlockSpec((1, N_EXPERTS, LANES), lambda m: (m, 0, 0))],
        out_shape=[jax.ShapeDtypeStruct((nt, N_EXPERTS, tt), F32), jax.ShapeDtypeStruct((nt, N_EXPERTS, tt), F32),
                   jax.ShapeDtypeStruct((nt, N_EXPERTS, LANES), F32)],
        compiler_params=_params("arbitrary"),
        name="moe_router",
    )(x, wrt, tri)


def _moe_kernel(cnt_ref, xb_ref, x_ref, gate_ref, pos_ref, wg_ref, wu_ref, wd_ref, g_ref, b_ref,
                o_ref, xg_ref, yacc_ref, gcol_ref, *, nf):
    t = pl.program_id(0)
    e = pl.program_id(1)
    f = pl.program_id(2)
    tt = xb_ref.shape[0]
    R = MOE_ROWS
    half = R // 2
    n = cnt_ref[t * N_EXPERTS + e]
    rem = n % R
    n_full = n // R + jnp.where(rem > half, 1, 0)
    has_tail = jnp.logical_and(rem > 0, rem <= half)
    tail_start = pl.multiple_of(n_full * R, half)
    pos_row = pos_ref[0, pl.ds(e, 1), :]

    def selection(start, rows):
        slot = lax.broadcasted_iota(jnp.int32, (rows, tt), 0) + start
        return jnp.where(pos_row == slot.astype(F32), 1.0, 0.0)

    def gather(start, rows):
        sel = selection(start, rows)
        dst = pl.ds(start, rows)
        xg_ref[dst, :] = jnp.dot(sel.astype(BF16), xb_ref[...], preferred_element_type=F32).astype(BF16)
        gate_row = gate_ref[0, pl.ds(e, 1), :]
        gcol_ref[dst, :] = jnp.broadcast_to(jnp.sum(sel * gate_row, axis=1, keepdims=True), (rows, LANES))

    def expert(start, rows):
        dst = pl.ds(start, rows)
        xc = xg_ref[dst, :]
        hg = jnp.dot(xc, wg_ref[0], preferred_element_type=F32)
        hu = jnp.dot(xc, wu_ref[0], preferred_element_type=F32)
        contrib = jnp.dot((hg * _sigmoid(hg) * hu).astype(BF16), wd_ref[0], preferred_element_type=F32)

        @pl.when(f == 0)
        def _():
            yacc_ref[dst, :] = contrib

        @pl.when(f > 0)
        def _():
            yacc_ref[dst, :] = yacc_ref[dst, :] + contrib

        @pl.when(f == nf - 1)
        def _():
            gate = jnp.concatenate([gcol_ref[dst, :]] * (o_ref.shape[1] // LANES), axis=1)
            ys = (yacc_ref[dst, :] * gate).astype(BF16)
            o_ref[...] = o_ref[...] + lax.dot_general(selection(start, rows).astype(BF16), ys, TN_DIMS,
                                                      preferred_element_type=F32)

    def over_chunks(fn):
        def body(r, carry):
            fn(pl.multiple_of(r * R, R), R)
            return carry
        lax.fori_loop(0, n_full, body, 0)

        @pl.when(has_tail)
        def _():
            fn(tail_start, half)

    @pl.when(jnp.logical_and(e == 0, f == 0))
    def _():
        o_ref[...] = jnp.zeros_like(o_ref)

    @pl.when(f == 0)
    def _():
        over_chunks(gather)

    over_chunks(expert)

    @pl.when(jnp.logical_and(e == N_EXPERTS - 1, f == nf - 1))
    def _():
        o_ref[...] = _layer_norm(DEEPNORM_ALPHA * x_ref[...] + o_ref[...], g_ref[...], b_ref[...])


def _moe_ln(xb, x, gates_t, pos_t, counts, wg, wu, wd, g, b):
    T, D = xb.shape
    F = wg.shape[2]
    tt = MOE_TILE
    nf = 2
    tf = F // nf
    grid_spec = pltpu.PrefetchScalarGridSpec(
        num_scalar_prefetch=1,
        grid=(T // tt, N_EXPERTS, nf),
        in_specs=[pl.BlockSpec((tt, D), lambda m, e, f, c: (m, 0)),
                  pl.BlockSpec((tt, D), lambda m, e, f, c: (m, 0)),
                  pl.BlockSpec((1, N_EXPERTS, tt), lambda m, e, f, c: (m, 0, 0)),
                  pl.BlockSpec((1, N_EXPERTS, tt), lambda m, e, f, c: (m, 0, 0)),
                  pl.BlockSpec((1, D, tf), lambda m, e, f, c: (e, 0, f)),
                  pl.BlockSpec((1, D, tf), lambda m, e, f, c: (e, 0, f)),
                  pl.BlockSpec((1, tf, D), lambda m, e, f, c: (e, f, 0)),
                  pl.BlockSpec((1, D), lambda m, e, f, c: (0, 0)),
                  pl.BlockSpec((1, D), lambda m, e, f, c: (0, 0))],
        out_specs=pl.BlockSpec((tt, D), lambda m, e, f, c: (m, 0)),
        scratch_shapes=[pltpu.VMEM((MOE_ROWS_MAX, D), BF16), pltpu.VMEM((MOE_ROWS_MAX, D), F32),
                        pltpu.VMEM((MOE_ROWS_MAX, LANES), F32)],
    )
    return pl.pallas_call(
        functools.partial(_moe_kernel, nf=nf),
        grid_spec=grid_spec,
        out_shape=jax.ShapeDtypeStruct((T, D), F32),
        compiler_params=_params("arbitrary", "arbitrary", "arbitrary"),
        name="moe_experts",
    )(counts, xb, x, gates_t, pos_t, wg, wu, wd, g, b)


def _block_diag(w):
    n, d, _ = w.shape
    out = jnp.zeros((n * d, n * d), w.dtype)
    for i in range(n):
        out = out.at[i * d:(i + 1) * d, i * d:(i + 1) * d].set(w[i])
    return out


def _rope_tables(positions):
    B, S = positions.shape
    half = ROPE_DIMS // 2
    inv_freq = ROPE_THETA ** (-jnp.arange(0, ROPE_DIMS, 2, dtype=F32) / ROPE_DIMS)
    d = jnp.arange(LANES) % HEAD_DIM
    ang = positions.astype(F32).reshape(B * S, 1) * inv_freq[d % half][None, :]
    cos = jnp.cos(ang)
    sin = jnp.sin(ang)
    c = jnp.where(d < ROPE_DIMS, cos, 1.0)
    s1 = jnp.where(d < half, -sin, 0.0)
    s2 = jnp.where((d >= half) & (d < ROPE_DIMS), sin, 0.0)
    return c, s1, s2


def _mixer_layer(x, xb, rope, B, S, w_in, conv_w, conv_b, ga_w, ga_b, gx_w, gx_b, lam,
                 mu, w0, w_up, a0, a_up, g_up, k_k, k_a, r_k, lnx_g, lnx_b, w_out, ln_g, ln_b):
    W = RWKV_WIDTH
    row = lambda t: t.reshape(1, -1)
    proj = _in_proj(xb, w_in.astype(BF16))
    q, k, vt, pen = _attn_prep(proj, *rope, B, S)
    att = _attention(q, k, vt, pen, B, S)
    lru = _rglru(proj, conv_w, row(conv_b), _block_diag(ga_w), row(ga_b), _block_diag(gx_w), row(gx_b), row(lam), B, S)
    head_ones = _block_diag(jnp.ones((RWKV_HEADS, HEAD_DIM, HEAD_DIM), F32))
    wlr = jnp.zeros((W, 3 * W), F32)
    wlr = wlr.at[0:64, 0:W].set(w_up).at[64:128, W:2 * W].set(a_up).at[128:256, 2 * W:3 * W].set(g_up)
    r, ld, k2, v, kn, a, g = _rwkv_prep(proj, row(mu), wlr, row(w0), row(a0), row(k_k), row(k_a), head_ones, B, S)
    y = _wkv(r, ld, k2, v, kn, a, B, S)
    rwk = _wkv_post(y, r, k2, v, g, row(r_k), row(lnx_g), row(lnx_b), head_ones.astype(BF16))
    wo = w_out.astype(BF16)
    return _outproj_ln(att, lru, rwk, x, wo[0:ATT_WIDTH], wo[ATT_WIDTH:ATT_WIDTH + LRU_WIDTH],
                       wo[ATT_WIDTH + LRU_WIDTH:], row(ln_g), row(ln_b))


def kernel(x, positions, w_in, lru_conv_w, lru_conv_b, lru_ga_w, lru_ga_b, lru_gx_w, lru_gx_b, lru_lambda, rwkv_mu, rwkv_w0, rwkv_w_up, rwkv_a0, rwkv_a_up, rwkv_g_up, rwkv_k_k, rwkv_k_a, rwkv_r_k, rwkv_lnx_g, rwkv_lnx_b, w_out, ln1_g, ln1_b, ffn_w_gate, ffn_w_up, ffn_w_down, moe_router, moe_w_gate, moe_w_up, moe_w_down, ln2_g, ln2_b):
    B, S, D = x.shape
    T = B * S
    rope = _rope_tables(positions)
    xf = x.reshape(T, D)
    xb = xf.astype(BF16)
    for l in range(DEPTH):
        xf, xb = _mixer_layer(xf, xb, rope, B, S, w_in[l], lru_conv_w[l], lru_conv_b[l], lru_ga_w[l], lru_ga_b[l],
                              lru_gx_w[l], lru_gx_b[l], lru_lambda[l], rwkv_mu[l], rwkv_w0[l], rwkv_w_up[l],
                              rwkv_a0[l], rwkv_a_up[l], rwkv_g_up[l], rwkv_k_k[l], rwkv_k_a[l], rwkv_r_k[l],
                              rwkv_lnx_g[l], rwkv_lnx_b[l], w_out[l], ln1_g[l], ln1_b[l])
        g2 = ln2_g[l].reshape(1, D)
        b2 = ln2_b[l].reshape(1, D)
        if l % 2 == 0:
            i = l // 2
            xf, xb = _ffn_ln(xb, xf, ffn_w_gate[i].astype(BF16), ffn_w_up[i].astype(BF16),
                             ffn_w_down[i].astype(BF16), g2, b2)
        else:
            i = l // 2
            idx = jnp.arange(MOE_TILE)
            tri = (idx[:, None] <= idx[None, :]).astype(BF16)
            gates_t, pos_t, cnt = _router(xf, moe_router[i].T, tri)
            counts = cnt[:, :, 0].astype(jnp.int32).reshape(-1)
            xf = _moe_ln(xb, xf, gates_t, pos_t, counts, moe_w_gate[i].astype(BF16), moe_w_up[i].astype(BF16),
                         moe_w_down[i].astype(BF16), g2, b2)
    return xf.reshape(B, S, D)
```

```python
import functools

import jax
import jax.numpy as jnp
from jax import lax
from jax.experimental import pallas as pl
from jax.experimental.pallas import tpu as pltpu

F32 = jnp.float32
BF16 = jnp.bfloat16
HI = lax.Precision.HIGHEST
NT_DIMS = (((1,), (1,)), ((), ()))
TN_DIMS = (((0,), (0,)), ((), ()))

HEAD_DIM = 64
ATT_HEADS = 8
ATT_WIDTH = ATT_HEADS * HEAD_DIM
LRU_WIDTH = 256
RWKV_HEADS = 4
RWKV_WIDTH = 256
ROPE_DIMS = 16
ROPE_THETA = 500000.0
MOBA_BLOCK = 256
MOBA_TOPK = 3
CONV_WIDTH = 4
LRU_C = 8.0
RWKV_LN_EPS = 64e-5
N_EXPERTS = 8
LN_EPS = 1e-5
DEPTH = 2
DEEPNORM_ALPHA = (2 * DEPTH) ** 0.25
NEG_INF = -1e30
LOG2_E = 1.4426950408889634

LANES = 128
VMEM_LIMIT = 56 * 1024 * 1024

ATTN_STREAMS = 4
WKV_CHUNK = 64
WKV_GROUP = 8
LRU_CHUNK = 512
RWKV_PREP_ROWS = 1024
MOE_TILE = 1024
MOE_ROWS = 288
MOE_ROWS_MAX = (MOE_TILE // MOE_ROWS + 1) * MOE_ROWS


def _params(*sem):
    return pltpu.CompilerParams(dimension_semantics=sem, vmem_limit_bytes=VMEM_LIMIT)


def _sigmoid(x):
    return 1.0 / (1.0 + jnp.exp(-x))


def _softplus(x):
    return jnp.maximum(x, 0.0) + jnp.log1p(jnp.exp(-jnp.abs(x)))


def _expm1(z):
    u = jnp.exp(z)
    um1 = u - 1.0
    return jnp.where(u == 1.0, z, jnp.where(um1 == -1.0, -1.0, um1 * z / jnp.log(u)))


def _layer_norm(y, g, b):
    m = jnp.mean(y, axis=-1, keepdims=True)
    d = y - m
    var = jnp.mean(d * d, axis=-1, keepdims=True)
    return d * lax.rsqrt(var + LN_EPS) * g + b


def _matmul_kernel(x_ref, w_ref, o_ref):
    o_ref[...] = jnp.dot(x_ref[...], w_ref[...], preferred_element_type=F32)


def _in_proj(xb, w):
    T, D = xb.shape
    N = w.shape[1]
    tm, tn = 1024, N
    return pl.pallas_call(
        _matmul_kernel,
        grid=(N // tn, T // tm),
        in_specs=[pl.BlockSpec((tm, D), lambda n, m: (m, 0)),
                  pl.BlockSpec((D, tn), lambda n, m: (0, n))],
        out_specs=pl.BlockSpec((tm, tn), lambda n, m: (m, n)),
        out_shape=jax.ShapeDtypeStruct((T, N), F32),
        compiler_params=_params("arbitrary", "arbitrary"),
        name="in_proj",
    )(xb, w)


def _attn_prep_kernel(proj_ref, c_ref, s1_ref, s2_ref, q_ref, k_ref, vt_ref, pen_ref, km_ref, *, nb):
    i = pl.program_id(1)

    @pl.when(i == 0)
    def _():
        km_ref[...] = jnp.zeros_like(km_ref)

    c = c_ref[...]
    s1 = s1_ref[...]
    s2 = s2_ref[...]

    def rope(xt):
        return xt * c + pltpu.roll(xt, LANES - ROPE_DIMS // 2, 1) * s1 + pltpu.roll(xt, ROPE_DIMS // 2, 1) * s2

    q_tiles = []
    for ct in range(ATT_WIDTH // LANES):
        lo, hi = ct * LANES, (ct + 1) * LANES
        qr = rope(proj_ref[:, lo:hi])
        q_tiles.append(qr)
        q_ref[:, lo:hi] = (qr * (HEAD_DIM ** -0.5 * LOG2_E)).astype(BF16)
        kr = rope(proj_ref[:, ATT_WIDTH + lo:ATT_WIDTH + hi])
        k_ref[0, 0, :, lo:hi] = kr.astype(BF16)
        km_row = lax.broadcasted_iota(jnp.int32, (nb, LANES), 0)
        km_ref[:, lo:hi] = jnp.where(km_row == i, jnp.mean(kr, axis=0, keepdims=True), km_ref[:, lo:hi])
    vt_ref[0, 0] = proj_ref[:, 2 * ATT_WIDTH:3 * ATT_WIDTH].T.astype(BF16)

    q_rot = jnp.concatenate(q_tiles, axis=1)
    km = km_ref[...]
    km_rows = jnp.concatenate([km] * ATT_HEADS, axis=0)
    row_head = lax.broadcasted_iota(jnp.int32, km_rows.shape, 0) // nb
    lane_head = lax.broadcasted_iota(jnp.int32, km_rows.shape, 1) // HEAD_DIM
    g_all = lax.dot_general(jnp.where(row_head == lane_head, km_rows, 0.0), q_rot, NT_DIMS,
                            precision=HI, preferred_element_type=F32)
    n_iota = lax.broadcasted_iota(jnp.int32, (nb, MOBA_BLOCK), 0)
    past = n_iota < i
    for h in range(ATT_HEADS):
        g = g_all[h * nb:(h + 1) * nb, :]
        rank = jnp.zeros((nb, MOBA_BLOCK), F32)
        for m in range(nb):
            gm = g[m:m + 1, :]
            beats = jnp.where(gm > g, 1.0, jnp.where(gm == g, jnp.where(n_iota > m, 1.0, 0.0), 0.0))
            rank = rank + jnp.where(m < i, beats, 0.0)
        pen = jnp.where(past, jnp.where(rank < float(MOBA_TOPK), 0.0, NEG_INF), NEG_INF)
        half = (h % 2) * MOBA_BLOCK
        pen_ref[0, h // 2, :, half:half + MOBA_BLOCK] = pen


def _attn_prep(proj, rope_c, rope_s1, rope_s2, B, S):
    T = B * S
    nb = S // MOBA_BLOCK
    blk = MOBA_BLOCK
    return pl.pallas_call(
        functools.partial(_attn_prep_kernel, nb=nb),
        grid=(B, nb),
        in_specs=[pl.BlockSpec((blk, 3 * ATT_WIDTH), lambda b, i: (b * nb + i, 0)),
                  pl.BlockSpec((blk, LANES), lambda b, i: (b * nb + i, 0)),
                  pl.BlockSpec((blk, LANES), lambda b, i: (b * nb + i, 0)),
                  pl.BlockSpec((blk, LANES), lambda b, i: (b * nb + i, 0))],
        out_specs=[pl.BlockSpec((blk, ATT_WIDTH), lambda b, i: (b * nb + i, 0)),
                   pl.BlockSpec((1, 1, blk, ATT_WIDTH), lambda b, i: (b, i, 0, 0)),
                   pl.BlockSpec((1, 1, ATT_WIDTH, blk), lambda b, i: (b, i, 0, 0)),
                   pl.BlockSpec((1, ATT_HEADS // 2, nb, 2 * blk), lambda b, i: (b, 0, 0, i))],
        out_shape=[jax.ShapeDtypeStruct((T, ATT_WIDTH), BF16),
                   jax.ShapeDtypeStruct((B, nb, blk, ATT_WIDTH), BF16),
                   jax.ShapeDtypeStruct((B, nb, ATT_WIDTH, blk), BF16),
                   jax.ShapeDtypeStruct((B, ATT_HEADS // 2, nb, 2 * S), F32)],
        scratch_shapes=[pltpu.VMEM((nb, ATT_WIDTH), F32)],
        compiler_params=_params("arbitrary", "arbitrary"),
        name="attn_prep",
    )(proj, rope_c, rope_s1, rope_s2)


def _attn_kernel(q_ref, k_ref, vt_ref, pen_ref, o_ref):
    i = pl.program_id(2)
    blk = MOBA_BLOCK
    streams = range(ATTN_STREAMS)
    each = lambda fn, *xs: [fn(*(x[t] for x in xs)) for t in streams]
    lanes = lambda t: slice(t * LANES, (t + 1) * LANES)
    lane_head = lax.broadcasted_iota(jnp.int32, (blk, LANES), 1) // HEAD_DIM
    kidx = lax.broadcasted_iota(jnp.int32, (blk, 2 * blk), 0)
    qidx = lax.broadcasted_iota(jnp.int32, (blk, 2 * blk), 1) % blk
    nt = lambda a, b: lax.dot_general(a, b, NT_DIMS, preferred_element_type=F32)
    nn = lambda a, b: jnp.dot(a, b, preferred_element_type=F32)
    col_max = lambda x: jnp.max(x, axis=0, keepdims=True)
    ones_tile = jnp.ones((16, blk), BF16)

    def stacked_q(t):
        q = q_ref[:, lanes(t)]
        zero = jnp.zeros_like(q)
        return jnp.concatenate([jnp.where(lane_head == 0, q, zero), jnp.where(lane_head == 1, q, zero)], axis=0)

    qs = [stacked_q(t) for t in streams]
    s = [jnp.where(kidx <= qidx, nt(k_ref[0, i, :, lanes(t)], qs[t]), NEG_INF) for t in streams]
    m0 = each(col_max, s)
    p = each(lambda x, m: jnp.exp2(x - m), s, m0)
    pv = [nn(jnp.concatenate([vt_ref[0, i, lanes(t), :], ones_tile], axis=0), p[t].astype(BF16)) for t in streams]
    l0 = [x[LANES:LANES + 1] for x in pv]
    acc0 = [x[0:LANES] for x in pv]

    def body(jj, carry):
        m, l, acc = carry
        j0 = 2 * jj
        j1 = j0 + 1
        s0 = [nt(k_ref[0, j0, :, lanes(t)], qs[t]) + pen_ref[0, t, pl.ds(j0, 1), :] for t in streams]
        s1 = [nt(k_ref[0, j1, :, lanes(t)], qs[t]) + pen_ref[0, t, pl.ds(j1, 1), :] for t in streams]
        m_new = each(lambda mo, a, b: jnp.maximum(mo, jnp.maximum(col_max(a), col_max(b))), m, s0, s1)
        alpha = each(lambda mo, mn: jnp.exp2(mo - mn), m, m_new)
        p0 = each(lambda a, mn: jnp.exp2(a - mn), s0, m_new)
        p1 = each(lambda a, mn: jnp.exp2(a - mn), s1, m_new)
        pb0 = [x.astype(BF16) for x in p0]
        pb1 = [x.astype(BF16) for x in p1]
        pv0 = [nn(jnp.concatenate([vt_ref[0, j0, lanes(t), :], ones_tile], axis=0), pb0[t]) for t in streams]
        pv1 = [nn(jnp.concatenate([vt_ref[0, j1, lanes(t), :], ones_tile], axis=0), pb1[t]) for t in streams]
        l = each(lambda al, lo, a, b: al * lo + a[LANES:LANES + 1] + b[LANES:LANES + 1], alpha, l, pv0, pv1)
        acc = each(lambda ac, al, a, b: ac * al + a[0:LANES] + b[0:LANES], acc, alpha, pv0, pv1)
        return tuple(m_new), tuple(l), tuple(acc)

    _, l, acc = lax.fori_loop(0, (i + 1) // 2, body, (tuple(m0), tuple(l0), tuple(acc0)))
    row = lax.broadcasted_iota(jnp.int32, (LANES, blk), 0)
    for t in streams:
        out = acc[t] / l[t]
        o_ref[:, lanes(t)] = jnp.where(row < HEAD_DIM, out[:, 0:blk], out[:, blk:2 * blk]).T


def _attention(q, k, vt, pen, B, S):
    T = B * S
    nb = S // MOBA_BLOCK
    blk = MOBA_BLOCK
    w = ATTN_STREAMS * LANES
    n_groups = ATT_WIDTH // w
    return pl.pallas_call(
        _attn_kernel,
        grid=(B, n_groups, nb),
        in_specs=[pl.BlockSpec((blk, w), lambda b, hp, i: (b * nb + i, hp)),
                  pl.BlockSpec((1, nb, blk, w), lambda b, hp, i: (b, 0, 0, hp)),
                  pl.BlockSpec((1, nb, w, blk), lambda b, hp, i: (b, 0, hp, 0)),
                  pl.BlockSpec((1, ATTN_STREAMS, nb, 2 * blk), lambda b, hp, i: (b, hp, 0, i))],
        out_specs=pl.BlockSpec((blk, w), lambda b, hp, i: (b * nb + i, hp)),
        out_shape=jax.ShapeDtypeStruct((T, ATT_WIDTH), F32),
        compiler_params=_params("arbitrary", "arbitrary", "arbitrary"),
        name="moba_attn",
    )(q, k, vt, pen)


def _lru_kernel(p_ref, cw_ref, cb_ref, ga_ref, gab_ref, gx_ref, gxb_ref, lam_ref, o_ref, xbuf, hc):
    t = pl.program_id(1)
    tc = LRU_CHUNK
    pad = 8

    @pl.when(t == 0)
    def _():
        xbuf[0:pad, :] = jnp.zeros((pad, LRU_WIDTH), F32)
        hc[...] = jnp.zeros_like(hc)

    @pl.when(t > 0)
    def _():
        xbuf[0:pad, :] = xbuf[tc:tc + pad, :]

    x = p_ref[:, 0:LRU_WIDTH]
    gate = p_ref[:, LRU_WIDTH:2 * LRU_WIDTH]
    xbuf[pad:pad + tc, :] = x
    xc = cb_ref[...] + x * cw_ref[CONV_WIDTH - 1:CONV_WIDTH, :]
    for j in range(CONV_WIDTH - 1):
        back = CONV_WIDTH - 1 - j
        xc = xc + xbuf[pad - back:pad - back + tc, :] * cw_ref[j:j + 1, :]

    r = _sigmoid(jnp.dot(xc, ga_ref[...], precision=HI, preferred_element_type=F32) + gab_ref[...])
    ig = _sigmoid(jnp.dot(xc, gx_ref[...], precision=HI, preferred_element_type=F32) + gxb_ref[...])
    log_a = -LRU_C * r * _softplus(-lam_ref[...])
    a = jnp.exp(log_a)
    u = jnp.sqrt(-_expm1(2.0 * log_a)) * (ig * xc)

    rows = lax.broadcasted_iota(jnp.int32, (tc, LRU_WIDTH), 0)
    d = 1
    while d < tc:
        keep = rows >= d
        a_prev = jnp.where(keep, pltpu.roll(a, d, 0), 1.0)
        u_prev = jnp.where(keep, pltpu.roll(u, d, 0), 0.0)
        u = a * u_prev + u
        a = a * a_prev
        d *= 2
    h = u + a * hc[...]
    hc[...] = h[tc - 1:tc, :]
    gl = 0.5 * gate * (1.0 + jnp.tanh(0.7978845608028654 * (gate + 0.044715 * gate * gate * gate)))
    o_ref[...] = h * gl


def _rglru(proj, cw, cb, ga, gab, gx, gxb, lam, B, S):
    T = B * S
    tc = LRU_CHUNK
    nt = S // tc
    col = (3 * ATT_WIDTH) // (2 * LRU_WIDTH)
    vec = lambda: pl.BlockSpec((1, LRU_WIDTH), lambda b, t: (0, 0))
    mat = lambda: pl.BlockSpec((LRU_WIDTH, LRU_WIDTH), lambda b, t: (0, 0))
    return pl.pallas_call(
        _lru_kernel,
        grid=(B, nt),
        in_specs=[pl.BlockSpec((tc, 2 * LRU_WIDTH), lambda b, t: (b * nt + t, col)),
                  pl.BlockSpec((CONV_WIDTH, LRU_WIDTH), lambda b, t: (0, 0)),
                  vec(), mat(), vec(), mat(), vec(), vec()],
        out_specs=pl.BlockSpec((tc, LRU_WIDTH), lambda b, t: (b * nt + t, 0)),
        out_shape=jax.ShapeDtypeStruct((T, LRU_WIDTH), F32),
        scratch_shapes=[pltpu.VMEM((tc + 8, LRU_WIDTH), F32), pltpu.VMEM((1, LRU_WIDTH), F32)],
        compiler_params=_params("arbitrary", "arbitrary"),
        name="rglru",
    )(proj, cw, cb, ga, gab, gx, gxb, lam)


def _rwkv_prep_kernel(p_ref, pp_ref, mu_ref, wlr_ref, w0_ref, a0_ref, kk_ref, ka_ref, ones_ref,
                      r_ref, ld_ref, k2_ref, v_ref, kn_ref, a_ref, g_ref):
    i = pl.program_id(1)
    W = RWKV_WIDTH
    p = p_ref[...]
    rows = lax.broadcasted_iota(jnp.int32, p.shape, 0)
    prev_last = jnp.where(i == 0, 0.0, pp_ref[7:8, :])
    p_prev = jnp.where(rows == 0, prev_last, pltpu.roll(p, 1, 0))
    pf = p + (p_prev - p) * mu_ref[...]
    r = pf[:, 0:W]
    k = pf[:, W:2 * W]
    v = pf[:, 2 * W:3 * W]
    z = pf[:, 3 * W:4 * W]
    lane = lax.broadcasted_iota(jnp.int32, z.shape, 1)
    zz = jnp.where(lane < 64, jnp.tanh(z), jnp.where(lane < 128, z, _sigmoid(z)))
    lr = jnp.dot(zz, wlr_ref[...], precision=HI, preferred_element_type=F32)
    w = -_softplus(-(w0_ref[...] + lr[:, 0:W])) - 0.5
    a = _sigmoid(a0_ref[...] + lr[:, W:2 * W])
    kk = k * kk_ref[...]
    ssq = jnp.dot(kk * kk, ones_ref[...], precision=HI, preferred_element_type=F32)
    kk = kk / jnp.maximum(jnp.sqrt(ssq), 1e-12)
    r_ref[...] = r
    ld_ref[...] = -jnp.exp(w)
    k2_ref[...] = k * (1.0 + (a - 1.0) * ka_ref[...])
    v_ref[...] = v
    kn_ref[...] = kk
    a_ref[...] = a
    g_ref[...] = lr[:, 2 * W:3 * W]


def _rwkv_prep(proj, mu, wlr, w0, a0, k_k, k_a, head_ones, B, S):
    T = B * S
    tr = RWKV_PREP_ROWS
    nt = S // tr
    W = RWKV_WIDTH
    col = 2
    vec = lambda: pl.BlockSpec((1, W), lambda b, t: (0, 0))
    out = lambda: pl.BlockSpec((tr, W), lambda b, t: (b * nt + t, 0))
    return pl.pallas_call(
        _rwkv_prep_kernel,
        grid=(B, nt),
        in_specs=[pl.BlockSpec((tr, 4 * W), lambda b, t: (b * nt + t, col)),
                  pl.BlockSpec((8, 4 * W), lambda b, t: (jnp.maximum((b * nt + t) * (tr // 8) - 1, 0), col)),
                  pl.BlockSpec((1, 4 * W), lambda b, t: (0, 0)),
                  pl.BlockSpec((W, 3 * W), lambda b, t: (0, 0)),
                  vec(), vec(), vec(), vec(),
                  pl.BlockSpec((W, W), lambda b, t: (0, 0))],
        out_specs=[out() for _ in range(7)],
        out_shape=[jax.ShapeDtypeStruct((T, W), F32) for _ in range(7)],
        compiler_params=_params("arbitrary", "arbitrary"),
        name="rwkv_prep",
    )(proj, proj, mu, wlr, w0, a0, k_k, k_a, head_ones)


def _wkv_kernel(r_ref, ld_ref, k2_ref, v_ref, kn_ref, a_ref, y_ref, st_ref):
    c = pl.program_id(1)
    C = WKV_CHUNK
    W = RWKV_WIDTH
    H = RWKV_HEADS

    @pl.when(c == 0)
    def _():
        st_ref[...] = jnp.zeros_like(st_ref)

    def mm(x, y):
        return jnp.dot(x.astype(BF16), y.astype(BF16), preferred_element_type=F32)

    def mm_nt(x, y):
        return lax.dot_general(x.astype(BF16), y.astype(BF16), NT_DIMS, preferred_element_type=F32)

    def mm_tn(x, y):
        return lax.dot_general(x.astype(BF16), y.astype(BF16), TN_DIMS, preferred_element_type=F32)

    tr = lax.broadcasted_iota(jnp.int32, (C, C), 0)
    tc = lax.broadcasted_iota(jnp.int32, (C, C), 1)
    tri = jnp.where(tr >= tc, 1.0, 0.0)
    row = lax.broadcasted_iota(jnp.int32, (W, W), 0)
    colm = lax.broadcasted_iota(jnp.int32, (W, W), 1)
    same_head = (row // C) == (colm // HEAD_DIM)
    strict = lambda x: jnp.where(same_head, jnp.where(row > colm, x, 0.0), 0.0)
    incl = lambda x: jnp.where(same_head, jnp.where(row >= colm, x, 0.0), 0.0)
    same16 = (row // 16) == (colm // 16)
    eye = jnp.where(row == colm, 1.0, 0.0)

    def stack(x):
        return jnp.where(same_head, jnp.concatenate([x] * H, axis=0), 0.0)

    G = range(WKV_GROUP)
    each = lambda fn, *xs: [fn(*(x[g] for x in xs)) for g in G]

    r = [r_ref[g] for g in G]
    ld = [ld_ref[g] for g in G]
    k2 = [k2_ref[g] for g in G]
    v_s = [stack(v_ref[g]) for g in G]
    kn = [kn_ref[g] for g in G]
    b_s = [kn[g] * a_ref[g] for g in G]
    cl = each(lambda x: jnp.dot(tri, x, precision=HI, preferred_element_type=F32), ld)
    cl_end = [x[C - 1:C, :] for x in cl]
    e_neg = [jnp.exp(-x) for x in cl]
    e_end = each(lambda ce, x: jnp.exp(ce - x), cl_end, cl)
    ar_s = [jnp.concatenate([stack(-kn[g] * jnp.exp(cl[g] - ld[g])), stack(r[g] * jnp.exp(cl[g]))], axis=0) for g in G]
    bk_s = [jnp.concatenate([stack(b_s[g] * e_neg[g]), stack(k2[g] * e_neg[g])], axis=0) for g in G]
    end_s = [jnp.concatenate([stack(b_s[g] * e_end[g]), stack(k2[g] * e_end[g])], axis=0) for g in G]

    prod = each(mm_nt, ar_s, bk_s)
    n_ab = [strict(p[0:W, 0:W]) for p in prod]
    a_ak = [strict(p[0:W, W:2 * W]) for p in prod]
    a_rb = [incl(p[W:2 * W, 0:W]) for p in prod]
    a_rk = [incl(p[W:2 * W, W:2 * W]) for p in prod]

    nd = [jnp.where(same16, n, 0.0) for n in n_ab]
    lo = each(lambda n, d: n - d, n_ab, nd)
    n2 = each(mm, nd, nd)
    p1 = [eye + d for d in nd]
    n4 = each(mm, n2, n2)
    p2 = each(lambda p, n: p + mm(p, n), p1, n2)
    n8 = each(mm, n4, n4)
    p3 = each(lambda p, n: p + mm(p, n), p2, n4)
    dinv = each(lambda p, n: p + mm(p, n), p3, n8)
    x1 = each(mm, dinv, lo)
    x2 = each(mm, x1, x1)
    y1 = [eye + x for x in x1]
    m_inv = each(lambda y, x: y + mm(y, x), y1, x2)
    tinv = each(mm, m_inv, dinv)

    st = [st_ref[g] for g in G]
    ar_st = each(mm_nt, ar_s, st)
    akv = each(mm, a_ak, v_s)
    u_s = each(lambda t, a, b: mm(t, a[0:W] + b), tinv, ar_st, akv)
    rkv = each(mm, a_rk, v_s)
    y_s = each(lambda a, m, u, b: a[W:2 * W] + mm(m, u) + b, ar_st, a_rb, u_s, rkv)
    upd = each(lambda u, v, e: mm_tn(jnp.concatenate([u, v], axis=0), e), u_s, v_s, end_s)
    for g in G:
        y = y_s[g][0:C, :]
        for h in range(1, H):
            y = y + y_s[g][h * C:(h + 1) * C, :]
        y_ref[g] = y
        st_ref[g] = st[g] * jnp.exp(cl_end[g]) + upd[g]


def _wkv(r, ld, k2, v, kn, a, B, S):
    C = WKV_CHUNK
    nc = S // C
    W = RWKV_WIDTH
    G = WKV_GROUP
    blk = lambda: pl.BlockSpec((G, C, W), lambda b, c: (b, c, 0))
    seq = lambda t: t.reshape(B, S, W)
    y = pl.pallas_call(
        _wkv_kernel,
        grid=(B // G, nc),
        in_specs=[blk() for _ in range(6)],
        out_specs=blk(),
        out_shape=jax.ShapeDtypeStruct((B, S, W), F32),
        scratch_shapes=[pltpu.VMEM((G, W, W), F32)],
        compiler_params=_params("arbitrary", "arbitrary"),
        name="wkv7",
    )(seq(r), seq(ld), seq(k2), seq(v), seq(kn), seq(a))
    return y.reshape(B * S, W)


def _wkv_post_kernel(y_ref, r_ref, k2_ref, v_ref, g_ref, rk_ref, lg_ref, lb_ref, ones_ref, o_ref):
    ones = ones_ref[...]

    def head_sum(x):
        hi = x.astype(BF16)
        lo = (x - hi.astype(F32)).astype(BF16)
        return (jnp.dot(hi, ones, preferred_element_type=F32) + jnp.dot(lo, ones, preferred_element_type=F32))

    y = y_ref[...]
    r = r_ref[...]
    k2 = k2_ref[...]
    inv_n = 1.0 / HEAD_DIM
    d = y - head_sum(y) * inv_n
    var = head_sum(d * d) * inv_n
    yn = d * lax.rsqrt(var + RWKV_LN_EPS) * lg_ref[...] + lb_ref[...]
    bonus = head_sum(r * k2 * rk_ref[...]) * v_ref[...]
    o_ref[...] = (yn + bonus) * g_ref[...]


def _wkv_post(y, r, k2, v, g, r_k, lnx_g, lnx_b, head_ones_bf16):
    T, W = y.shape
    tm = 1024
    blk = lambda: pl.BlockSpec((tm, W), lambda m: (m, 0))
    vec = lambda: pl.BlockSpec((1, W), lambda m: (0, 0))
    return pl.pallas_call(
        _wkv_post_kernel,
        grid=(T // tm,),
        in_specs=[blk() for _ in range(5)] + [vec(), vec(), vec(), pl.BlockSpec((W, W), lambda m: (0, 0))],
        out_specs=blk(),
        out_shape=jax.ShapeDtypeStruct((T, W), F32),
        compiler_params=_params("arbitrary"),
        name="wkv_post",
    )(y, r, k2, v, g, r_k, lnx_g, lnx_b, head_ones_bf16)


def _outproj_ln_kernel(att_ref, lru_ref, rwk_ref, x_ref, wa_ref, wl_ref, wr_ref, g_ref, b_ref, o_ref, ob_ref):
    h = jnp.dot(att_ref[...].astype(BF16), wa_ref[...], preferred_element_type=F32)
    h = h + jnp.dot(lru_ref[...].astype(BF16), wl_ref[...], preferred_element_type=F32)
    h = h + jnp.dot(rwk_ref[...].astype(BF16), wr_ref[...], preferred_element_type=F32)
    y = _layer_norm(DEEPNORM_ALPHA * x_ref[...] + h, g_ref[...], b_ref[...])
    o_ref[...] = y
    ob_ref[...] = y.astype(BF16)


def _outproj_ln(att, lru, rwk, x, wa, wl, wr, g, b):
    T, D = x.shape
    tm = 1024
    row = lambda w: pl.BlockSpec((tm, w), lambda m: (m, 0))
    full = lambda a: pl.BlockSpec(a.shape, lambda m: (0, 0))
    return pl.pallas_call(
        _outproj_ln_kernel,
        grid=(T // tm,),
        in_specs=[row(ATT_WIDTH), row(LRU_WIDTH), row(RWKV_WIDTH), row(D), full(wa), full(wl), full(wr), full(g), full(b)],
        out_specs=[row(D), row(D)],
        out_shape=[jax.ShapeDtypeStruct((T, D), F32), jax.ShapeDtypeStruct((T, D), BF16)],
        compiler_params=_params("arbitrary"),
        name="outproj_ln",
    )(att, lru, rwk, x, wa, wl, wr, g, b)


def _ffn_kernel(xb_ref, x_ref, wg_ref, wu_ref, wd_ref, g_ref, b_ref, o_ref, ob_ref, acc_ref, *, nf):
    f = pl.program_id(1)
    xb = xb_ref[...]
    hg = jnp.dot(xb, wg_ref[...], preferred_element_type=F32)
    hu = jnp.dot(xb, wu_ref[...], preferred_element_type=F32)
    hh = (hg * _sigmoid(hg) * hu).astype(BF16)
    contrib = jnp.dot(hh, wd_ref[...], preferred_element_type=F32)

    @pl.when(f == 0)
    def _():
        acc_ref[...] = contrib

    @pl.when(f > 0)
    def _():
        acc_ref[...] = acc_ref[...] + contrib

    @pl.when(f == nf - 1)
    def _():
        y = _layer_norm(DEEPNORM_ALPHA * x_ref[...] + acc_ref[...], g_ref[...], b_ref[...])
        o_ref[...] = y
        ob_ref[...] = y.astype(BF16)


def _ffn_ln(xb, x, wg, wu, wd, g, b):
    T, D = x.shape
    F = wg.shape[1]
    tm = 512
    nf = 2
    tf = F // nf
    return pl.pallas_call(
        functools.partial(_ffn_kernel, nf=nf),
        grid=(T // tm, nf),
        in_specs=[pl.BlockSpec((tm, D), lambda m, f: (m, 0)),
                  pl.BlockSpec((tm, D), lambda m, f: (m, 0)),
                  pl.BlockSpec((D, tf), lambda m, f: (0, f)),
                  pl.BlockSpec((D, tf), lambda m, f: (0, f)),
                  pl.BlockSpec((tf, D), lambda m, f: (f, 0)),
                  pl.BlockSpec((1, D), lambda m, f: (0, 0)),
                  pl.BlockSpec((1, D), lambda m, f: (0, 0))],
        out_specs=[pl.BlockSpec((tm, D), lambda m, f: (m, 0)), pl.BlockSpec((tm, D), lambda m, f: (m, 0))],
        out_shape=[jax.ShapeDtypeStruct((T, D), F32), jax.ShapeDtypeStruct((T, D), BF16)],
        scratch_shapes=[pltpu.VMEM((tm, D), F32)],
        compiler_params=_params("arbitrary", "arbitrary"),
        name="ffn_ln",
    )(xb, x, wg, wu, wd, g, b)


def _router_kernel(x_ref, wrt_ref, tri_ref, g_ref, pos_ref, cnt_ref):
    tt = x_ref.shape[0]
    logits = lax.dot_general(wrt_ref[...], x_ref[...], NT_DIMS, precision=HI, preferred_element_type=F32)
    row = lax.broadcasted_iota(jnp.int32, logits.shape, 0).astype(F32)
    m1 = jnp.max(logits, axis=0, keepdims=True)
    i1 = jnp.min(jnp.where(logits == m1, row, float(N_EXPERTS)), axis=0, keepdims=True)
    rest = jnp.where(row == i1, -jnp.inf, logits)
    m2 = jnp.max(rest, axis=0, keepdims=True)
    i2 = jnp.min(jnp.where(rest == m2, row, float(N_EXPERTS)), axis=0, keepdims=True)
    e = jnp.exp(m2 - m1)
    g1 = 1.0 / (1.0 + e)
    g2 = e / (1.0 + e)
    g_ref[0] = jnp.where(row == i1, g1, jnp.where(row == i2, g2, 0.0))
    ind = jnp.where(row == i1, 1.0, jnp.where(row == i2, 1.0, 0.0))
    csum = jnp.dot(ind.astype(BF16), tri_ref[...], preferred_element_type=F32)
    pos_ref[0] = jnp.where(ind > 0.5, csum - 1.0, -1.0)
    cnt_ref[0] = jnp.broadcast_to(csum[:, tt - 1:tt], (N_EXPERTS, LANES))


def _router(x, wrt, tri):
    T, D = x.shape
    tt = MOE_TILE
    nt = T // tt
    tile = lambda: pl.BlockSpec((1, N_EXPERTS, tt), lambda m: (m, 0, 0))
    return pl.pallas_call(
        _router_kernel,
        grid=(nt,),
        in_specs=[pl.BlockSpec((tt, D), lambda m: (m, 0)), pl.BlockSpec((N_EXPERTS, D), lambda m: (0, 0)),
                  pl.BlockSpec((tt, tt), lambda m: (0, 0))],
        out_specs=[tile(), tile(), pl.BlockSpec((1, N_EXPERTS, LANES), lambda m: (m, 0, 0))],
        out_shape=[jax.ShapeDtypeStruct((nt, N_EXPERTS, tt), F32), jax.ShapeDtypeStruct((nt, N_EXPERTS, tt), F32),
                   jax.ShapeDtypeStruct((nt, N_EXPERTS, LANES), F32)],
        compiler_params=_params("arbitrary"),
        name="moe_router",
    )(x, wrt, tri)


def _moe_kernel(cnt_ref, xb_ref, x_ref, gate_ref, pos_ref, wg_ref, wu_ref, wd_ref, g_ref, b_ref,
                o_ref, xg_ref, yacc_ref, gcol_ref, *, nf):
    t = pl.program_id(0)
    e = pl.program_id(1)
    f = pl.program_id(2)
    tt = xb_ref.shape[0]
    R = MOE_ROWS
    half = R // 2
    n = cnt_ref[t * N_EXPERTS + e]
    rem = n % R
    n_full = n // R + jnp.where(rem > half, 1, 0)
    has_tail = jnp.logical_and(rem > 0, rem <= half)
    tail_start = pl.multiple_of(n_full * R, half)
    pos_row = pos_ref[0, pl.ds(e, 1), :]

    def selection(start, rows):
        slot = lax.broadcasted_iota(jnp.int32, (rows, tt), 0) + start
        return jnp.where(pos_row == slot.astype(F32), 1.0, 0.0)

    def gather(start, rows):
        sel = selection(start, rows)
        dst = pl.ds(start, rows)
        xg_ref[dst, :] = jnp.dot(sel.astype(BF16), xb_ref[...], preferred_element_type=F32).astype(BF16)
        gate_row = gate_ref[0, pl.ds(e, 1), :]
        gcol_ref[dst, :] = jnp.broadcast_to(jnp.sum(sel * gate_row, axis=1, keepdims=True), (rows, LANES))

    def expert(start, rows):
        dst = pl.ds(start, rows)
        xc = xg_ref[dst, :]
        hg = jnp.dot(xc, wg_ref[0], preferred_element_type=F32)
        hu = jnp.dot(xc, wu_ref[0], preferred_element_type=F32)
        contrib = jnp.dot((hg * _sigmoid(hg) * hu).astype(BF16), wd_ref[0], preferred_element_type=F32)

        @pl.when(f == 0)
        def _():
            yacc_ref[dst, :] = contrib

        @pl.when(f > 0)
        def _():
            yacc_ref[dst, :] = yacc_ref[dst, :] + contrib

        @pl.when(f == nf - 1)
        def _():
            gate = jnp.concatenate([gcol_ref[dst, :]] * (o_ref.shape[1] // LANES), axis=1)
            ys = (yacc_ref[dst, :] * gate).astype(BF16)
            o_ref[...] = o_ref[...] + lax.dot_general(selection(start, rows).astype(BF16), ys, TN_DIMS,
                                                      preferred_element_type=F32)

    def over_chunks(fn):
        def body(r, carry):
            fn(pl.multiple_of(r * R, R), R)
            return carry
        lax.fori_loop(0, n_full, body, 0)

        @pl.when(has_tail)
        def _():
            fn(tail_start, half)

    @pl.when(jnp.logical_and(e == 0, f == 0))
    def _():
        o_ref[...] = jnp.zeros_like(o_ref)

    @pl.when(f == 0)
    def _():
        over_chunks(gather)

    over_chunks(expert)

    @pl.when(jnp.logical_and(e == N_EXPERTS - 1, f == nf - 1))
    def _():
        o_ref[...] = _layer_norm(DEEPNORM_ALPHA * x_ref[...] + o_ref[...], g_ref[...], b_ref[...])


def _moe_ln(xb, x, gates_t, pos_t, counts, wg, wu, wd, g, b):
    T, D = xb.shape
    F = wg.shape[2]
    tt = MOE_TILE
    nf = 2
    tf = F // nf
    grid_spec = pltpu.PrefetchScalarGridSpec(
        num_scalar_prefetch=1,
        grid=(T // tt, N_EXPERTS, nf),
        in_specs=[pl.BlockSpec((tt, D), lambda m, e, f, c: (m, 0)),
                  pl.BlockSpec((tt, D), lambda m, e, f, c: (m, 0)),
                  pl.BlockSpec((1, N_EXPERTS, tt), lambda m, e, f, c: (m, 0, 0)),
                  pl.BlockSpec((1, N_EXPERTS, tt), lambda m, e, f, c: (m, 0, 0)),
                  pl.BlockSpec((1, D, tf), lambda m, e, f, c: (e, 0, f)),
                  pl.BlockSpec((1, D, tf), lambda m, e, f, c: (e, 0, f)),
                  pl.BlockSpec((1, tf, D), lambda m, e, f, c: (e, f, 0)),
                  pl.BlockSpec((1, D), lambda m, e, f, c: (0, 0)),
                  pl.BlockSpec((1, D), lambda m, e, f, c: (0, 0))],
        out_specs=pl.BlockSpec((tt, D), lambda m, e, f, c: (m, 0)),
        scratch_shapes=[pltpu.VMEM((MOE_ROWS_MAX, D), BF16), pltpu.VMEM((MOE_ROWS_MAX, D), F32),
                        pltpu.VMEM((MOE_ROWS_MAX, LANES), F32)],
    )
    return pl.pallas_call(
        functools.partial(_moe_kernel, nf=nf),
        grid_spec=grid_spec,
        out_shape=jax.ShapeDtypeStruct((T, D), F32),
        compiler_params=_params("arbitrary", "arbitrary", "arbitrary"),
        name="moe_experts",
    )(counts, xb, x, gates_t, pos_t, wg, wu, wd, g, b)


def _block_diag(w):
    n, d, _ = w.shape
    out = jnp.zeros((n * d, n * d), w.dtype)
    for i in range(n):
        out = out.at[i * d:(i + 1) * d, i * d:(i + 1) * d].set(w[i])
    return out


def _rope_tables(positions):
    B, S = positions.shape
    half = ROPE_DIMS // 2
    inv_freq = ROPE_THETA ** (-jnp.arange(0, ROPE_DIMS, 2, dtype=F32) / ROPE_DIMS)
    d = jnp.arange(LANES) % HEAD_DIM
    ang = positions.astype(F32).reshape(B * S, 1) * inv_freq[d % half][None, :]
    cos = jnp.cos(ang)
    sin = jnp.sin(ang)
    c = jnp.where(d < ROPE_DIMS, cos, 1.0)
    s1 = jnp.where(d < half, -sin, 0.0)
    s2 = jnp.where((d >= half) & (d < ROPE_DIMS), sin, 0.0)
    return c, s1, s2


def _mixer_layer(x, xb, rope, B, S, w_in, conv_w, conv_b, ga_w, ga_b, gx_w, gx_b, lam,
                 mu, w0, w_up, a0, a_up, g_up, k_k, k_a, r_k, lnx_g, lnx_b, w_out, ln_g, ln_b):
    W = RWKV_WIDTH
    row = lambda t: t.reshape(1, -1)
    proj = _in_proj(xb, w_in.astype(BF16))
    q, k, vt, pen = _attn_prep(proj, *rope, B, S)
    att = _attention(q, k, vt, pen, B, S)
    lru = _rglru(proj, conv_w, row(conv_b), _block_diag(ga_w), row(ga_b), _block_diag(gx_w), row(gx_b), row(lam), B, S)
    head_ones = _block_diag(jnp.ones((RWKV_HEADS, HEAD_DIM, HEAD_DIM), F32))
    wlr = jnp.zeros((W, 3 * W), F32)
    wlr = wlr.at[0:64, 0:W].set(w_up).at[64:128, W:2 * W].set(a_up).at[128:256, 2 * W:3 * W].set(g_up)
    r, ld, k2, v, kn, a, g = _rwkv_prep(proj, row(mu), wlr, row(w0), row(a0), row(k_k), row(k_a), head_ones, B, S)
    y = _wkv(r, ld, k2, v, kn, a, B, S)
    rwk = _wkv_post(y, r, k2, v, g, row(r_k), row(lnx_g), row(lnx_b), head_ones.astype(BF16))
    wo = w_out.astype(BF16)
    return _outproj_ln(att, lru, rwk, x, wo[0:ATT_WIDTH], wo[ATT_WIDTH:ATT_WIDTH + LRU_WIDTH],
                       wo[ATT_WIDTH + LRU_WIDTH:], row(ln_g), row(ln_b))


def kernel(x, positions, w_in, lru_conv_w, lru_conv_b, lru_ga_w, lru_ga_b, lru_gx_w, lru_gx_b, lru_lambda, rwkv_mu, rwkv_w0, rwkv_w_up, rwkv_a0, rwkv_a_up, rwkv_g_up, rwkv_k_k, rwkv_k_a, rwkv_r_k, rwkv_lnx_g, rwkv_lnx_b, w_out, ln1_g, ln1_b, ffn_w_gate, ffn_w_up, ffn_w_down, moe_router, moe_w_gate, moe_w_up, moe_w_down, ln2_g, ln2_b):
    B, S, D = x.shape
    T = B * S
    rope = _rope_tables(positions)
    xf = x.reshape(T, D)
    xb = xf.astype(BF16)
    for l in range(DEPTH):
        xf, xb = _mixer_layer(xf, xb, rope, B, S, w_in[l], lru_conv_w[l], lru_conv_b[l], lru_ga_w[l], lru_ga_b[l],
                              lru_gx_w[l], lru_gx_b[l], lru_lambda[l], rwkv_mu[l], rwkv_w0[l], rwkv_w_up[l],
                              rwkv_a0[l], rwkv_a_up[l], rwkv_g_up[l], rwkv_k_k[l], rwkv_k_a[l], rwkv_r_k[l],
                              rwkv_lnx_g[l], rwkv_lnx_b[l], w_out[l], ln1_g[l], ln1_b[l])
        g2 = ln2_g[l].reshape(1, D)
        b2 = ln2_b[l].reshape(1, D)
        if l % 2 == 0:
            i = l // 2
            xf, xb = _ffn_ln(xb, xf, ffn_w_gate[i].astype(BF16), ffn_w_up[i].astype(BF16),
                             ffn_w_down[i].astype(BF16), g2, b2)
        else:
            i = l // 2
            idx = jnp.arange(MOE_TILE)
            tri = (idx[:, None] <= idx[None, :]).astype(BF16)
            gates_t, pos_t, cnt = _router(xf, moe_router[i].T, tri)
            counts = cnt[:, :, 0].astype(jnp.int32).reshape(-1)
            xf = _moe_ln(xb, xf, gates_t, pos_t, counts, moe_w_gate[i].astype(BF16), moe_w_up[i].astype(BF16),
                         moe_w_down[i].astype(BF16), g2, b2)
    return xf.reshape(B, S, D)
```
